```python
import math
import jax, jax.numpy as jnp
from jax import lax
import numpy as np

D_MODEL = 1024
BATCH = 2
SEQ = 8192
DEPTH = 4

MIX_WIDTH = D_MODEL
DA_WIDTH = MIX_WIDTH // 2
DA_HEADS = 4
DA_DV = DA_WIDTH // DA_HEADS
DA_DK = DA_DV // 2
DA_QBLOCK = 128
ML_WIDTH = MIX_WIDTH - DA_WIDTH
ML_HEADS = 4
ML_DH = ML_WIDTH // ML_HEADS
ML_CHUNK = 64
CONV_K = 4
MEM_LEN = 256
MEM_HEADS = 4
MEM_DH = D_MODEL // MEM_HEADS
PEER_HEADS = 8
PEER_NKEYS = 128
PEER_N = PEER_NKEYS * PEER_NKEYS
PEER_DK = 256
PEER_TOPK = 16
PEER_TOKBLOCK = 128
ALPHA = (2.0 * DEPTH) ** 0.25
BETA = (8.0 * DEPTH) ** -0.25
LN_EPS = 1e-5
IN_SIZES = (DA_HEADS * 2 * DA_DK, DA_HEADS * 2 * DA_DK, DA_HEADS * DA_DV,
            ML_WIDTH, ML_WIDTH, ML_WIDTH, ML_WIDTH, ML_HEADS, ML_HEADS)
P_IN = sum(IN_SIZES)

kernel_name = 'hymba_diffattn_mlstm_peer_deepnorm'


def layer_norm(x, g, b):
    xf = x.astype(jnp.float32)
    mu = jnp.mean(xf, axis=-1, keepdims=True)
    var = jnp.mean(jnp.square(xf - mu), axis=-1, keepdims=True)
    return ((xf - mu) * lax.rsqrt(var + LN_EPS) * g + b).astype(x.dtype)


def causal_dwconv(x, w, b):
    y = lax.conv_general_dilated(x, w[:, None, :], window_strides=(1,),
                                 padding=[(CONV_K - 1, 0)],
                                 dimension_numbers=('NWC', 'WIO', 'NWC'),
                                 feature_group_count=x.shape[-1])
    return y + b


def diff_attention(q, k, v, lam, lam_init, norm_g):
    B, S, H, _, dk = q.shape
    nq = S // DA_QBLOCK
    q = q * (dk ** -0.5)
    qb = q.reshape(B, nq, DA_QBLOCK, H, 2, dk).transpose(1, 0, 3, 4, 2, 5)
    kt = k.transpose(0, 2, 3, 1, 4)
    vt = v.transpose(0, 2, 1, 3)
    kpos = jnp.arange(S)

    def block(args):
        qblk, start = args
        s = jnp.einsum('bhcqd,bhckd->bhcqk', qblk, kt).astype(jnp.float32)
        qpos = start + jnp.arange(DA_QBLOCK)
        mask = kpos[None, :] <= qpos[:, None]
        p = jax.nn.softmax(jnp.where(mask, s, -jnp.inf), axis=-1)
        a = p[:, :, 0] - lam * p[:, :, 1]
        return jnp.einsum('bhqk,bhkv->bhqv', a.astype(vt.dtype), vt)

    o = lax.map(block, (qb, jnp.arange(nq) * DA_QBLOCK))
    o = o.transpose(1, 0, 3, 2, 4).reshape(B, S, H, -1).astype(jnp.float32)
    o = o * lax.rsqrt(jnp.mean(jnp.square(o), axis=-1, keepdims=True) + LN_EPS)
    o = o * norm_g * (1.0 - lam_init)
    return o.reshape(B, S, H * o.shape[-1])


def mlstm_chunkwise(q, k, v, i_pre, f_pre):
    B, S, H, dh = q.shape
    L = ML_CHUNK
    nc = S // L
    f32 = jnp.float32

    def to_chunks(t):
        t = t.astype(f32).reshape((B, nc, L, H) + t.shape[3:])
        return jnp.moveaxis(t, (1, 3), (0, 2))

    qc = to_chunks(q)
    kc = to_chunks(k) * (dh ** -0.5)
    vc = to_chunks(v)
    ic = to_chunks(i_pre)
    bc = jnp.cumsum(jax.nn.log_sigmoid(to_chunks(f_pre)), axis=-1)
    causal = jnp.tril(jnp.ones((L, L), dtype=bool))

    def step(carry, xs):
        C, n, m = carry
        qt, kt, vt, it, bt = xs
        a = bt + m[..., None]
        d = jnp.where(causal, bt[..., :, None] - bt[..., None, :] + it[..., None, :], -jnp.inf)
        m_t = jnp.maximum(a, jnp.max(d, axis=-1))
        w_inter = jnp.exp(a - m_t)
        w_intra = jnp.exp(d - m_t[..., None]) * jnp.einsum('bhtk,bhsk->bhts', qt, kt)
        num = (w_inter[..., None] * jnp.einsum('bhtk,bhkv->bhtv', qt, C)
               + jnp.einsum('bhts,bhsv->bhtv', w_intra, vt))
        nq = w_inter * jnp.einsum('bhtk,bhk->bht', qt, n) + jnp.sum(w_intra, axis=-1)
        h = num / jnp.maximum(jnp.abs(nq), jnp.exp(-m_t))[..., None]
        b_last = bt[..., -1]
        g = b_last[..., None] - bt + it
        m_new = jnp.maximum(b_last + m, jnp.max(g, axis=-1))
        decay = jnp.exp(b_last + m - m_new)
        wk = jnp.exp(g - m_new[..., None])[..., None] * kt
        C_new = decay[..., None, None] * C + jnp.einsum('bhsk,bhsv->bhkv', wk, vt)
        n_new = decay[..., None] * n + jnp.sum(wk, axis=2)
        return (C_new, n_new, m_new), h

    init = (jnp.zeros((B, H, dh, dh), f32), jnp.zeros((B, H, dh), f32), jnp.zeros((B, H), f32))
    _, hs = lax.scan(step, init, (qc, kc, vc, ic, bc))
    return jnp.moveaxis(hs, (0, 2), (1, 3)).reshape(B, S, H, dh)


def hybrid_mixer(x, w_in, i_bias, f_bias, conv_w, conv_b, lam_qk, lam_init, da_g, ml_g, w_out):
    B, S, _ = x.shape
    h = x @ w_in
    offs = np.cumsum(IN_SIZES)[:-1].tolist()
    da_q, da_k, da_v, ml_q, ml_k, ml_v, ml_o, ml_i, ml_f = jnp.split(h, offs, axis=-1)
    lq = lam_qk.astype(jnp.float32)
    lam = jnp.exp(jnp.sum(lq[0] * lq[1])) - jnp.exp(jnp.sum(lq[2] * lq[3])) + lam_init
    da = diff_attention(da_q.reshape(B, S, DA_HEADS, 2, DA_DK),
                        da_k.reshape(B, S, DA_HEADS, 2, DA_DK),
                        da_v.reshape(B, S, DA_HEADS, DA_DV), lam, lam_init, da_g)
    qk = jax.nn.silu(causal_dwconv(jnp.concatenate([ml_q, ml_k], axis=-1), conv_w, conv_b))
    ml_q, ml_k = jnp.split(qk, 2, axis=-1)
    hm = mlstm_chunkwise(ml_q.reshape(B, S, ML_HEADS, ML_DH), ml_k.reshape(B, S, ML_HEADS, ML_DH),
                         ml_v.reshape(B, S, ML_HEADS, ML_DH), ml_i + i_bias, ml_f + f_bias)
    mu = jnp.mean(hm, axis=-1, keepdims=True)
    var = jnp.mean(jnp.square(hm - mu), axis=-1, keepdims=True)
    hm = ((hm - mu) * lax.rsqrt(var + LN_EPS)).reshape(B, S, ML_WIDTH) * ml_g
    hm = hm * jax.nn.sigmoid(ml_o.astype(jnp.float32))
    mixed = jnp.concatenate([da, hm], axis=-1).astype(x.dtype)
    return mixed @ w_out


def memory_attention(x, mem, wq, wkv, wo):
    B, S, D = x.shape
    q = (x @ wq).reshape(B, S, MEM_HEADS, MEM_DH) * (MEM_DH ** -0.5)
    k, v = jnp.split(mem @ wkv, 2, axis=-1)
    k = k.reshape(B, -1, MEM_HEADS, MEM_DH)
    v = v.reshape(B, -1, MEM_HEADS, MEM_DH)
    s = jnp.einsum('bshd,bmhd->bhsm', q, k).astype(jnp.float32)
    p = jax.nn.softmax(s, axis=-1).astype(x.dtype)
    o = jnp.einsum('bhsm,bmhd->bshd', p, v).reshape(B, S, D)
    return o @ wo


def peer_ffn(x, w_pq, sub_keys, u_tab, v_tab):
    B, S, D = x.shape
    half = PEER_DK // 2
    xt = x.reshape(-1, PEER_TOKBLOCK, D)

    def block(xc):
        T = xc.shape[0]
        q = (xc @ w_pq).reshape(T, PEER_HEADS, 2, half)
        s = jnp.einsum('thcd,cnd->thcn', q, sub_keys).astype(jnp.float32)
        sc, idx = lax.top_k(s, PEER_TOPK)
        cand = sc[:, :, 0, :, None] + sc[:, :, 1, None, :]
        cs, ci = lax.top_k(cand.reshape(T, PEER_HEADS, PEER_TOPK * PEER_TOPK), PEER_TOPK)
        e = (jnp.take_along_axis(idx[:, :, 0], ci // PEER_TOPK, axis=-1) * PEER_NKEYS
             + jnp.take_along_axis(idx[:, :, 1], ci % PEER_TOPK, axis=-1))
        g = jax.nn.softmax(cs, axis=-1)
        act = jax.nn.gelu(jnp.einsum('td,thkd->thk', xc, u_tab[e]).astype(jnp.float32),
                          approximate=False)
        return jnp.einsum('thk,thkd->td', (g * act).astype(xc.dtype), v_tab[e])

    return lax.map(block, xt).reshape(B, S, D)


def setup_inputs(seed: int = 0) -> dict:
    key = jax.random.key(seed)
    ks = jax.random.split(key, 24)
    f32 = jnp.float32

    def nrm(k, shape, scale):
        return jax.random.normal(k, shape, f32) * scale

    def gain(k, shape):
        return 1.0 + nrm(k, shape, 0.02)

    return {
        'x': nrm(ks[0], (BATCH, SEQ, D_MODEL), 1.0),
        'mem': nrm(ks[1], (BATCH, MEM_LEN, D_MODEL), 1.0),
        'w_in': nrm(ks[2], (DEPTH, D_MODEL, P_IN), D_MODEL ** -0.5),
        'i_bias': nrm(ks[3], (DEPTH, ML_HEADS), 0.1),
        'f_bias': jnp.broadcast_to(jnp.linspace(3.0, 6.0, ML_HEADS, dtype=f32), (DEPTH, ML_HEADS))
                  + nrm(ks[4], (DEPTH, ML_HEADS), 0.01),
        'conv_w': nrm(ks[5], (DEPTH, CONV_K, 2 * ML_WIDTH), CONV_K ** -0.5),
        'conv_b': nrm(ks[6], (DEPTH, 2 * ML_WIDTH), 0.01),
        'lam_qk': nrm(ks[7], (DEPTH, 4, DA_DK), 0.1),
        'da_norm_g': gain(ks[8], (DEPTH, DA_DV)),
        'ml_norm_g': gain(ks[9], (DEPTH, ML_WIDTH)),
        'w_out': nrm(ks[10], (DEPTH, MIX_WIDTH, D_MODEL), MIX_WIDTH ** -0.5 * BETA),
        'ln1_g': gain(ks[11], (DEPTH, D_MODEL)),
        'ln1_b': nrm(ks[12], (DEPTH, D_MODEL), 0.01),
        'wq_mem': nrm(ks[13], (DEPTH, D_MODEL, D_MODEL), D_MODEL ** -0.5),
        'wkv_mem': nrm(ks[14], (DEPTH, D_MODEL, 2 * D_MODEL), D_MODEL ** -0.5),
        'wo_mem': nrm(ks[15], (DEPTH, D_MODEL, D_MODEL), D_MODEL ** -0.5 * BETA),
        'ln2_g': gain(ks[16], (DEPTH, D_MODEL)),
        'ln2_b': nrm(ks[17], (DEPTH, D_MODEL), 0.01),
        'w_pq': nrm(ks[18], (DEPTH, D_MODEL, PEER_HEADS * PEER_DK), D_MODEL ** -0.5),
        'sub_keys': nrm(ks[19], (DEPTH, 2, PEER_NKEYS, PEER_DK // 2), (PEER_DK // 2) ** -0.5),
        'u_tab': nrm(ks[20], (DEPTH, PEER_N, D_MODEL), D_MODEL ** -0.5),
        'v_tab': nrm(ks[21], (DEPTH, PEER_N, D_MODEL), BETA * PEER_HEADS ** -0.5),
        'ln3_g': gain(ks[22], (DEPTH, D_MODEL)),
        'ln3_b': nrm(ks[23], (DEPTH, D_MODEL), 0.01),
    }


def reference(x, mem, w_in, i_bias, f_bias, conv_w, conv_b, lam_qk, da_norm_g, ml_norm_g, w_out,
              ln1_g, ln1_b, wq_mem, wkv_mem, wo_mem, ln2_g, ln2_b, w_pq, sub_keys, u_tab, v_tab,
              ln3_g, ln3_b):
    for l in range(DEPTH):
        lam_init = 0.8 - 0.6 * math.exp(-0.3 * l)
        mix = hybrid_mixer(x, w_in[l], i_bias[l], f_bias[l], conv_w[l], conv_b[l], lam_qk[l],
                           lam_init, da_norm_g[l], ml_norm_g[l], w_out[l])
        x = layer_norm(ALPHA * x + mix, ln1_g[l], ln1_b[l])
        x = layer_norm(ALPHA * x + memory_attention(x, mem, wq_mem[l], wkv_mem[l], wo_mem[l]),
                       ln2_g[l], ln2_b[l])
        x = layer_norm(ALPHA * x + peer_ffn(x, w_pq[l], sub_keys[l], u_tab[l], v_tab[l]),
                       ln3_g[l], ln3_b[l])
    return x
```

```python
import functools
import math

import jax
import jax.numpy as jnp
from jax import lax
from jax.experimental import pallas as pl
from jax.experimental.pallas import tpu as pltpu

F32 = jnp.float32
BF16 = jnp.bfloat16
LN_EPS = 1e-5
NEG_INF = float("-inf")

DA_HEADS = 4
ML_HEADS = 4
MEM_HEADS = 4
PEER_HEADS = 8
PEER_TOPK = 16
CONV_K = 4

VMEM_LIMIT_BYTES = 56 * 1024 * 1024
LANES = 128
SUBLANES = 8

_NT = (((1,), (1,)), ((), ()))


def _cparams(sem):
    return pltpu.CompilerParams(dimension_semantics=sem, vmem_limit_bytes=VMEM_LIMIT_BYTES)


def _layer_norm(y, g, b):
    mu = jnp.mean(y, axis=-1, keepdims=True)
    yc = y - mu
    var = jnp.mean(yc * yc, axis=-1, keepdims=True)
    return yc * lax.rsqrt(var + LN_EPS) * g + b


def _inproj_kernel(x_ref, wb_ref, wf_ref, hb_ref, hf_ref):
    xb = x_ref[...].astype(BF16)
    hb_ref[...] = jnp.dot(xb, wb_ref[...], preferred_element_type=F32).astype(BF16)
    hf_ref[...] = jnp.dot(xb, wf_ref[...], preferred_element_type=F32)


def _inproj(x2d, wb, wf, tm):
    T, D = x2d.shape
    nb, nf = wb.shape[1], wf.shape[1]
    return pl.pallas_call(
        _inproj_kernel,
        grid=(T // tm,),
        in_specs=[pl.BlockSpec((tm, D), lambda i: (i, 0)),
                  pl.BlockSpec((D, nb), lambda i: (0, 0)),
                  pl.BlockSpec((D, nf), lambda i: (0, 0))],
        out_specs=[pl.BlockSpec((tm, nb), lambda i: (i, 0)),
                   pl.BlockSpec((tm, nf), lambda i: (i, 0))],
        out_shape=[jax.ShapeDtypeStruct((T, nb), BF16),
                   jax.ShapeDtypeStruct((T, nf), F32)],
        compiler_params=_cparams(("parallel",)),
        name="inproj",
    )(x2d, wb, wf)


def _mm_kernel(x_ref, w_ref, o_ref):
    o_ref[...] = jnp.dot(x_ref[...].astype(BF16), w_ref[...],
                         preferred_element_type=F32).astype(o_ref.dtype)


def _matmul(x2d, w, tm, out_dtype):
    M, K = x2d.shape
    N = w.shape[1]
    return pl.pallas_call(
        _mm_kernel,
        grid=(M // tm,),
        in_specs=[pl.BlockSpec((tm, K), lambda i: (i, 0)),
                  pl.BlockSpec((K, N), lambda i: (0, 0))],
        out_specs=pl.BlockSpec((tm, N), lambda i: (i, 0)),
        out_shape=jax.ShapeDtypeStruct((M, N), out_dtype),
        compiler_params=_cparams(("parallel",)),
        name="matmul",
    )(x2d, w)


def _da_kernel(q_ref, k_ref, v_ref, lq_ref, g_ref, lami_ref, o_ref,
               m1, l1, a1, m2, l2, a2, *, tq, dk):
    qb = pl.program_id(2)
    dv = q_ref.shape[-1]
    rep = tq // LANES

    lane = lax.broadcasted_iota(jnp.int32, (1, dv), 1)
    qf = q_ref[...].astype(F32) * (dk ** -0.5)
    q1 = jnp.where(lane < dk, qf, 0.0).astype(BF16)
    q2 = jnp.where(lane >= dk, qf, 0.0).astype(BF16)

    m1[...] = jnp.full(m1.shape, NEG_INF, F32)
    m2[...] = jnp.full(m2.shape, NEG_INF, F32)
    l1[...] = jnp.zeros(l1.shape, F32)
    l2[...] = jnp.zeros(l2.shape, F32)
    a1[...] = jnp.zeros(a1.shape, F32)
    a2[...] = jnp.zeros(a2.shape, F32)

    def update(s, vblk, m_ref, l_ref, a_ref):
        m_prev = m_ref[...]
        m_cur = jnp.max(s, axis=1, keepdims=True)
        m_next = jnp.maximum(m_prev, m_cur)
        p = jnp.exp(s - pltpu.repeat(m_next, rep, axis=1))
        alpha = jnp.exp(m_prev - m_next)
        l_ref[...] = alpha * l_ref[...] + jnp.sum(p, axis=1, keepdims=True)
        a_ref[...] = alpha * a_ref[...] + jnp.dot(p.astype(BF16), vblk,
                                                  preferred_element_type=F32)
        m_ref[...] = m_next

    def block(kb, masked):
        off = pl.multiple_of(kb * tq, tq)
        kblk = k_ref[pl.ds(off, tq), :]
        vblk = v_ref[pl.ds(off, tq), :]
        s1 = lax.dot_general(q1, kblk, _NT, preferred_element_type=F32)
        s2 = lax.dot_general(q2, kblk, _NT, preferred_element_type=F32)
        if masked:
            row = lax.broadcasted_iota(jnp.int32, (tq, tq), 0)
            col = lax.broadcasted_iota(jnp.int32, (tq, tq), 1)
            keep = col <= row
            s1 = jnp.where(keep, s1, NEG_INF)
            s2 = jnp.where(keep, s2, NEG_INF)
        update(s1, vblk, m1, l1, a1)
        update(s2, vblk, m2, l2, a2)

    def body(kb, carry):
        block(kb, False)
        return carry

    lax.fori_loop(0, qb, body, 0)
    block(qb, True)

    lq = lq_ref[...]
    lam_init = lami_ref[...][:, :1]
    lam = (jnp.exp(jnp.sum(lq[0:1] * lq[1:2], axis=1, keepdims=True))
           - jnp.exp(jnp.sum(lq[2:3] * lq[3:4], axis=1, keepdims=True)) + lam_init)
    o = a1[...] / l1[...] - lam * (a2[...] / l2[...])
    o = o * lax.rsqrt(jnp.mean(o * o, axis=-1, keepdims=True) + LN_EPS)
    o = o * g_ref[...] * (1.0 - lami_ref[...])
    o_ref[...] = o.astype(o_ref.dtype)


def _diff_attention(hb3, lam_qk, da_g, lami, tq):
    B, S, _ = hb3.shape
    dv = da_g.shape[-1]
    dk = dv // 2
    H = DA_HEADS
    kern = functools.partial(_da_kernel, tq=tq, dk=dk)
    return pl.pallas_call(
        kern,
        grid=(B, H, S // tq),
        in_specs=[pl.BlockSpec((None, tq, dv), lambda b, h, i: (b, i, h)),
                  pl.BlockSpec((None, S, dv), lambda b, h, i: (b, 0, H + h)),
                  pl.BlockSpec((None, S, dv), lambda b, h, i: (b, 0, 2 * H + h)),
                  pl.BlockSpec(lam_qk.shape, lambda b, h, i: (0, 0)),
                  pl.BlockSpec((1, dv), lambda b, h, i: (0, 0)),
                  pl.BlockSpec((1, dv), lambda b, h, i: (0, 0))],
        out_specs=pl.BlockSpec((None, tq, dv), lambda b, h, i: (b, i, h)),
        out_shape=jax.ShapeDtypeStruct((B, S, H * dv), BF16),
        scratch_shapes=[pltpu.VMEM((tq, LANES), F32), pltpu.VMEM((tq, LANES), F32),
                        pltpu.VMEM((tq, dv), F32),
                        pltpu.VMEM((tq, LANES), F32), pltpu.VMEM((tq, LANES), F32),
                        pltpu.VMEM((tq, dv), F32)],
        compiler_params=_cparams(("parallel", "parallel", "arbitrary")),
        name="diff_attention",
    )(hb3, hb3, hb3, lam_qk, da_g, lami)


def _log_sigmoid(x):
    return -(jnp.maximum(-x, 0.0) + jnp.log1p(jnp.exp(-jnp.abs(x))))


def _ml_kernel(qk_ref, og_ref, gt_ref, v_ref, cw_ref, cb_ref, gb_ref, g_ref, out_ref,
               buf, c_scr, n_scr, m_scr, *, L, H, dh):
    c = pl.program_id(1)
    W = H * dh

    @pl.when(c == 0)
    def _():
        buf[0:SUBLANES, :] = jnp.zeros((SUBLANES, buf.shape[1]), F32)
        c_scr[...] = jnp.zeros(c_scr.shape, F32)
        n_scr[...] = jnp.zeros(n_scr.shape, F32)
        m_scr[...] = jnp.zeros(m_scr.shape, F32)

    buf[SUBLANES:SUBLANES + L, :] = qk_ref[...]
    cw = cw_ref[...]
    y = cb_ref[...]
    for j in range(CONV_K):
        s0 = SUBLANES - (CONV_K - 1) + j
        y = y + cw[j:j + 1, :] * buf[s0:s0 + L, :]
    buf[0:SUBLANES, :] = buf[L:L + SUBLANES, :]
    qk = y * jax.nn.sigmoid(y)

    G = gt_ref[...] + gb_ref[...]
    ls = _log_sigmoid(G)
    row = lax.broadcasted_iota(jnp.int32, (L, L), 0)
    col = lax.broadcasted_iota(jnp.int32, (L, L), 1)
    causal = col <= row
    tri = causal.astype(F32)
    tri_t = (row <= col).astype(F32)
    bcol_all = jnp.dot(tri, ls, precision=lax.Precision.HIGHEST,
                       preferred_element_type=F32)
    GT = G.T
    brow_all = jnp.dot(ls.T, tri_t, precision=lax.Precision.HIGHEST,
                       preferred_element_type=F32)

    for h in range(H):
        b_col = bcol_all[:, H + h:H + h + 1]
        i_col = G[:, h:h + 1]
        b_row = brow_all[H + h:H + h + 1, :]
        i_row = GT[h:h + 1, :]
        b_last = bcol_all[L - 1:L, H + h:H + h + 1]
        m_old = m_scr[h:h + 1, 0:1]

        a = b_col + m_old
        d = jnp.where(causal, b_col - b_row + i_row, NEG_INF)
        m_t = jnp.maximum(a, jnp.max(d, axis=1, keepdims=True))
        w_inter = jnp.exp(a - m_t)

        q = qk[:, h * dh:(h + 1) * dh]
        k = qk[:, W + h * dh:W + (h + 1) * dh] * (dh ** -0.5)
        qb = q.astype(BF16)
        kb = k.astype(BF16)
        vh = v_ref[:, h * dh:(h + 1) * dh]
        s = lax.dot_general(qb, kb, _NT, preferred_element_type=F32)
        w_intra = jnp.exp(d - m_t) * s
        c_old = c_scr[h]
        n_old = n_scr[h:h + 1, :]
        num = (w_inter * jnp.dot(qb, c_old.astype(BF16), preferred_element_type=F32)
               + jnp.dot(w_intra.astype(BF16), vh, preferred_element_type=F32))
        qn = jnp.sum(q * n_old, axis=1, keepdims=True)
        nq = w_inter * qn + jnp.sum(w_intra, axis=1, keepdims=True)
        hh = num / jnp.maximum(jnp.abs(nq), jnp.exp(-m_t))

        g_col = b_last - b_col + i_col
        g_row = b_last - b_row + i_row
        m_new = jnp.maximum(b_last + m_old, jnp.max(g_row, axis=1, keepdims=True))
        decay = jnp.exp(b_last + m_old - m_new)
        wk = jnp.exp(g_col - m_new) * k
        c_scr[h] = decay * c_old + jnp.dot(wk.T.astype(BF16), vh,
                                           preferred_element_type=F32)
        n_scr[h:h + 1, :] = decay * n_old + jnp.sum(wk, axis=0, keepdims=True)
        m_scr[h:h + 1, :] = jnp.broadcast_to(m_new, (1, m_scr.shape[1]))

        mu = jnp.mean(hh, axis=-1, keepdims=True)
        hc = hh - mu
        var = jnp.mean(hc * hc, axis=-1, keepdims=True)
        hn = hc * lax.rsqrt(var + LN_EPS)
        gate = jax.nn.sigmoid(og_ref[:, h * dh:(h + 1) * dh])
        out_ref[:, h * dh:(h + 1) * dh] = (
            hn * g_ref[:, h * dh:(h + 1) * dh] * gate).astype(out_ref.dtype)


def _mlstm(hf3, hb3, conv_w, conv_b, gbias, ml_g, L):
    B, S, _ = hf3.shape
    W = ml_g.shape[-1]
    H = ML_HEADS
    dh = W // H
    kern = functools.partial(_ml_kernel, L=L, H=H, dh=dh)
    return pl.pallas_call(
        kern,
        grid=(B, S // L),
        in_specs=[pl.BlockSpec((None, L, 2 * W), lambda b, c: (b, c, 0)),
                  pl.BlockSpec((None, L, W), lambda b, c: (b, c, 2)),
                  pl.BlockSpec((None, L, LANES), lambda b, c: (b, c, 3 * W // LANES)),
                  pl.BlockSpec((None, L, W), lambda b, c: (b, c, 3)),
                  pl.BlockSpec((CONV_K, 2 * W), lambda b, c: (0, 0)),
                  pl.BlockSpec((1, 2 * W), lambda b, c: (0, 0)),
                  pl.BlockSpec((1, LANES), lambda b, c: (0, 0)),
                  pl.BlockSpec((1, W), lambda b, c: (0, 0))],
        out_specs=pl.BlockSpec((None, L, W), lambda b, c: (b, c, 0)),
        out_shape=jax.ShapeDtypeStruct((B, S, W), BF16),
        scratch_shapes=[pltpu.VMEM((L + SUBLANES, 2 * W), F32),
                        pltpu.VMEM((H, dh, dh), F32),
                        pltpu.VMEM((SUBLANES, dh), F32),
                        pltpu.VMEM((SUBLANES, LANES), F32)],
        compiler_params=_cparams(("parallel", "arbitrary")),
        name="mlstm",
    )(hf3, hf3, hf3, hb3, conv_w, conv_b, gbias, ml_g)


def _outproj_kernel(x_ref, da_ref, hm_ref, w_ref, g_ref, b_ref, o_ref, *, alpha):
    wd = da_ref.shape[-1]
    mix = (jnp.dot(da_ref[...], w_ref[0:wd, :], preferred_element_type=F32)
           + jnp.dot(hm_ref[...], w_ref[wd:, :], preferred_element_type=F32))
    o_ref[...] = _layer_norm(alpha * x_ref[...] + mix, g_ref[...], b_ref[...])


def _outproj_ln(x2d, da2d, hm2d, w_out, g, b, alpha, tm):
    T, D = x2d.shape
    wd, wm = da2d.shape[1], hm2d.shape[1]
    kern = functools.partial(_outproj_kernel, alpha=alpha)
    return pl.pallas_call(
        kern,
        grid=(T // tm,),
        in_specs=[pl.BlockSpec((tm, D), lambda i: (i, 0)),
                  pl.BlockSpec((tm, wd), lambda i: (i, 0)),
                  pl.BlockSpec((tm, wm), lambda i: (i, 0)),
                  pl.BlockSpec((wd + wm, D), lambda i: (0, 0)),
                  pl.BlockSpec((1, D), lambda i: (0, 0)),
                  pl.BlockSpec((1, D), lambda i: (0, 0))],
        out_specs=pl.BlockSpec((tm, D), lambda i: (i, 0)),
        out_shape=jax.ShapeDtypeStruct((T, D), F32),
        compiler_params=_cparams(("parallel",)),
        name="outproj_ln",
    )(x2d, da2d, hm2d, w_out, g, b)


def _memattn_kernel(x_ref, kv_ref, wq_ref, wo_ref, g_ref, b_ref, o_ref, *, alpha, H):
    x = x_ref[...]
    D = x.shape[-1]
    dh = D // H
    q = jnp.dot(x.astype(BF16), wq_ref[...], preferred_element_type=F32) * (dh ** -0.5)
    qb = q.astype(BF16)
    outs = []
    for h in range(H):
        kh = kv_ref[:, h * dh:(h + 1) * dh]
        vh = kv_ref[:, D + h * dh:D + (h + 1) * dh]
        s = lax.dot_general(qb[:, h * dh:(h + 1) * dh], kh, _NT, preferred_element_type=F32)
        s = s - jnp.max(s, axis=-1, keepdims=True)
        e = jnp.exp(s)
        p = e / jnp.sum(e, axis=-1, keepdims=True)
        outs.append(jnp.dot(p.astype(BF16), vh, preferred_element_type=F32))
    o = jnp.concatenate(outs, axis=-1).astype(BF16)
    att = jnp.dot(o, wo_ref[...], preferred_element_type=F32)
    o_ref[...] = _layer_norm(alpha * x + att, g_ref[...], b_ref[...])


def _memattn_ln(x3, kv3, wq, wo, g, b, alpha, tm):
    B, S, D = x3.shape
    M = kv3.shape[1]
    kern = functools.partial(_memattn_kernel, alpha=alpha, H=MEM_HEADS)
    return pl.pallas_call(
        kern,
        grid=(B, S // tm),
        in_specs=[pl.BlockSpec((None, tm, D), lambda bb, i: (bb, i, 0)),
                  pl.BlockSpec((None, M, 2 * D), lambda bb, i: (bb, 0, 0)),
                  pl.BlockSpec((D, D), lambda bb, i: (0, 0)),
                  pl.BlockSpec((D, D), lambda bb, i: (0, 0)),
                  pl.BlockSpec((1, D), lambda bb, i: (0, 0)),
                  pl.BlockSpec((1, D), lambda bb, i: (0, 0))],
        out_specs=pl.BlockSpec((None, tm, D), lambda bb, i: (bb, i, 0)),
        out_shape=jax.ShapeDtypeStruct((B, S, D), F32),
        compiler_params=_cparams(("parallel", "parallel")),
        name="memattn_ln",
    )(x3, kv3, wq, wo, g, b)


def _extract_desc(v, count):
    tops = []
    for r in range(count):
        mx = jnp.max(v, axis=0, keepdims=True)
        tops.append(mx)
        if r + 1 < count:
            v = jnp.where(v == mx, NEG_INF, v)
    return tops


def _route_kernel(x_ref, wpq_ref, sk_ref, thr_ref, e1_ref, s2_ref, e2_ref, q_scr, *, H, nk, topk):
    q_scr[...] = jnp.dot(x_ref[...].astype(BF16), wpq_ref[...], preferred_element_type=F32)
    T = x_ref.shape[0]
    half = sk_ref.shape[-1]
    kk = topk + 1
    sub = lax.broadcasted_iota(jnp.int32, (SUBLANES, T), 0)

    def head(h, carry):
        base = pl.multiple_of(h * 2 * half, 2 * half)
        qa = q_scr[:, pl.ds(base, half)].astype(BF16)
        qb = q_scr[:, pl.ds(base + half, half)].astype(BF16)
        s1 = lax.dot_general(sk_ref[0], qa, _NT, preferred_element_type=F32)
        s2 = lax.dot_general(sk_ref[1], qb, _NT, preferred_element_type=F32)
        a = _extract_desc(s1, kk)
        b = _extract_desc(s2, kk)
        b_arr = jnp.concatenate(b + [jnp.full((1, T), NEG_INF, F32)] * ((-kk) % SUBLANES), axis=0)
        slabs = []
        for p in range(kk):
            nq = kk // (p + 1)
            for s0 in range(0, nq, SUBLANES):
                blk = a[p] + b_arr[s0:s0 + SUBLANES, :]
                if nq - s0 < SUBLANES:
                    blk = jnp.where(sub < (nq - s0), blk, NEG_INF)
                slabs.append(blk)
        cand = jnp.concatenate(slabs, axis=0)
        cs = _extract_desc(cand, kk)
        tau = 0.5 * (cs[topk - 1] + cs[topk])
        z = jnp.zeros((1, T), F32)
        for r in range(topk):
            z = z + jnp.exp(cs[r] - cs[0])
        thr_ref[h] = tau - s1
        e1_ref[h] = jnp.exp(s1 - a[0])
        s2_ref[h] = s2
        e2_ref[h] = jnp.exp(s2 - b[0]) / z
        return carry

    lax.fori_loop(0, H, head, 0)


def _peer_route(x2d, wpq, sk, tr):
    T, D = x2d.shape
    H = PEER_HEADS
    nk = sk.shape[1]
    kern = functools.partial(_route_kernel, H=H, nk=nk, topk=PEER_TOPK)
    shp = jax.ShapeDtypeStruct((H, nk, T), F32)
    ospec = pl.BlockSpec((H, nk, tr), lambda i: (0, 0, i))
    return pl.pallas_call(
        kern,
        grid=(T // tr,),
        in_specs=[pl.BlockSpec((tr, D), lambda i: (i, 0)),
                  pl.BlockSpec(wpq.shape, lambda i: (0, 0)),
                  pl.BlockSpec(sk.shape, lambda i: (0, 0, 0))],
        out_specs=[ospec, ospec, ospec, ospec],
        out_shape=[shp, shp, shp, shp],
        scratch_shapes=[pltpu.VMEM((tr, wpq.shape[1]), F32)],
        compiler_params=_cparams(("parallel",)),
        name="peer_route",
    )(x2d, wpq, sk)


def _peer_kernel(x_ref, thr_ref, e1_ref, s2_ref, e2_ref, u_ref, vt_ref, g_ref, b_ref, o_ref,
                 xb_scr, act_scr, w_scr, acc_scr, *, alpha, H, nk, ib, rc):
    step = pl.program_id(1)
    tt = x_ref.shape[0]

    @pl.when(step == 0)
    def _():
        xb_scr[...] = x_ref[...].astype(BF16)
        acc_scr[...] = jnp.zeros(acc_scr.shape, F32)

    act_scr[...] = lax.dot_general(u_ref[...], xb_scr[...], _NT, preferred_element_type=F32)

    for ii in range(ib):
        def rows(r, carry, ii=ii):
            j0 = pl.multiple_of(r * rc, rc)
            e0 = pl.multiple_of(ii * nk + r * rc, rc)
            a = act_scr[pl.ds(e0, rc), :]
            gel = 0.5 * a * (1.0 + lax.erf(a * (2.0 ** -0.5)))
            acc = jnp.zeros((rc, tt), F32)
            for h in range(H):
                thr_b = jnp.broadcast_to(thr_ref[h, ii:ii + 1, :], (rc, tt))
                e1_b = jnp.broadcast_to(e1_ref[h, ii:ii + 1, :], (rc, tt))
                sel = jnp.where(s2_ref[h, pl.ds(j0, rc), :] >= thr_b,
                                e2_ref[h, pl.ds(j0, rc), :], 0.0)
                acc = acc + sel * e1_b
            w_scr[pl.ds(e0, rc), :] = (acc * gel).astype(BF16)
            return carry

        lax.fori_loop(0, nk // rc, rows, 0)

    acc_scr[...] += jnp.dot(vt_ref[...], w_scr[...], preferred_element_type=F32)

    @pl.when(step == pl.num_programs(1) - 1)
    def _():
        y = alpha * x_ref[...] + acc_scr[...].T
        o_ref[...] = _layer_norm(y, g_ref[...], b_ref[...])


def _peer_experts_ln(x2d, thr, e1, s2, e2, u_bf, vt_bf, g, b, alpha, tt, ib):
    T, D = x2d.shape
    H, nk, _ = thr.shape
    ne = ib * nk
    steps = nk // ib
    kern = functools.partial(_peer_kernel, alpha=alpha, H=H, nk=nk, ib=ib, rc=2 * SUBLANES)
    return pl.pallas_call(
        kern,
        grid=(T // tt, steps),
        in_specs=[pl.BlockSpec((tt, D), lambda t, s: (t, 0)),
                  pl.BlockSpec((H, ib, tt), lambda t, s: (0, s, t)),
                  pl.BlockSpec((H, ib, tt), lambda t, s: (0, s, t)),
                  pl.BlockSpec((H, nk, tt), lambda t, s: (0, 0, t)),
                  pl.BlockSpec((H, nk, tt), lambda t, s: (0, 0, t)),
                  pl.BlockSpec((ne, D), lambda t, s: (s, 0)),
                  pl.BlockSpec((D, ne), lambda t, s: (0, s)),
                  pl.BlockSpec((1, D), lambda t, s: (0, 0)),
                  pl.BlockSpec((1, D), lambda t, s: (0, 0))],
        out_specs=pl.BlockSpec((tt, D), lambda t, s: (t, 0)),
        out_shape=jax.ShapeDtypeStruct((T, D), F32),
        scratch_shapes=[pltpu.VMEM((tt, D), BF16),
                        pltpu.VMEM((ne, tt), F32),
                        pltpu.VMEM((ne, tt), BF16),
                        pltpu.VMEM((D, tt), F32)],
        compiler_params=_cparams(("parallel", "arbitrary")),
        name="peer_experts_ln",
    )(x2d, thr, e1, s2, e2, u_bf, vt_bf, g, b)


def _tile(n, pref):
    t = min(n, pref)
    assert n % t == 0, (n, pref)
    return t


def kernel(x, mem, w_in, i_bias, f_bias, conv_w, conv_b, lam_qk, da_norm_g, ml_norm_g, w_out,
           ln1_g, ln1_b, wq_mem, wkv_mem, wo_mem, ln2_g, ln2_b, w_pq, sub_keys, u_tab, v_tab,
           ln3_g, ln3_b):
    B, S, D = x.shape
    depth = w_in.shape[0]
    T = B * S
    M = mem.shape[1]
    alpha = (2.0 * depth) ** 0.25
    wda = DA_HEADS * da_norm_g.shape[-1]
    wml = ml_norm_g.shape[-1]
    assert w_in.shape[-1] == 3 * wda + 4 * wml + 2 * ML_HEADS
    assert wda == wml and wml % LANES == 0 and 2 * ML_HEADS <= LANES

    tm = _tile(T, 256)
    tq = _tile(S, 512)
    L = _tile(S, 128)
    tmem = _tile(S, 256)
    tr = _tile(T, 256)
    tt = _tile(T, 512)
    ib = SUBLANES

    mem2d = mem.reshape(B * M, D)
    o_q, o_k, o_v = 0, wda, 2 * wda
    o_mq = 3 * wda
    o_mk, o_mv, o_mo, o_gt = o_mq + wml, o_mq + 2 * wml, o_mq + 3 * wml, o_mq + 4 * wml

    for l in range(depth):
        lam_init = 0.8 - 0.6 * math.exp(-0.3 * l)
        wl = w_in[l]
        wb = jnp.concatenate([wl[:, o_q:o_mq], wl[:, o_mv:o_mo]], axis=1).astype(BF16)
        wf = jnp.concatenate(
            [wl[:, o_mq:o_mv], wl[:, o_mo:o_gt],
             jnp.pad(wl[:, o_gt:], ((0, 0), (0, LANES - 2 * ML_HEADS)))], axis=1).astype(BF16)
        gbias = jnp.pad(jnp.concatenate([i_bias[l], f_bias[l]]),
                        (0, LANES - 2 * ML_HEADS)).reshape(1, LANES)
        lami = jnp.full((1, da_norm_g.shape[-1]), lam_init, F32)

        x2d = x.reshape(T, D)
        hb, hf = _inproj(x2d, wb, wf, tm)
        hb3 = hb.reshape(B, S, -1)
        hf3 = hf.reshape(B, S, -1)
        da = _diff_attention(hb3, lam_qk[l], da_norm_g[l].reshape(1, -1), lami, tq)
        hm = _mlstm(hf3, hb3, conv_w[l], conv_b[l].reshape(1, -1), gbias,
                    ml_norm_g[l].reshape(1, -1), L)
        x1 = _outproj_ln(x2d, da.reshape(T, -1), hm.reshape(T, -1), w_out[l].astype(BF16),
                         ln1_g[l].reshape(1, D), ln1_b[l].reshape(1, D), alpha, tm)

        kv = _matmul(mem2d, wkv_mem[l].astype(BF16), _tile(B * M, 256), BF16)
        x2 = _memattn_ln(x1.reshape(B, S, D), kv.reshape(B, M, 2 * D), wq_mem[l].astype(BF16),
                         wo_mem[l].astype(BF16), ln2_g[l].reshape(1, D), ln2_b[l].reshape(1, D),
                         alpha, tmem)
        x2d = x2.reshape(T, D)

        thr, e1, s2, e2 = _peer_route(x2d, w_pq[l].astype(BF16), sub_keys[l].astype(BF16), tr)
        x3 = _peer_experts_ln(x2d, thr, e1, s2, e2, u_tab[l].astype(BF16),
                              v_tab[l].T.astype(BF16), ln3_g[l].reshape(1, D),
                              ln3_b[l].reshape(1, D), alpha, tt, ib)
        x = x3.reshape(B, S, D)
    return x
```

```python
import functools
import math

import jax
import jax.numpy as jnp
from jax import lax
from jax.experimental import pallas as pl
from jax.experimental.pallas import tpu as pltpu

F32 = jnp.float32
BF16 = jnp.bfloat16
LN_EPS = 1e-5
NEG_INF = float("-inf")

DA_HEADS = 4
ML_HEADS = 4
MEM_HEADS = 4
PEER_HEADS = 8
PEER_TOPK = 16
CONV_K = 4

VMEM_LIMIT_BYTES = 56 * 1024 * 1024
LANES = 128
SUBLANES = 8

_NT = (((1,), (1,)), ((), ()))


def _cparams(sem):
    return pltpu.CompilerParams(dimension_semantics=sem, vmem_limit_bytes=VMEM_LIMIT_BYTES)


def _layer_norm(y, g, b):
    mu = jnp.mean(y, axis=-1, keepdims=True)
    yc = y - mu
    var = jnp.mean(yc * yc, axis=-1, keepdims=True)
    return yc * lax.rsqrt(var + LN_EPS) * g + b


def _inproj_kernel(x_ref, wb_ref, wf_ref, hb_ref, hf_ref):
    xb = x_ref[...].astype(BF16)
    hb_ref[...] = jnp.dot(xb, wb_ref[...], preferred_element_type=F32).astype(BF16)
    hf_ref[...] = jnp.dot(xb, wf_ref[...], preferred_element_type=F32)


def _inproj(x2d, wb, wf, tm):
    T, D = x2d.shape
    nb, nf = wb.shape[1], wf.shape[1]
    return pl.pallas_call(
        _inproj_kernel,
        grid=(T // tm,),
        in_specs=[pl.BlockSpec((tm, D), lambda i: (i, 0)),
                  pl.BlockSpec((D, nb), lambda i: (0, 0)),
                  pl.BlockSpec((D, nf), lambda i: (0, 0))],
        out_specs=[pl.BlockSpec((tm, nb), lambda i: (i, 0)),
                   pl.BlockSpec((tm, nf), lambda i: (i, 0))],
        out_shape=[jax.ShapeDtypeStruct((T, nb), BF16),
                   jax.ShapeDtypeStruct((T, nf), F32)],
        compiler_params=_cparams(("parallel",)),
        name="inproj",
    )(x2d, wb, wf)


def _mm_kernel(x_ref, w_ref, o_ref):
    o_ref[...] = jnp.dot(x_ref[...].astype(BF16), w_ref[...],
                         preferred_element_type=F32).astype(o_ref.dtype)


def _matmul(x2d, w, tm, out_dtype):
    M, K = x2d.shape
    N = w.shape[1]
    return pl.pallas_call(
        _mm_kernel,
        grid=(M // tm,),
        in_specs=[pl.BlockSpec((tm, K), lambda i: (i, 0)),
                  pl.BlockSpec((K, N), lambda i: (0, 0))],
        out_specs=pl.BlockSpec((tm, N), lambda i: (i, 0)),
        out_shape=jax.ShapeDtypeStruct((M, N), out_dtype),
        compiler_params=_cparams(("parallel",)),
        name="matmul",
    )(x2d, w)


def _da_kernel(q_ref, k_ref, v_ref, lq_ref, g_ref, lami_ref, o_ref,
               m1, l1, a1, m2, l2, a2, *, tq, dk):
    qb = pl.program_id(2)
    dv = q_ref.shape[-1]
    rep = tq // LANES

    lane = lax.broadcasted_iota(jnp.int32, (1, dv), 1)
    qf = q_ref[...].astype(F32) * (dk ** -0.5)
    q1 = jnp.where(lane < dk, qf, 0.0).astype(BF16)
    q2 = jnp.where(lane >= dk, qf, 0.0).astype(BF16)

    m1[...] = jnp.full(m1.shape, NEG_INF, F32)
    m2[...] = jnp.full(m2.shape, NEG_INF, F32)
    l1[...] = jnp.zeros(l1.shape, F32)
    l2[...] = jnp.zeros(l2.shape, F32)
    a1[...] = jnp.zeros(a1.shape, F32)
    a2[...] = jnp.zeros(a2.shape, F32)

    def update(s, vblk, m_ref, l_ref, a_ref):
        m_prev = m_ref[...]
        m_cur = jnp.max(s, axis=1, keepdims=True)
        m_next = jnp.maximum(m_prev, m_cur)
        p = jnp.exp(s - pltpu.repeat(m_next, rep, axis=1))
        alpha = jnp.exp(m_prev - m_next)
        l_ref[...] = alpha * l_ref[...] + jnp.sum(p, axis=1, keepdims=True)
        a_ref[...] = alpha * a_ref[...] + jnp.dot(p.astype(BF16), vblk,
                                                  preferred_element_type=F32)
        m_ref[...] = m_next

    def block(kb, masked):
        off = pl.multiple_of(kb * tq, tq)
        kblk = k_ref[pl.ds(off, tq), :]
        vblk = v_ref[pl.ds(off, tq), :]
        s1 = lax.dot_general(q1, kblk, _NT, preferred_element_type=F32)
        s2 = lax.dot_general(q2, kblk, _NT, preferred_element_type=F32)
        if masked:
            row = lax.broadcasted_iota(jnp.int32, (tq, tq), 0)
            col = lax.broadcasted_iota(jnp.int32, (tq, tq), 1)
            keep = col <= row
            s1 = jnp.where(keep, s1, NEG_INF)
            s2 = jnp.where(keep, s2, NEG_INF)
        update(s1, vblk, m1, l1, a1)
        update(s2, vblk, m2, l2, a2)

    def body(kb, carry):
        block(kb, False)
        return carry

    lax.fori_loop(0, qb, body, 0)
    block(qb, True)

    lq = lq_ref[...]
    lam_init = lami_ref[...][:, :1]
    lam = (jnp.exp(jnp.sum(lq[0:1] * lq[1:2], axis=1, keepdims=True))
           - jnp.exp(jnp.sum(lq[2:3] * lq[3:4], axis=1, keepdims=True)) + lam_init)
    o = a1[...] / l1[...] - lam * (a2[...] / l2[...])
    o = o * lax.rsqrt(jnp.mean(o * o, axis=-1, keepdims=True) + LN_EPS)
    o = o * g_ref[...] * (1.0 - lami_ref[...])
    o_ref[...] = o.astype(o_ref.dtype)


def _diff_attention(hb3, lam_qk, da_g, lami, tq):
    B, S, _ = hb3.shape
    dv = da_g.shape[-1]
    dk = dv // 2
    H = DA_HEADS
    kern = functools.partial(_da_kernel, tq=tq, dk=dk)
    return pl.pallas_call(
        kern,
        grid=(B, H, S // tq),
        in_specs=[pl.BlockSpec((None, tq, dv), lambda b, h, i: (b, i, h)),
                  pl.BlockSpec((None, S, dv), lambda b, h, i: (b, 0, H + h)),
                  pl.BlockSpec((None, S, dv), lambda b, h, i: (b, 0, 2 * H + h)),
                  pl.BlockSpec(lam_qk.shape, lambda b, h, i: (0, 0)),
                  pl.BlockSpec((1, dv), lambda b, h, i: (0, 0)),
                  pl.BlockSpec((1, dv), lambda b, h, i: (0, 0))],
        out_specs=pl.BlockSpec((None, tq, dv), lambda b, h, i: (b, i, h)),
        out_shape=jax.ShapeDtypeStruct((B, S, H * dv), BF16),
        scratch_shapes=[pltpu.VMEM((tq, LANES), F32), pltpu.VMEM((tq, LANES), F32),
                        pltpu.VMEM((tq, dv), F32),
                        pltpu.VMEM((tq, LANES), F32), pltpu.VMEM((tq, LANES), F32),
                        pltpu.VMEM((tq, dv), F32)],
        compiler_params=_cparams(("parallel", "parallel", "arbitrary")),
        name="diff_attention",
    )(hb3, hb3, hb3, lam_qk, da_g, lami)


def _log_sigmoid(x):
    return -(jnp.maximum(-x, 0.0) + jnp.log1p(jnp.exp(-jnp.abs(x))))


def _ml_kernel(qk_ref, og_ref, gt_ref, v_ref, cw_ref, cb_ref, gb_ref, g_ref, out_ref,
               buf, c_scr, n_scr, m_scr, *, L, H, dh):
    c = pl.program_id(1)
    W = H * dh

    @pl.when(c == 0)
    def _():
        buf[0:SUBLANES, :] = jnp.zeros((SUBLANES, buf.shape[1]), F32)
        c_scr[...] = jnp.zeros(c_scr.shape, F32)
        n_scr[...] = jnp.zeros(n_scr.shape, F32)
        m_scr[...] = jnp.zeros(m_scr.shape, F32)

    buf[SUBLANES:SUBLANES + L, :] = qk_ref[...]
    cw = cw_ref[...]
    y = cb_ref[...]
    for j in range(CONV_K):
        s0 = SUBLANES - (CONV_K - 1) + j
        y = y + cw[j:j + 1, :] * buf[s0:s0 + L, :]
    buf[0:SUBLANES, :] = buf[L:L + SUBLANES, :]
    qk = y * jax.nn.sigmoid(y)

    G = gt_ref[...] + gb_ref[...]
    ls = _log_sigmoid(G)
    row = lax.broadcasted_iota(jnp.int32, (L, L), 0)
    col = lax.broadcasted_iota(jnp.int32, (L, L), 1)
    causal = col <= row
    tri = causal.astype(F32)
    tri_t = (row <= col).astype(F32)
    bcol_all = jnp.dot(tri, ls, precision=lax.Precision.HIGHEST,
                       preferred_element_type=F32)
    GT = G.T
    brow_all = jnp.dot(ls.T, tri_t, precision=lax.Precision.HIGHEST,
                       preferred_element_type=F32)

    for h in range(H):
        b_col = bcol_all[:, H + h:H + h + 1]
        i_col = G[:, h:h + 1]
        b_row = brow_all[H + h:H + h + 1, :]
        i_row = GT[h:h + 1, :]
        b_last = bcol_all[L - 1:L, H + h:H + h + 1]
        m_old = m_scr[h:h + 1, 0:1]

        a = b_col + m_old
        d = jnp.where(causal, b_col - b_row + i_row, NEG_INF)
        m_t = jnp.maximum(a, jnp.max(d, axis=1, keepdims=True))
        w_inter = jnp.exp(a - m_t)

        q = qk[:, h * dh:(h + 1) * dh]
        k = qk[:, W + h * dh:W + (h + 1) * dh] * (dh ** -0.5)
        qb = q.astype(BF16)
        kb = k.astype(BF16)
        vh = v_ref[:, h * dh:(h + 1) * dh]
        s = lax.dot_general(qb, kb, _NT, preferred_element_type=F32)
        w_intra = jnp.exp(d - m_t) * s
        c_old = c_scr[h]
        n_old = n_scr[h:h + 1, :]
        num = (w_inter * jnp.dot(qb, c_old.astype(BF16), preferred_element_type=F32)
               + jnp.dot(w_intra.astype(BF16), vh, preferred_element_type=F32))
        qn = jnp.sum(q * n_old, axis=1, keepdims=True)
        nq = w_inter * qn + jnp.sum(w_intra, axis=1, keepdims=True)
        hh = num / jnp.maximum(jnp.abs(nq), jnp.exp(-m_t))

        g_col = b_last - b_col + i_col
        g_row = b_last - b_row + i_row
        m_new = jnp.maximum(b_last + m_old, jnp.max(g_row, axis=1, keepdims=True))
        decay = jnp.exp(b_last + m_old - m_new)
        wk = jnp.exp(g_col - m_new) * k
        c_scr[h] = decay * c_old + jnp.dot(wk.T.astype(BF16), vh,
                                           preferred_element_type=F32)
        n_scr[h:h + 1, :] = decay * n_old + jnp.sum(wk, axis=0, keepdims=True)
        m_scr[h:h + 1, :] = jnp.broadcast_to(m_new, (1, m_scr.shape[1]))

        mu = jnp.mean(hh, axis=-1, keepdims=True)
        hc = hh - mu
        var = jnp.mean(hc * hc, axis=-1, keepdims=True)
        hn = hc * lax.rsqrt(var + LN_EPS)
        gate = jax.nn.sigmoid(og_ref[:, h * dh:(h + 1) * dh])
        out_ref[:, h * dh:(h + 1) * dh] = (
            hn * g_ref[:, h * dh:(h + 1) * dh] * gate).astype(out_ref.dtype)


def _mlstm(hf3, hb3, conv_w, conv_b, gbias, ml_g, L):
    B, S, _ = hf3.shape
    W = ml_g.shape[-1]
    H = ML_HEADS
    dh = W // H
    kern = functools.partial(_ml_kernel, L=L, H=H, dh=dh)
    return pl.pallas_call(
        kern,
        grid=(B, S // L),
        in_specs=[pl.BlockSpec((None, L, 2 * W), lambda b, c: (b, c, 0)),
                  pl.BlockSpec((None, L, W), lambda b, c: (b, c, 2)),
                  pl.BlockSpec((None, L, LANES), lambda b, c: (b, c, 3 * W // LANES)),
                  pl.BlockSpec((None, L, W), lambda b, c: (b, c, 3)),
                  pl.BlockSpec((CONV_K, 2 * W), lambda b, c: (0, 0)),
                  pl.BlockSpec((1, 2 * W), lambda b, c: (0, 0)),
                  pl.BlockSpec((1, LANES), lambda b, c: (0, 0)),
                  pl.BlockSpec((1, W), lambda b, c: (0, 0))],
        out_specs=pl.BlockSpec((None, L, W), lambda b, c: (b, c, 0)),
        out_shape=jax.ShapeDtypeStruct((B, S, W), BF16),
        scratch_shapes=[pltpu.VMEM((L + SUBLANES, 2 * W), F32),
                        pltpu.VMEM((H, dh, dh), F32),
                        pltpu.VMEM((SUBLANES, dh), F32),
                        pltpu.VMEM((SUBLANES, LANES), F32)],
        compiler_params=_cparams(("parallel", "arbitrary")),
        name="mlstm",
    )(hf3, hf3, hf3, hb3, conv_w, conv_b, gbias, ml_g)


def _outproj_kernel(x_ref, da_ref, hm_ref, w_ref, g_ref, b_ref, o_ref, *, alpha):
    wd = da_ref.shape[-1]
    mix = (jnp.dot(da_ref[...], w_ref[0:wd, :], preferred_element_type=F32)
           + jnp.dot(hm_ref[...], w_ref[wd:, :], preferred_element_type=F32))
    o_ref[...] = _layer_norm(alpha * x_ref[...] + mix, g_ref[...], b_ref[...])


def _outproj_ln(x2d, da2d, hm2d, w_out, g, b, alpha, tm):
    T, D = x2d.shape
    wd, wm = da2d.shape[1], hm2d.shape[1]
    kern = functools.partial(_outproj_kernel, alpha=alpha)
    return pl.pallas_call(
        kern,
        grid=(T // tm,),
        in_specs=[pl.BlockSpec((tm, D), lambda i: (i, 0)),
                  pl.BlockSpec((tm, wd), lambda i: (i, 0)),
                  pl.BlockSpec((tm, wm), lambda i: (i, 0)),
                  pl.BlockSpec((wd + wm, D), lambda i: (0, 0)),
                  pl.BlockSpec((1, D), lambda i: (0, 0)),
                  pl.BlockSpec((1, D), lambda i: (0, 0))],
        out_specs=pl.BlockSpec((tm, D), lambda i: (i, 0)),
        out_shape=jax.ShapeDtypeStruct((T, D), F32),
        compiler_params=_cparams(("parallel",)),
        name="outproj_ln",
    )(x2d, da2d, hm2d, w_out, g, b)


def _memattn_kernel(x_ref, kv_ref, wq_ref, wo_ref, g_ref, b_ref, o_ref, *, alpha, H):
    x = x_ref[...]
    D = x.shape[-1]
    dh = D // H
    q = jnp.dot(x.astype(BF16), wq_ref[...], preferred_element_type=F32) * (dh ** -0.5)
    qb = q.astype(BF16)
    outs = []
    for h in range(H):
        kh = kv_ref[:, h * dh:(h + 1) * dh]
        vh = kv_ref[:, D + h * dh:D + (h + 1) * dh]
        s = lax.dot_general(qb[:, h * dh:(h + 1) * dh], kh, _NT, preferred_element_type=F32)
        s = s - jnp.max(s, axis=-1, keepdims=True)
        e = jnp.exp(s)
        p = e / jnp.sum(e, axis=-1, keepdims=True)
        outs.append(jnp.dot(p.astype(BF16), vh, preferred_element_type=F32))
    o = jnp.concatenate(outs, axis=-1).astype(BF16)
    att = jnp.dot(o, wo_ref[...], preferred_element_type=F32)
    o_ref[...] = _layer_norm(alpha * x + att, g_ref[...], b_ref[...])


def _memattn_ln(x3, kv3, wq, wo, g, b, alpha, tm):
    B, S, D = x3.shape
    M = kv3.shape[1]
    kern = functools.partial(_memattn_kernel, alpha=alpha, H=MEM_HEADS)
    return pl.pallas_call(
        kern,
        grid=(B, S // tm),
        in_specs=[pl.BlockSpec((None, tm, D), lambda bb, i: (bb, i, 0)),
                  pl.BlockSpec((None, M, 2 * D), lambda bb, i: (bb, 0, 0)),
                  pl.BlockSpec((D, D), lambda bb, i: (0, 0)),
                  pl.BlockSpec((D, D), lambda bb, i: (0, 0)),
                  pl.BlockSpec((1, D), lambda bb, i: (0, 0)),
                  pl.BlockSpec((1, D), lambda bb, i: (0, 0))],
        out_specs=pl.BlockSpec((None, tm, D), lambda bb, i: (bb, i, 0)),
        out_shape=jax.ShapeDtypeStruct((B, S, D), F32),
        compiler_params=_cparams(("parallel", "parallel")),
        name="memattn_ln",
    )(x3, kv3, wq, wo, g, b)


def _extract_desc(v, count):
    tops = []
    for r in range(count):
        mx = jnp.max(v, axis=0, keepdims=True)
        tops.append(mx)
        if r + 1 < count:
            v = jnp.where(v == mx, NEG_INF, v)
    return tops


def _route_kernel(x_ref, wpq_ref, sk_ref, thr_ref, e1_ref, s2_ref, e2_ref, q_scr, *, H, nk, topk):
    q_scr[...] = jnp.dot(x_ref[...].astype(BF16), wpq_ref[...], preferred_element_type=F32)
    T = x_ref.shape[0]
    half = sk_ref.shape[-1]
    kk = topk + 1
    sub = lax.broadcasted_iota(jnp.int32, (SUBLANES, T), 0)

    def head(h, carry):
        base = pl.multiple_of(h * 2 * half, 2 * half)
        qa = q_scr[:, pl.ds(base, half)].astype(BF16)
        qb = q_scr[:, pl.ds(base + half, half)].astype(BF16)
        s1 = lax.dot_general(sk_ref[0], qa, _NT, preferred_element_type=F32)
        s2 = lax.dot_general(sk_ref[1], qb, _NT, preferred_element_type=F32)
        a = _extract_desc(s1, kk)
        b = _extract_desc(s2, kk)
        b_arr = jnp.concatenate(b + [jnp.full((1, T), NEG_INF, F32)] * ((-kk) % SUBLANES), axis=0)
        slabs = []
        for p in range(kk):
            nq = kk // (p + 1)
            for s0 in range(0, nq, SUBLANES):
                blk = a[p] + b_arr[s0:s0 + SUBLANES, :]
                if nq - s0 < SUBLANES:
                    blk = jnp.where(sub < (nq - s0), blk, NEG_INF)
                slabs.append(blk)
        cand = jnp.concatenate(slabs, axis=0)
        cs = _extract_desc(cand, kk)
        tau = 0.5 * (cs[topk - 1] + cs[topk])
        z = jnp.zeros((1, T), F32)
        for r in range(topk):
            z = z + jnp.exp(cs[r] - cs[0])
        thr_ref[h] = tau - s1
        e1_ref[h] = jnp.exp(s1 - a[0])
        s2_ref[h] = s2
        e2_ref[h] = jnp.exp(s2 - b[0]) / z
        return carry

    lax.fori_loop(0, H, head, 0)


def _peer_route(x2d, wpq, sk, tr):
    T, D = x2d.shape
    H = PEER_HEADS
    nk = sk.shape[1]
    kern = functools.partial(_route_kernel, H=H, nk=nk, topk=PEER_TOPK)
    shp = jax.ShapeDtypeStruct((H, nk, T), F32)
    ospec = pl.BlockSpec((H, nk, tr), lambda i: (0, 0, i))
    return pl.pallas_call(
        kern,
        grid=(T // tr,),
        in_specs=[pl.BlockSpec((tr, D), lambda i: (i, 0)),
                  pl.BlockSpec(wpq.shape, lambda i: (0, 0)),
                  pl.BlockSpec(sk.shape, lambda i: (0, 0, 0))],
        out_specs=[ospec, ospec, ospec, ospec],
        out_shape=[shp, shp, shp, shp],
        scratch_shapes=[pltpu.VMEM((tr, wpq.shape[1]), F32)],
        compiler_params=_cparams(("parallel",)),
        name="peer_route",
    )(x2d, wpq, sk)


def _peer_kernel(x_ref, thr_ref, e1_ref, s2_ref, e2_ref, u_ref, vt_ref, g_ref, b_ref, o_ref,
                 xb_scr, act_scr, w_scr, acc_scr, bthr_scr, be1_scr, *, alpha, H, nk, ib, sub):
    step = pl.program_id(1)
    tt = x_ref.shape[0]
    stages = ib // sub
    se = sub * nk
    rc = 2 * SUBLANES

    @pl.when(step == 0)
    def _():
        xb_scr[...] = x_ref[...].astype(BF16)
        acc_scr[...] = jnp.zeros(acc_scr.shape, F32)

    def act_mm(k):
        act_scr[k % 2] = lax.dot_general(u_ref[k * se:(k + 1) * se, :], xb_scr[...], _NT,
                                         preferred_element_type=F32)

    def out_mm(k):
        acc_scr[...] += jnp.dot(vt_ref[:, k * se:(k + 1) * se], w_scr[k % 2],
                                preferred_element_type=F32)

    def gates(k):
        for half in range(sub):
            ii = k * sub + half
            par = ii % 2
            for h in range(H):
                bthr_scr[par, h] = jnp.broadcast_to(thr_ref[h, ii:ii + 1, :], (SUBLANES, tt))
                be1_scr[par, h] = jnp.broadcast_to(e1_ref[h, ii:ii + 1, :], (SUBLANES, tt))
            for r in range(nk // rc):
                parts = []
                for q in range(rc // SUBLANES):
                    j0 = r * rc + q * SUBLANES
                    e0 = half * nk + j0
                    a = act_scr[k % 2, e0:e0 + SUBLANES, :]
                    gel = 0.5 * a * (1.0 + lax.erf(a * (2.0 ** -0.5)))
                    acc = jnp.zeros((SUBLANES, tt), F32)
                    for h in range(H):
                        sel = jnp.where(s2_ref[h, j0:j0 + SUBLANES, :] >= bthr_scr[par, h],
                                        e2_ref[h, j0:j0 + SUBLANES, :], 0.0)
                        acc = acc + sel * be1_scr[par, h]
                    parts.append(acc * gel)
                e0 = half * nk + r * rc
                w_scr[k % 2, e0:e0 + rc, :] = jnp.concatenate(parts, axis=0).astype(BF16)

    act_mm(0)
    for k in range(stages):
        if k + 1 < stages:
            act_mm(k + 1)
        gates(k)
        if k >= 1:
            out_mm(k - 1)
    out_mm(stages - 1)

    @pl.when(step == pl.num_programs(1) - 1)
    def _():
        y = alpha * x_ref[...] + acc_scr[...].T
        o_ref[...] = _layer_norm(y, g_ref[...], b_ref[...])


def _peer_experts_ln(x2d, thr, e1, s2, e2, u_bf, vt_bf, g, b, alpha, tt, ib):
    T, D = x2d.shape
    H, nk, _ = thr.shape
    ne = ib * nk
    steps = nk // ib
    sub = 2
    kern = functools.partial(_peer_kernel, alpha=alpha, H=H, nk=nk, ib=ib, sub=sub)
    return pl.pallas_call(
        kern,
        grid=(T // tt, steps),
        in_specs=[pl.BlockSpec((tt, D), lambda t, s: (t, 0)),
                  pl.BlockSpec((H, ib, tt), lambda t, s: (0, s, t)),
                  pl.BlockSpec((H, ib, tt), lambda t, s: (0, s, t)),
                  pl.BlockSpec((H, nk, tt), lambda t, s: (0, 0, t)),
                  pl.BlockSpec((H, nk, tt), lambda t, s: (0, 0, t)),
                  pl.BlockSpec((ne, D), lambda t, s: (s, 0)),
                  pl.BlockSpec((D, ne), lambda t, s: (0, s)),
                  pl.BlockSpec((1, D), lambda t, s: (0, 0)),
                  pl.BlockSpec((1, D), lambda t, s: (0, 0))],
        out_specs=pl.BlockSpec((tt, D), lambda t, s: (t, 0)),
        out_shape=jax.ShapeDtypeStruct((T, D), F32),
        scratch_shapes=[pltpu.VMEM((tt, D), BF16),
                        pltpu.VMEM((2, sub * nk, tt), F32),
                        pltpu.VMEM((2, sub * nk, tt), BF16),
                        pltpu.VMEM((D, tt), F32),
                        pltpu.VMEM((2, H, SUBLANES, tt), F32),
                        pltpu.VMEM((2, H, SUBLANES, tt), F32)],
        compiler_params=_cparams(("parallel", "arbitrary")),
        name="peer_experts_ln",
    )(x2d, thr, e1, s2, e2, u_bf, vt_bf, g, b)


def _tile(n, pref):
    t = min(n, pref)
    assert n % t == 0, (n, pref)
    return t


def kernel(x, mem, w_in, i_bias, f_bias, conv_w, conv_b, lam_qk, da_norm_g, ml_norm_g, w_out,
           ln1_g, ln1_b, wq_mem, wkv_mem, wo_mem, ln2_g, ln2_b, w_pq, sub_keys, u_tab, v_tab,
           ln3_g, ln3_b):
    B, S, D = x.shape
    depth = w_in.shape[0]
    T = B * S
    M = mem.shape[1]
    alpha = (2.0 * depth) ** 0.25
    wda = DA_HEADS * da_norm_g.shape[-1]
    wml = ml_norm_g.shape[-1]
    assert w_in.shape[-1] == 3 * wda + 4 * wml + 2 * ML_HEADS
    assert wda == wml and wml % LANES == 0 and 2 * ML_HEADS <= LANES

    tm = _tile(T, 256)
    tq = _tile(S, 512)
    L = _tile(S, 128)
    tmem = _tile(S, 256)
    tr = _tile(T, 256)
    tt = _tile(T, 512)
    ib = 2 * SUBLANES

    mem2d = mem.reshape(B * M, D)
    o_q, o_k, o_v = 0, wda, 2 * wda
    o_mq = 3 * wda
    o_mk, o_mv, o_mo, o_gt = o_mq + wml, o_mq + 2 * wml, o_mq + 3 * wml, o_mq + 4 * wml

    for l in range(depth):
        lam_init = 0.8 - 0.6 * math.exp(-0.3 * l)
        wl = w_in[l]
        wb = jnp.concatenate([wl[:, o_q:o_mq], wl[:, o_mv:o_mo]], axis=1).astype(BF16)
        wf = jnp.concatenate(
            [wl[:, o_mq:o_mv], wl[:, o_mo:o_gt],
             jnp.pad(wl[:, o_gt:], ((0, 0), (0, LANES - 2 * ML_HEADS)))], axis=1).astype(BF16)
        gbias = jnp.pad(jnp.concatenate([i_bias[l], f_bias[l]]),
                        (0, LANES - 2 * ML_HEADS)).reshape(1, LANES)
        lami = jnp.full((1, da_norm_g.shape[-1]), lam_init, F32)

        x2d = x.reshape(T, D)
        hb, hf = _inproj(x2d, wb, wf, tm)
        hb3 = hb.reshape(B, S, -1)
        hf3 = hf.reshape(B, S, -1)
        da = _diff_attention(hb3, lam_qk[l], da_norm_g[l].reshape(1, -1), lami, tq)
        hm = _mlstm(hf3, hb3, conv_w[l], conv_b[l].reshape(1, -1), gbias,
                    ml_norm_g[l].reshape(1, -1), L)
        x1 = _outproj_ln(x2d, da.reshape(T, -1), hm.reshape(T, -1), w_out[l].astype(BF16),
                         ln1_g[l].reshape(1, D), ln1_b[l].reshape(1, D), alpha, tm)

        kv = _matmul(mem2d, wkv_mem[l].astype(BF16), _tile(B * M, 256), BF16)
        x2 = _memattn_ln(x1.reshape(B, S, D), kv.reshape(B, M, 2 * D), wq_mem[l].astype(BF16),
                         wo_mem[l].astype(BF16), ln2_g[l].reshape(1, D), ln2_b[l].reshape(1, D),
                         alpha, tmem)
        x2d = x2.reshape(T, D)

        thr, e1, s2, e2 = _peer_route(x2d, w_pq[l].astype(BF16), sub_keys[l].astype(BF16), tr)
        x3 = _peer_experts_ln(x2d, thr, e1, s2, e2, u_tab[l].astype(BF16),
                              v_tab[l].T.astype(BF16), ln3_g[l].reshape(1, D),
                              ln3_b[l].reshape(1, D), alpha, tt, ib)
        x = x3.reshape(B, S, D)
    return x
```

```python
import functools
import math

import jax
import jax.numpy as jnp
from jax import lax
from jax.experimental import pallas as pl
from jax.experimental.pallas import tpu as pltpu

F32 = jnp.float32
BF16 = jnp.bfloat16
LN_EPS = 1e-5
NEG_INF = float("-inf")

DA_HEADS = 4
ML_HEADS = 4
MEM_HEADS = 4
PEER_HEADS = 8
PEER_TOPK = 16
CONV_K = 4

VMEM_LIMIT_BYTES = 56 * 1024 * 1024
LANES = 128
SUBLANES = 8

_NT = (((1,), (1,)), ((), ()))


def _cparams(sem):
    return pltpu.CompilerParams(dimension_semantics=sem, vmem_limit_bytes=VMEM_LIMIT_BYTES)


def _layer_norm(y, g, b):
    mu = jnp.mean(y, axis=-1, keepdims=True)
    yc = y - mu
    var = jnp.mean(yc * yc, axis=-1, keepdims=True)
    return yc * lax.rsqrt(var + LN_EPS) * g + b


def _inproj_kernel(x_ref, wb_ref, wf_ref, hb_ref, hf_ref):
    xb = x_ref[...].astype(BF16)
    hb_ref[...] = jnp.dot(xb, wb_ref[...], preferred_element_type=F32).astype(BF16)
    hf_ref[...] = jnp.dot(xb, wf_ref[...], preferred_element_type=F32)


def _inproj(x2d, wb, wf, tm):
    T, D = x2d.shape
    nb, nf = wb.shape[1], wf.shape[1]
    return pl.pallas_call(
        _inproj_kernel,
        grid=(T // tm,),
        in_specs=[pl.BlockSpec((tm, D), lambda i: (i, 0)),
                  pl.BlockSpec((D, nb), lambda i: (0, 0)),
                  pl.BlockSpec((D, nf), lambda i: (0, 0))],
        out_specs=[pl.BlockSpec((tm, nb), lambda i: (i, 0)),
                   pl.BlockSpec((tm, nf), lambda i: (i, 0))],
        out_shape=[jax.ShapeDtypeStruct((T, nb), BF16),
                   jax.ShapeDtypeStruct((T, nf), F32)],
        compiler_params=_cparams(("parallel",)),
        name="inproj",
    )(x2d, wb, wf)


def _mm_kernel(x_ref, w_ref, o_ref):
    o_ref[...] = jnp.dot(x_ref[...].astype(BF16), w_ref[...],
                         preferred_element_type=F32).astype(o_ref.dtype)


def _matmul(x2d, w, tm, out_dtype):
    M, K = x2d.shape
    N = w.shape[1]
    return pl.pallas_call(
        _mm_kernel,
        grid=(M // tm,),
        in_specs=[pl.BlockSpec((tm, K), lambda i: (i, 0)),
                  pl.BlockSpec((K, N), lambda i: (0, 0))],
        out_specs=pl.BlockSpec((tm, N), lambda i: (i, 0)),
        out_shape=jax.ShapeDtypeStruct((M, N), out_dtype),
        compiler_params=_cparams(("parallel",)),
        name="matmul",
    )(x2d, w)


def _da_kernel(q_ref, k_ref, v_ref, lq_ref, g_ref, lami_ref, o_ref,
               m1, l1, a1, m2, l2, a2, *, tq, dk):
    qb = pl.program_id(2)
    dv = q_ref.shape[-1]
    rep = tq // LANES

    lane = lax.broadcasted_iota(jnp.int32, (1, dv), 1)
    qf = q_ref[...].astype(F32) * (dk ** -0.5)
    q1 = jnp.where(lane < dk, qf, 0.0).astype(BF16)
    q2 = jnp.where(lane >= dk, qf, 0.0).astype(BF16)

    m1[...] = jnp.full(m1.shape, NEG_INF, F32)
    m2[...] = jnp.full(m2.shape, NEG_INF, F32)
    l1[...] = jnp.zeros(l1.shape, F32)
    l2[...] = jnp.zeros(l2.shape, F32)
    a1[...] = jnp.zeros(a1.shape, F32)
    a2[...] = jnp.zeros(a2.shape, F32)

    def update(s, vblk, m_ref, l_ref, a_ref):
        m_prev = m_ref[...]
        m_cur = jnp.max(s, axis=1, keepdims=True)
        m_next = jnp.maximum(m_prev, m_cur)
        p = jnp.exp(s - jnp.concatenate([m_next] * rep, axis=1))
        alpha = jnp.exp(m_prev - m_next)
        l_ref[...] = alpha * l_ref[...] + jnp.sum(p, axis=1, keepdims=True)
        a_ref[...] = alpha * a_ref[...] + jnp.dot(p.astype(BF16), vblk,
                                                  preferred_element_type=F32)
        m_ref[...] = m_next

    def block(kb, masked):
        off = pl.multiple_of(kb * tq, tq)
        kblk = k_ref[pl.ds(off, tq), :]
        vblk = v_ref[pl.ds(off, tq), :]
        s1 = lax.dot_general(q1, kblk, _NT, preferred_element_type=F32)
        s2 = lax.dot_general(q2, kblk, _NT, preferred_element_type=F32)
        if masked:
            row = lax.broadcasted_iota(jnp.int32, (tq, tq), 0)
            col = lax.broadcasted_iota(jnp.int32, (tq, tq), 1)
            keep = col <= row
            s1 = jnp.where(keep, s1, NEG_INF)
            s2 = jnp.where(keep, s2, NEG_INF)
        update(s1, vblk, m1, l1, a1)
        update(s2, vblk, m2, l2, a2)

    def body(kb, carry):
        block(kb, False)
        return carry

    lax.fori_loop(0, qb, body, 0)
    block(qb, True)

    lq = lq_ref[...]
    lam_init = lami_ref[...][:, :1]
    lam = (jnp.exp(jnp.sum(lq[0:1] * lq[1:2], axis=1, keepdims=True))
           - jnp.exp(jnp.sum(lq[2:3] * lq[3:4], axis=1, keepdims=True)) + lam_init)
    o = a1[...] / l1[...] - lam * (a2[...] / l2[...])
    o = o * lax.rsqrt(jnp.mean(o * o, axis=-1, keepdims=True) + LN_EPS)
    o = o * g_ref[...] * (1.0 - lami_ref[...])
    o_ref[...] = o.astype(o_ref.dtype)


def _diff_attention(hb3, lam_qk, da_g, lami, tq):
    B, S, _ = hb3.shape
    dv = da_g.shape[-1]
    dk = dv // 2
    H = DA_HEADS
    kern = functools.partial(_da_kernel, tq=tq, dk=dk)
    return pl.pallas_call(
        kern,
        grid=(B, H, S // tq),
        in_specs=[pl.BlockSpec((None, tq, dv), lambda b, h, i: (b, i, h)),
                  pl.BlockSpec((None, S, dv), lambda b, h, i: (b, 0, H + h)),
                  pl.BlockSpec((None, S, dv), lambda b, h, i: (b, 0, 2 * H + h)),
                  pl.BlockSpec(lam_qk.shape, lambda b, h, i: (0, 0)),
                  pl.BlockSpec((1, dv), lambda b, h, i: (0, 0)),
                  pl.BlockSpec((1, dv), lambda b, h, i: (0, 0))],
        out_specs=pl.BlockSpec((None, tq, dv), lambda b, h, i: (b, i, h)),
        out_shape=jax.ShapeDtypeStruct((B, S, H * dv), BF16),
        scratch_shapes=[pltpu.VMEM((tq, LANES), F32), pltpu.VMEM((tq, LANES), F32),
                        pltpu.VMEM((tq, dv), F32),
                        pltpu.VMEM((tq, LANES), F32), pltpu.VMEM((tq, LANES), F32),
                        pltpu.VMEM((tq, dv), F32)],
        compiler_params=_cparams(("parallel", "parallel", "arbitrary")),
        name="diff_attention",
    )(hb3, hb3, hb3, lam_qk, da_g, lami)


def _log_sigmoid(x):
    return -(jnp.maximum(-x, 0.0) + jnp.log1p(jnp.exp(-jnp.abs(x))))


def _ml_kernel(qk_ref, og_ref, gt_ref, v_ref, cw_ref, cb_ref, gb_ref, g_ref, out_ref,
               buf, c_scr, n_scr, m_scr, *, L, H, dh):
    c = pl.program_id(1)
    W = H * dh

    @pl.when(c == 0)
    def _():
        buf[0:SUBLANES, :] = jnp.zeros((SUBLANES, buf.shape[1]), F32)
        c_scr[...] = jnp.zeros(c_scr.shape, F32)
        n_scr[...] = jnp.zeros(n_scr.shape, F32)
        m_scr[...] = jnp.zeros(m_scr.shape, F32)

    buf[SUBLANES:SUBLANES + L, :] = qk_ref[...]
    cw = cw_ref[...]
    y = cb_ref[...]
    for j in range(CONV_K):
        s0 = SUBLANES - (CONV_K - 1) + j
        y = y + cw[j:j + 1, :] * buf[s0:s0 + L, :]
    buf[0:SUBLANES, :] = buf[L:L + SUBLANES, :]
    qk = y * jax.nn.sigmoid(y)

    G = gt_ref[...] + gb_ref[...]
    ls = _log_sigmoid(G)
    row = lax.broadcasted_iota(jnp.int32, (L, L), 0)
    col = lax.broadcasted_iota(jnp.int32, (L, L), 1)
    causal = col <= row
    tri = causal.astype(F32)
    tri_t = (row <= col).astype(F32)
    bcol_all = jnp.dot(tri, ls, precision=lax.Precision.HIGHEST,
                       preferred_element_type=F32)
    GT = G.T
    brow_all = jnp.dot(ls.T, tri_t, precision=lax.Precision.HIGHEST,
                       preferred_element_type=F32)

    for h in range(H):
        b_col = bcol_all[:, H + h:H + h + 1]
        i_col = G[:, h:h + 1]
        b_row = brow_all[H + h:H + h + 1, :]
        i_row = GT[h:h + 1, :]
        b_last = bcol_all[L - 1:L, H + h:H + h + 1]
        m_old = m_scr[h:h + 1, 0:1]

        a = b_col + m_old
        d = jnp.where(causal, b_col - b_row + i_row, NEG_INF)
        m_t = jnp.maximum(a, jnp.max(d, axis=1, keepdims=True))
        w_inter = jnp.exp(a - m_t)

        q = qk[:, h * dh:(h + 1) * dh]
        k = qk[:, W + h * dh:W + (h + 1) * dh] * (dh ** -0.5)
        qb = q.astype(BF16)
        kb = k.astype(BF16)
        vh = v_ref[:, h * dh:(h + 1) * dh]
        s = lax.dot_general(qb, kb, _NT, preferred_element_type=F32)
        w_intra = jnp.exp(d - m_t) * s
        c_old = c_scr[h]
        n_old = n_scr[h:h + 1, :]
        num = (w_inter * jnp.dot(qb, c_old.astype(BF16), preferred_element_type=F32)
               + jnp.dot(w_intra.astype(BF16), vh, preferred_element_type=F32))
        qn = jnp.sum(q * n_old, axis=1, keepdims=True)
        nq = w_inter * qn + jnp.sum(w_intra, axis=1, keepdims=True)
        hh = num / jnp.maximum(jnp.abs(nq), jnp.exp(-m_t))

        g_col = b_last - b_col + i_col
        g_row = b_last - b_row + i_row
        m_new = jnp.maximum(b_last + m_old, jnp.max(g_row, axis=1, keepdims=True))
        decay = jnp.exp(b_last + m_old - m_new)
        wk = jnp.exp(g_col - m_new) * k
        c_scr[h] = decay * c_old + jnp.dot(wk.T.astype(BF16), vh,
                                           preferred_element_type=F32)
        n_scr[h:h + 1, :] = decay * n_old + jnp.sum(wk, axis=0, keepdims=True)
        m_scr[h:h + 1, :] = jnp.broadcast_to(m_new, (1, m_scr.shape[1]))

        mu = jnp.mean(hh, axis=-1, keepdims=True)
        hc = hh - mu
        var = jnp.mean(hc * hc, axis=-1, keepdims=True)
        hn = hc * lax.rsqrt(var + LN_EPS)
        gate = jax.nn.sigmoid(og_ref[:, h * dh:(h + 1) * dh])
        out_ref[:, h * dh:(h + 1) * dh] = (
            hn * g_ref[:, h * dh:(h + 1) * dh] * gate).astype(out_ref.dtype)


def _mlstm(hf3, hb3, conv_w, conv_b, gbias, ml_g, L):
    B, S, _ = hf3.shape
    W = ml_g.shape[-1]
    H = ML_HEADS
    dh = W // H
    kern = functools.partial(_ml_kernel, L=L, H=H, dh=dh)
    return pl.pallas_call(
        kern,
        grid=(B, S // L),
        in_specs=[pl.BlockSpec((None, L, 2 * W), lambda b, c: (b, c, 0)),
                  pl.BlockSpec((None, L, W), lambda b, c: (b, c, 2)),
                  pl.BlockSpec((None, L, LANES), lambda b, c: (b, c, 3 * W // LANES)),
                  pl.BlockSpec((None, L, W), lambda b, c: (b, c, 3)),
                  pl.BlockSpec((CONV_K, 2 * W), lambda b, c: (0, 0)),
                  pl.BlockSpec((1, 2 * W), lambda b, c: (0, 0)),
                  pl.BlockSpec((1, LANES), lambda b, c: (0, 0)),
                  pl.BlockSpec((1, W), lambda b, c: (0, 0))],
        out_specs=pl.BlockSpec((None, L, W), lambda b, c: (b, c, 0)),
        out_shape=jax.ShapeDtypeStruct((B, S, W), BF16),
        scratch_shapes=[pltpu.VMEM((L + SUBLANES, 2 * W), F32),
                        pltpu.VMEM((H, dh, dh), F32),
                        pltpu.VMEM((SUBLANES, dh), F32),
                        pltpu.VMEM((SUBLANES, LANES), F32)],
        compiler_params=_cparams(("parallel", "arbitrary")),
        name="mlstm",
    )(hf3, hf3, hf3, hb3, conv_w, conv_b, gbias, ml_g)


def _outproj_kernel(x_ref, da_ref, hm_ref, w_ref, g_ref, b_ref, o_ref, *, alpha):
    wd = da_ref.shape[-1]
    mix = (jnp.dot(da_ref[...], w_ref[0:wd, :], preferred_element_type=F32)
           + jnp.dot(hm_ref[...], w_ref[wd:, :], preferred_element_type=F32))
    o_ref[...] = _layer_norm(alpha * x_ref[...] + mix, g_ref[...], b_ref[...])


def _outproj_ln(x2d, da2d, hm2d, w_out, g, b, alpha, tm):
    T, D = x2d.shape
    wd, wm = da2d.shape[1], hm2d.shape[1]
    kern = functools.partial(_outproj_kernel, alpha=alpha)
    return pl.pallas_call(
        kern,
        grid=(T // tm,),
        in_specs=[pl.BlockSpec((tm, D), lambda i: (i, 0)),
                  pl.BlockSpec((tm, wd), lambda i: (i, 0)),
                  pl.BlockSpec((tm, wm), lambda i: (i, 0)),
                  pl.BlockSpec((wd + wm, D), lambda i: (0, 0)),
                  pl.BlockSpec((1, D), lambda i: (0, 0)),
                  pl.BlockSpec((1, D), lambda i: (0, 0))],
        out_specs=pl.BlockSpec((tm, D), lambda i: (i, 0)),
        out_shape=jax.ShapeDtypeStruct((T, D), F32),
        compiler_params=_cparams(("parallel",)),
        name="outproj_ln",
    )(x2d, da2d, hm2d, w_out, g, b)


def _memattn_kernel(x_ref, kv_ref, wq_ref, wo_ref, g_ref, b_ref, o_ref, *, alpha, H):
    x = x_ref[...]
    D = x.shape[-1]
    dh = D // H
    q = jnp.dot(x.astype(BF16), wq_ref[...], preferred_element_type=F32) * (dh ** -0.5)
    qb = q.astype(BF16)
    outs = []
    for h in range(H):
        kh = kv_ref[:, h * dh:(h + 1) * dh]
        vh = kv_ref[:, D + h * dh:D + (h + 1) * dh]
        s = lax.dot_general(qb[:, h * dh:(h + 1) * dh], kh, _NT, preferred_element_type=F32)
        s = s - jnp.max(s, axis=-1, keepdims=True)
        e = jnp.exp(s)
        p = e / jnp.sum(e, axis=-1, keepdims=True)
        outs.append(jnp.dot(p.astype(BF16), vh, preferred_element_type=F32))
    o = jnp.concatenate(outs, axis=-1).astype(BF16)
    att = jnp.dot(o, wo_ref[...], preferred_element_type=F32)
    o_ref[...] = _layer_norm(alpha * x + att, g_ref[...], b_ref[...])


def _memattn_ln(x3, kv3, wq, wo, g, b, alpha, tm):
    B, S, D = x3.shape
    M = kv3.shape[1]
    kern = functools.partial(_memattn_kernel, alpha=alpha, H=MEM_HEADS)
    return pl.pallas_call(
        kern,
        grid=(B, S // tm),
        in_specs=[pl.BlockSpec((None, tm, D), lambda bb, i: (bb, i, 0)),
                  pl.BlockSpec((None, M, 2 * D), lambda bb, i: (bb, 0, 0)),
                  pl.BlockSpec((D, D), lambda bb, i: (0, 0)),
                  pl.BlockSpec((D, D), lambda bb, i: (0, 0)),
                  pl.BlockSpec((1, D), lambda bb, i: (0, 0)),
                  pl.BlockSpec((1, D), lambda bb, i: (0, 0))],
        out_specs=pl.BlockSpec((None, tm, D), lambda bb, i: (bb, i, 0)),
        out_shape=jax.ShapeDtypeStruct((B, S, D), F32),
        compiler_params=_cparams(("parallel", "parallel")),
        name="memattn_ln",
    )(x3, kv3, wq, wo, g, b)


NO_RANK = 127.0


def _extract_desc(v, count, n_ranked=0):
    tops = []
    rank = jnp.full(v.shape, NO_RANK, F32) if n_ranked else None
    for r in range(count):
        mx = jnp.max(v, axis=0, keepdims=True)
        tops.append(mx)
        hit = v == mx
        if r < n_ranked:
            rank = jnp.where(hit, float(r), rank)
        if r + 1 < count:
            v = jnp.where(hit, NEG_INF, v)
    return tops, rank


def _dup_bf16_words(x):
    u = lax.bitcast_convert_type(x.astype(BF16).astype(F32), jnp.uint32)
    return u | (u >> 16)


def _route_kernel(x_ref, wpq_ref, sk_ref, c1_ref, e1_ref, r2_ref, e2_ref, q_scr, *, H, nk, topk):
    q_scr[...] = jnp.dot(x_ref[...].astype(BF16), wpq_ref[...], preferred_element_type=F32)
    T = x_ref.shape[0]
    half = sk_ref.shape[-1]
    kk = topk + 1
    sub = lax.broadcasted_iota(jnp.int32, (SUBLANES, T), 0)

    def head(h, carry):
        base = pl.multiple_of(h * 2 * half, 2 * half)
        qa = q_scr[:, pl.ds(base, half)].astype(BF16)
        qb = q_scr[:, pl.ds(base + half, half)].astype(BF16)
        s1 = lax.dot_general(sk_ref[0], qa, _NT, preferred_element_type=F32)
        s2 = lax.dot_general(sk_ref[1], qb, _NT, preferred_element_type=F32)
        a, rank1 = _extract_desc(s1, kk, topk)
        b, rank2 = _extract_desc(s2, kk, topk)
        b_arr = jnp.concatenate(b + [jnp.full((1, T), NEG_INF, F32)] * ((-kk) % SUBLANES), axis=0)
        slabs = []
        for p in range(kk):
            nq = kk // (p + 1)
            for s0 in range(0, nq, SUBLANES):
                blk = a[p] + b_arr[s0:s0 + SUBLANES, :]
                if nq - s0 < SUBLANES:
                    blk = jnp.where(sub < (nq - s0), blk, NEG_INF)
                slabs.append(blk)
        cand = jnp.concatenate(slabs, axis=0)
        cs, _ = _extract_desc(cand, kk)
        tau = 0.5 * (cs[topk - 1] + cs[topk])
        z = jnp.zeros((1, T), F32)
        for r in range(topk):
            z = z + jnp.exp(cs[r] - cs[0])
        b_top = b_arr[0:topk, :]
        c1 = jnp.zeros((nk, T), F32)
        for p in range(topk):
            cnt = jnp.sum((a[p] + b_top > tau).astype(F32), axis=0, keepdims=True)
            c1 = jnp.where(rank1 == float(p), cnt, c1)
        c1_ref[h] = _dup_bf16_words(c1)
        e1_ref[h] = _dup_bf16_words(jnp.exp(s1 - a[0]))
        r2_ref[h] = rank2.astype(BF16)
        e2_ref[h] = (jnp.exp(s2 - b[0]) / z).astype(BF16)
        return carry

    lax.fori_loop(0, H, head, 0)


def _peer_route(x2d, wpq, sk, tr):
    T, D = x2d.shape
    H = PEER_HEADS
    nk = sk.shape[1]
    kern = functools.partial(_route_kernel, H=H, nk=nk, topk=PEER_TOPK)
    shp_w = jax.ShapeDtypeStruct((H, nk, T), jnp.uint32)
    shp_b = jax.ShapeDtypeStruct((H, nk, T), BF16)
    ospec = pl.BlockSpec((H, nk, tr), lambda i: (0, 0, i))
    return pl.pallas_call(
        kern,
        grid=(T // tr,),
        in_specs=[pl.BlockSpec((tr, D), lambda i: (i, 0)),
                  pl.BlockSpec(wpq.shape, lambda i: (0, 0)),
                  pl.BlockSpec(sk.shape, lambda i: (0, 0, 0))],
        out_specs=[ospec, ospec, ospec, ospec],
        out_shape=[shp_w, shp_w, shp_b, shp_b],
        scratch_shapes=[pltpu.VMEM((tr, wpq.shape[1]), F32)],
        compiler_params=_cparams(("parallel",)),
        name="peer_route",
    )(x2d, wpq, sk)


def _peer_kernel(x_ref, c1_ref, e1_ref, r2_ref, e2_ref, u_ref, vt_ref, g_ref, b_ref, o_ref,
                 xb_scr, act_scr, w_scr, acc_scr, bc1_scr, be1_scr, *, alpha, H, nk, ib, sub):
    step = pl.program_id(1)
    tt = x_ref.shape[0]
    stages = ib // sub
    se = sub * nk
    rc = 2 * SUBLANES
    grp = 4

    @pl.when(step == 0)
    def _():
        xb_scr[...] = x_ref[...].astype(BF16)
        acc_scr[...] = jnp.zeros(acc_scr.shape, F32)

    def act_mm(k):
        act_scr[k] = lax.dot_general(u_ref[k * se:(k + 1) * se, :], xb_scr[...], _NT,
                                     preferred_element_type=F32)

    def out_mm(k):
        acc_scr[...] += jnp.dot(vt_ref[:, k * se:(k + 1) * se], w_scr[k],
                                preferred_element_type=F32)

    def gates(k):
        for half in range(sub):
            ii = k * sub + half
            par = ii % 2
            for h in range(H):
                bc1_scr[par, h] = pltpu.bitcast(
                    jnp.broadcast_to(c1_ref[h, ii:ii + 1, :], (SUBLANES, tt)), BF16)
                be1_scr[par, h] = pltpu.bitcast(
                    jnp.broadcast_to(e1_ref[h, ii:ii + 1, :], (SUBLANES, tt)), BF16)
            for g in range(nk // (grp * rc)):
                accs = [jnp.zeros((rc, tt), BF16) for _ in range(grp)]
                for h in range(H):
                    cb = bc1_scr[par, h]
                    eb = be1_scr[par, h]
                    for c in range(grp):
                        j0 = (g * grp + c) * rc
                        sel = jnp.where(r2_ref[h, j0:j0 + rc, :] < cb,
                                        e2_ref[h, j0:j0 + rc, :], jnp.zeros((), BF16))
                        accs[c] = accs[c] + sel * eb
                for c in range(grp):
                    e0 = half * nk + (g * grp + c) * rc
                    a = act_scr[k, e0:e0 + rc, :]
                    gel = 0.5 * a * (1.0 + lax.erf(a * (2.0 ** -0.5)))
                    w_scr[k, e0:e0 + rc, :] = accs[c] * gel.astype(BF16)

    act_mm(0)
    for k in range(stages):
        if k + 1 < stages:
            act_mm(k + 1)
        gates(k)
        if k >= 1:
            out_mm(k - 1)
    out_mm(stages - 1)

    @pl.when(step == pl.num_programs(1) - 1)
    def _():
        y = alpha * x_ref[...] + acc_scr[...].T
        o_ref[...] = _layer_norm(y, g_ref[...], b_ref[...])


def _peer_experts_ln(x2d, c1, e1, r2, e2, u_bf, vt_bf, g, b, alpha, tt, ib):
    T, D = x2d.shape
    H, nk, _ = c1.shape
    ne = ib * nk
    steps = nk // ib
    sub = 2
    kern = functools.partial(_peer_kernel, alpha=alpha, H=H, nk=nk, ib=ib, sub=sub)
    return pl.pallas_call(
        kern,
        grid=(T // tt, steps),
        in_specs=[pl.BlockSpec((tt, D), lambda t, s: (t, 0)),
                  pl.BlockSpec((H, ib, tt), lambda t, s: (0, s, t)),
                  pl.BlockSpec((H, ib, tt), lambda t, s: (0, s, t)),
                  pl.BlockSpec((H, nk, tt), lambda t, s: (0, 0, t)),
                  pl.BlockSpec((H, nk, tt), lambda t, s: (0, 0, t)),
                  pl.BlockSpec((ne, D), lambda t, s: (s, 0)),
                  pl.BlockSpec((D, ne), lambda t, s: (0, s)),
                  pl.BlockSpec((1, D), lambda t, s: (0, 0)),
                  pl.BlockSpec((1, D), lambda t, s: (0, 0))],
        out_specs=pl.BlockSpec((tt, D), lambda t, s: (t, 0)),
        out_shape=jax.ShapeDtypeStruct((T, D), F32),
        scratch_shapes=[pltpu.VMEM((tt, D), BF16),
                        pltpu.VMEM((ib // sub, sub * nk, tt), F32),
                        pltpu.VMEM((ib // sub, sub * nk, tt), BF16),
                        pltpu.VMEM((D, tt), F32),
                        pltpu.VMEM((2, H, 2 * SUBLANES, tt), BF16),
                        pltpu.VMEM((2, H, 2 * SUBLANES, tt), BF16)],
        compiler_params=_cparams(("parallel", "arbitrary")),
        name="peer_experts_ln",
    )(x2d, c1, e1, r2, e2, u_bf, vt_bf, g, b)


def _tile(n, pref):
    t = min(n, pref)
    assert n % t == 0, (n, pref)
    return t


def kernel(x, mem, w_in, i_bias, f_bias, conv_w, conv_b, lam_qk, da_norm_g, ml_norm_g, w_out,
           ln1_g, ln1_b, wq_mem, wkv_mem, wo_mem, ln2_g, ln2_b, w_pq, sub_keys, u_tab, v_tab,
           ln3_g, ln3_b):
    B, S, D = x.shape
    depth = w_in.shape[0]
    T = B * S
    M = mem.shape[1]
    alpha = (2.0 * depth) ** 0.25
    wda = DA_HEADS * da_norm_g.shape[-1]
    wml = ml_norm_g.shape[-1]
    assert w_in.shape[-1] == 3 * wda + 4 * wml + 2 * ML_HEADS
    assert wda == wml and wml % LANES == 0 and 2 * ML_HEADS <= LANES

    tm = _tile(T, 256)
    tq = _tile(S, 512)
    L = _tile(S, 128)
    tmem = _tile(S, 256)
    tr = _tile(T, 256)
    tt = _tile(T, 512)
    ib = 2 * SUBLANES

    mem2d = mem.reshape(B * M, D)
    o_q, o_k, o_v = 0, wda, 2 * wda
    o_mq = 3 * wda
    o_mk, o_mv, o_mo, o_gt = o_mq + wml, o_mq + 2 * wml, o_mq + 3 * wml, o_mq + 4 * wml

    for l in range(depth):
        lam_init = 0.8 - 0.6 * math.exp(-0.3 * l)
        wl = w_in[l]
        wb = jnp.concatenate([wl[:, o_q:o_mq], wl[:, o_mv:o_mo]], axis=1).astype(BF16)
        wf = jnp.concatenate(
            [wl[:, o_mq:o_mv], wl[:, o_mo:o_gt],
             jnp.pad(wl[:, o_gt:], ((0, 0), (0, LANES - 2 * ML_HEADS)))], axis=1).astype(BF16)
        gbias = jnp.pad(jnp.concatenate([i_bias[l], f_bias[l]]),
                        (0, LANES - 2 * ML_HEADS)).reshape(1, LANES)
        lami = jnp.full((1, da_norm_g.shape[-1]), lam_init, F32)

        x2d = x.reshape(T, D)
        hb, hf = _inproj(x2d, wb, wf, tm)
        hb3 = hb.reshape(B, S, -1)
        hf3 = hf.reshape(B, S, -1)
        da = _diff_attention(hb3, lam_qk[l], da_norm_g[l].reshape(1, -1), lami, tq)
        hm = _mlstm(hf3, hb3, conv_w[l], conv_b[l].reshape(1, -1), gbias,
                    ml_norm_g[l].reshape(1, -1), L)
        x1 = _outproj_ln(x2d, da.reshape(T, -1), hm.reshape(T, -1), w_out[l].astype(BF16),
                         ln1_g[l].reshape(1, D), ln1_b[l].reshape(1, D), alpha, tm)

        kv = _matmul(mem2d, wkv_mem[l].astype(BF16), _tile(B * M, 256), BF16)
        x2 = _memattn_ln(x1.reshape(B, S, D), kv.reshape(B, M, 2 * D), wq_mem[l].astype(BF16),
                         wo_mem[l].astype(BF16), ln2_g[l].reshape(1, D), ln2_b[l].reshape(1, D),
                         alpha, tmem)
        x2d = x2.reshape(T, D)

        c1, e1, r2, e2 = _peer_route(x2d, w_pq[l].astype(BF16), sub_keys[l].astype(BF16), tr)
        x3 = _peer_experts_ln(x2d, c1, e1, r2, e2, u_tab[l].astype(BF16),
                              v_tab[l].T.astype(BF16), ln3_g[l].reshape(1, D),
                              ln3_b[l].reshape(1, D), alpha, tt, ib)
        x = x3.reshape(B, S, D)
    return x
```

```python
import functools
import math

import jax
import jax.numpy as jnp
from jax import lax
from jax.experimental import pallas as pl
from jax.experimental.pallas import tpu as pltpu

F32 = jnp.float32
BF16 = jnp.bfloat16
LN_EPS = 1e-5
NEG_INF = float("-inf")

DA_HEADS = 4
ML_HEADS = 4
MEM_HEADS = 4
PEER_HEADS = 8
PEER_TOPK = 16
CONV_K = 4

VMEM_LIMIT_BYTES = 56 * 1024 * 1024
LANES = 128
SUBLANES = 8

_NT = (((1,), (1,)), ((), ()))


def _cparams(sem):
    return pltpu.CompilerParams(dimension_semantics=sem, vmem_limit_bytes=VMEM_LIMIT_BYTES)


def _layer_norm(y, g, b):
    mu = jnp.mean(y, axis=-1, keepdims=True)
    yc = y - mu
    var = jnp.mean(yc * yc, axis=-1, keepdims=True)
    return yc * lax.rsqrt(var + LN_EPS) * g + b


def _inproj_kernel(x_ref, wb_ref, wf_ref, hb_ref, hf_ref):
    xb = x_ref[...].astype(BF16)
    hb_ref[...] = jnp.dot(xb, wb_ref[...], preferred_element_type=F32).astype(BF16)
    hf_ref[...] = jnp.dot(xb, wf_ref[...], preferred_element_type=F32)


def _inproj(x2d, wb, wf, tm):
    T, D = x2d.shape
    nb, nf = wb.shape[1], wf.shape[1]
    return pl.pallas_call(
        _inproj_kernel,
        grid=(T // tm,),
        in_specs=[pl.BlockSpec((tm, D), lambda i: (i, 0)),
                  pl.BlockSpec((D, nb), lambda i: (0, 0)),
                  pl.BlockSpec((D, nf), lambda i: (0, 0))],
        out_specs=[pl.BlockSpec((tm, nb), lambda i: (i, 0)),
                   pl.BlockSpec((tm, nf), lambda i: (i, 0))],
        out_shape=[jax.ShapeDtypeStruct((T, nb), BF16),
                   jax.ShapeDtypeStruct((T, nf), F32)],
        compiler_params=_cparams(("parallel",)),
        name="inproj",
    )(x2d, wb, wf)


def _mm_kernel(x_ref, w_ref, o_ref):
    o_ref[...] = jnp.dot(x_ref[...].astype(BF16), w_ref[...],
                         preferred_element_type=F32).astype(o_ref.dtype)


def _matmul(x2d, w, tm, out_dtype):
    M, K = x2d.shape
    N = w.shape[1]
    return pl.pallas_call(
        _mm_kernel,
        grid=(M // tm,),
        in_specs=[pl.BlockSpec((tm, K), lambda i: (i, 0)),
                  pl.BlockSpec((K, N), lambda i: (0, 0))],
        out_specs=pl.BlockSpec((tm, N), lambda i: (i, 0)),
        out_shape=jax.ShapeDtypeStruct((M, N), out_dtype),
        compiler_params=_cparams(("parallel",)),
        name="matmul",
    )(x2d, w)


def _da_kernel(q_ref, k_ref, v_ref, lq_ref, g_ref, lami_ref, o_ref,
               m1, l1, a1, m2, l2, a2, *, tq, dk):
    qb = pl.program_id(2)
    dv = q_ref.shape[-1]
    rep = tq // LANES

    lane = lax.broadcasted_iota(jnp.int32, (1, dv), 1)
    qf = q_ref[...].astype(F32) * (dk ** -0.5)
    q1 = jnp.where(lane < dk, qf, 0.0).astype(BF16)
    q2 = jnp.where(lane >= dk, qf, 0.0).astype(BF16)

    m1[...] = jnp.full(m1.shape, NEG_INF, F32)
    m2[...] = jnp.full(m2.shape, NEG_INF, F32)
    l1[...] = jnp.zeros(l1.shape, F32)
    l2[...] = jnp.zeros(l2.shape, F32)
    a1[...] = jnp.zeros(a1.shape, F32)
    a2[...] = jnp.zeros(a2.shape, F32)

    def update(s, vblk, m_ref, l_ref, a_ref):
        m_prev = m_ref[...]
        m_cur = jnp.max(s, axis=1, keepdims=True)
        m_next = jnp.maximum(m_prev, m_cur)
        p = jnp.exp(s - jnp.concatenate([m_next] * rep, axis=1))
        alpha = jnp.exp(m_prev - m_next)
        l_ref[...] = alpha * l_ref[...] + jnp.sum(p, axis=1, keepdims=True)
        a_ref[...] = alpha * a_ref[...] + jnp.dot(p.astype(BF16), vblk,
                                                  preferred_element_type=F32)
        m_ref[...] = m_next

    def block(kb, masked):
        off = pl.multiple_of(kb * tq, tq)
        kblk = k_ref[pl.ds(off, tq), :]
        vblk = v_ref[pl.ds(off, tq), :]
        s1 = lax.dot_general(q1, kblk, _NT, preferred_element_type=F32)
        s2 = lax.dot_general(q2, kblk, _NT, preferred_element_type=F32)
        if masked:
            row = lax.broadcasted_iota(jnp.int32, (tq, tq), 0)
            col = lax.broadcasted_iota(jnp.int32, (tq, tq), 1)
            keep = col <= row
            s1 = jnp.where(keep, s1, NEG_INF)
            s2 = jnp.where(keep, s2, NEG_INF)
        update(s1, vblk, m1, l1, a1)
        update(s2, vblk, m2, l2, a2)

    def body(kb, carry):
        block(kb, False)
        return carry

    lax.fori_loop(0, qb, body, 0)
    block(qb, True)

    lq = lq_ref[...]
    lam_init = lami_ref[...][:, :1]
    lam = (jnp.exp(jnp.sum(lq[0:1] * lq[1:2], axis=1, keepdims=True))
           - jnp.exp(jnp.sum(lq[2:3] * lq[3:4], axis=1, keepdims=True)) + lam_init)
    o = a1[...] / l1[...] - lam * (a2[...] / l2[...])
    o = o * lax.rsqrt(jnp.mean(o * o, axis=-1, keepdims=True) + LN_EPS)
    o = o * g_ref[...] * (1.0 - lami_ref[...])
    o_ref[...] = o.astype(o_ref.dtype)


def _diff_attention(hb3, lam_qk, da_g, lami, tq):
    B, S, _ = hb3.shape
    dv = da_g.shape[-1]
    dk = dv // 2
    H = DA_HEADS
    kern = functools.partial(_da_kernel, tq=tq, dk=dk)
    return pl.pallas_call(
        kern,
        grid=(B, H, S // tq),
        in_specs=[pl.BlockSpec((None, tq, dv), lambda b, h, i: (b, i, h)),
                  pl.BlockSpec((None, S, dv), lambda b, h, i: (b, 0, H + h)),
                  pl.BlockSpec((None, S, dv), lambda b, h, i: (b, 0, 2 * H + h)),
                  pl.BlockSpec(lam_qk.shape, lambda b, h, i: (0, 0)),
                  pl.BlockSpec((1, dv), lambda b, h, i: (0, 0)),
                  pl.BlockSpec((1, dv), lambda b, h, i: (0, 0))],
        out_specs=pl.BlockSpec((None, tq, dv), lambda b, h, i: (b, i, h)),
        out_shape=jax.ShapeDtypeStruct((B, S, H * dv), BF16),
        scratch_shapes=[pltpu.VMEM((tq, LANES), F32), pltpu.VMEM((tq, LANES), F32),
                        pltpu.VMEM((tq, dv), F32),
                        pltpu.VMEM((tq, LANES), F32), pltpu.VMEM((tq, LANES), F32),
                        pltpu.VMEM((tq, dv), F32)],
        compiler_params=_cparams(("parallel", "parallel", "arbitrary")),
        name="diff_attention",
    )(hb3, hb3, hb3, lam_qk, da_g, lami)


def _log_sigmoid(x):
    return -(jnp.maximum(-x, 0.0) + jnp.log1p(jnp.exp(-jnp.abs(x))))


def _ml_kernel(qk_ref, og_ref, gt_ref, v_ref, cw_ref, cb_ref, gb_ref, g_ref, out_ref,
               buf, c_scr, n_scr, m_scr, *, L, H, dh):
    c = pl.program_id(1)
    W = H * dh

    @pl.when(c == 0)
    def _():
        buf[0:SUBLANES, :] = jnp.zeros((SUBLANES, buf.shape[1]), F32)
        c_scr[...] = jnp.zeros(c_scr.shape, F32)
        n_scr[...] = jnp.zeros(n_scr.shape, F32)
        m_scr[...] = jnp.zeros(m_scr.shape, F32)

    buf[SUBLANES:SUBLANES + L, :] = qk_ref[...]
    cw = cw_ref[...]
    y = cb_ref[...]
    for j in range(CONV_K):
        s0 = SUBLANES - (CONV_K - 1) + j
        y = y + cw[j:j + 1, :] * buf[s0:s0 + L, :]
    buf[0:SUBLANES, :] = buf[L:L + SUBLANES, :]
    qk = y * jax.nn.sigmoid(y)

    G = gt_ref[...] + gb_ref[...]
    ls = _log_sigmoid(G)
    row = lax.broadcasted_iota(jnp.int32, (L, L), 0)
    col = lax.broadcasted_iota(jnp.int32, (L, L), 1)
    causal = col <= row
    tri = causal.astype(F32)
    tri_t = (row <= col).astype(F32)
    bcol_all = jnp.dot(tri, ls, precision=lax.Precision.HIGHEST,
                       preferred_element_type=F32)
    GT = G.T
    brow_all = jnp.dot(ls.T, tri_t, precision=lax.Precision.HIGHEST,
                       preferred_element_type=F32)

    for h in range(H):
        b_col = bcol_all[:, H + h:H + h + 1]
        i_col = G[:, h:h + 1]
        b_row = brow_all[H + h:H + h + 1, :]
        i_row = GT[h:h + 1, :]
        b_last = bcol_all[L - 1:L, H + h:H + h + 1]
        m_old = m_scr[h:h + 1, 0:1]

        a = b_col + m_old
        d = jnp.where(causal, b_col - b_row + i_row, NEG_INF)
        m_t = jnp.maximum(a, jnp.max(d, axis=1, keepdims=True))
        w_inter = jnp.exp(a - m_t)

        q = qk[:, h * dh:(h + 1) * dh]
        k = qk[:, W + h * dh:W + (h + 1) * dh] * (dh ** -0.5)
        qb = q.astype(BF16)
        kb = k.astype(BF16)
        vh = v_ref[:, h * dh:(h + 1) * dh]
        s = lax.dot_general(qb, kb, _NT, preferred_element_type=F32)
        w_intra = jnp.exp(d - m_t) * s
        c_old = c_scr[h]
        n_old = n_scr[h:h + 1, :]
        num = (w_inter * jnp.dot(qb, c_old.astype(BF16), preferred_element_type=F32)
               + jnp.dot(w_intra.astype(BF16), vh, preferred_element_type=F32))
        qn = jnp.sum(q * n_old, axis=1, keepdims=True)
        nq = w_inter * qn + jnp.sum(w_intra, axis=1, keepdims=True)
        hh = num / jnp.maximum(jnp.abs(nq), jnp.exp(-m_t))

        g_col = b_last - b_col + i_col
        g_row = b_last - b_row + i_row
        m_new = jnp.maximum(b_last + m_old, jnp.max(g_row, axis=1, keepdims=True))
        decay = jnp.exp(b_last + m_old - m_new)
        wk = jnp.exp(g_col - m_new) * k
        c_scr[h] = decay * c_old + jnp.dot(wk.T.astype(BF16), vh,
                                           preferred_element_type=F32)
        n_scr[h:h + 1, :] = decay * n_old + jnp.sum(wk, axis=0, keepdims=True)
        m_scr[h:h + 1, :] = jnp.broadcast_to(m_new, (1, m_scr.shape[1]))

        mu = jnp.mean(hh, axis=-1, keepdims=True)
        hc = hh - mu
        var = jnp.mean(hc * hc, axis=-1, keepdims=True)
        hn = hc * lax.rsqrt(var + LN_EPS)
        gate = jax.nn.sigmoid(og_ref[:, h * dh:(h + 1) * dh])
        out_ref[:, h * dh:(h + 1) * dh] = (
            hn * g_ref[:, h * dh:(h + 1) * dh] * gate).astype(out_ref.dtype)


def _mlstm(hf3, hb3, conv_w, conv_b, gbias, ml_g, L):
    B, S, _ = hf3.shape
    W = ml_g.shape[-1]
    H = ML_HEADS
    dh = W // H
    kern = functools.partial(_ml_kernel, L=L, H=H, dh=dh)
    return pl.pallas_call(
        kern,
        grid=(B, S // L),
        in_specs=[pl.BlockSpec((None, L, 2 * W), lambda b, c: (b, c, 0)),
                  pl.BlockSpec((None, L, W), lambda b, c: (b, c, 2)),
                  pl.BlockSpec((None, L, LANES), lambda b, c: (b, c, 3 * W // LANES)),
                  pl.BlockSpec((None, L, W), lambda b, c: (b, c, 3)),
                  pl.BlockSpec((CONV_K, 2 * W), lambda b, c: (0, 0)),
                  pl.BlockSpec((1, 2 * W), lambda b, c: (0, 0)),
                  pl.BlockSpec((1, LANES), lambda b, c: (0, 0)),
                  pl.BlockSpec((1, W), lambda b, c: (0, 0))],
        out_specs=pl.BlockSpec((None, L, W), lambda b, c: (b, c, 0)),
        out_shape=jax.ShapeDtypeStruct((B, S, W), BF16),
        scratch_shapes=[pltpu.VMEM((L + SUBLANES, 2 * W), F32),
                        pltpu.VMEM((H, dh, dh), F32),
                        pltpu.VMEM((SUBLANES, dh), F32),
                        pltpu.VMEM((SUBLANES, LANES), F32)],
        compiler_params=_cparams(("parallel", "arbitrary")),
        name="mlstm",
    )(hf3, hf3, hf3, hb3, conv_w, conv_b, gbias, ml_g)


def _outproj_kernel(x_ref, da_ref, hm_ref, w_ref, g_ref, b_ref, o_ref, *, alpha):
    wd = da_ref.shape[-1]
    mix = (jnp.dot(da_ref[...], w_ref[0:wd, :], preferred_element_type=F32)
           + jnp.dot(hm_ref[...], w_ref[wd:, :], preferred_element_type=F32))
    o_ref[...] = _layer_norm(alpha * x_ref[...] + mix, g_ref[...], b_ref[...])


def _outproj_ln(x2d, da2d, hm2d, w_out, g, b, alpha, tm):
    T, D = x2d.shape
    wd, wm = da2d.shape[1], hm2d.shape[1]
    kern = functools.partial(_outproj_kernel, alpha=alpha)
    return pl.pallas_call(
        kern,
        grid=(T // tm,),
        in_specs=[pl.BlockSpec((tm, D), lambda i: (i, 0)),
                  pl.BlockSpec((tm, wd), lambda i: (i, 0)),
                  pl.BlockSpec((tm, wm), lambda i: (i, 0)),
                  pl.BlockSpec((wd + wm, D), lambda i: (0, 0)),
                  pl.BlockSpec((1, D), lambda i: (0, 0)),
                  pl.BlockSpec((1, D), lambda i: (0, 0))],
        out_specs=pl.BlockSpec((tm, D), lambda i: (i, 0)),
        out_shape=jax.ShapeDtypeStruct((T, D), F32),
        compiler_params=_cparams(("parallel",)),
        name="outproj_ln",
    )(x2d, da2d, hm2d, w_out, g, b)


def _memattn_kernel(x_ref, kv_ref, wq_ref, wo_ref, g_ref, b_ref, o_ref, *, alpha, H):
    x = x_ref[...]
    D = x.shape[-1]
    dh = D // H
    q = jnp.dot(x.astype(BF16), wq_ref[...], preferred_element_type=F32) * (dh ** -0.5)
    qb = q.astype(BF16)
    outs = []
    for h in range(H):
        kh = kv_ref[:, h * dh:(h + 1) * dh]
        vh = kv_ref[:, D + h * dh:D + (h + 1) * dh]
        s = lax.dot_general(qb[:, h * dh:(h + 1) * dh], kh, _NT, preferred_element_type=F32)
        s = s - jnp.max(s, axis=-1, keepdims=True)
        e = jnp.exp(s)
        p = e / jnp.sum(e, axis=-1, keepdims=True)
        outs.append(jnp.dot(p.astype(BF16), vh, preferred_element_type=F32))
    o = jnp.concatenate(outs, axis=-1).astype(BF16)
    att = jnp.dot(o, wo_ref[...], preferred_element_type=F32)
    o_ref[...] = _layer_norm(alpha * x + att, g_ref[...], b_ref[...])


def _memattn_ln(x3, kv3, wq, wo, g, b, alpha, tm):
    B, S, D = x3.shape
    M = kv3.shape[1]
    kern = functools.partial(_memattn_kernel, alpha=alpha, H=MEM_HEADS)
    return pl.pallas_call(
        kern,
        grid=(B, S // tm),
        in_specs=[pl.BlockSpec((None, tm, D), lambda bb, i: (bb, i, 0)),
                  pl.BlockSpec((None, M, 2 * D), lambda bb, i: (bb, 0, 0)),
                  pl.BlockSpec((D, D), lambda bb, i: (0, 0)),
                  pl.BlockSpec((D, D), lambda bb, i: (0, 0)),
                  pl.BlockSpec((1, D), lambda bb, i: (0, 0)),
                  pl.BlockSpec((1, D), lambda bb, i: (0, 0))],
        out_specs=pl.BlockSpec((None, tm, D), lambda bb, i: (bb, i, 0)),
        out_shape=jax.ShapeDtypeStruct((B, S, D), F32),
        compiler_params=_cparams(("parallel", "parallel")),
        name="memattn_ln",
    )(x3, kv3, wq, wo, g, b)


NO_RANK = 127.0


def _extract_desc(v, count, n_ranked=0):
    tops = []
    rank = jnp.full(v.shape, NO_RANK, F32) if n_ranked else None
    for r in range(count):
        mx = jnp.max(v, axis=0, keepdims=True)
        tops.append(mx)
        hit = v == mx
        if r < n_ranked:
            rank = jnp.where(hit, float(r), rank)
        if r + 1 < count:
            v = jnp.where(hit, NEG_INF, v)
    return tops, rank


def _dup_bf16_words(x):
    u = lax.bitcast_convert_type(x.astype(BF16).astype(F32), jnp.uint32)
    return u | (u >> 16)


def _route_kernel(x_ref, wpq_ref, sk_ref, c1_ref, e1_ref, r2_ref, e2_ref, q_scr, *, H, nk, topk):
    q_scr[...] = jnp.dot(x_ref[...].astype(BF16), wpq_ref[...], preferred_element_type=F32)
    T = x_ref.shape[0]
    half = sk_ref.shape[-1]
    kk = topk + 1
    sub = lax.broadcasted_iota(jnp.int32, (SUBLANES, T), 0)

    def head(h, carry):
        base = pl.multiple_of(h * 2 * half, 2 * half)
        qa = q_scr[:, pl.ds(base, half)].astype(BF16)
        qb = q_scr[:, pl.ds(base + half, half)].astype(BF16)
        s1 = lax.dot_general(sk_ref[0], qa, _NT, preferred_element_type=F32)
        s2 = lax.dot_general(sk_ref[1], qb, _NT, preferred_element_type=F32)
        a, rank1 = _extract_desc(s1, kk, topk)
        b, rank2 = _extract_desc(s2, kk, topk)
        b_arr = jnp.concatenate(b + [jnp.full((1, T), NEG_INF, F32)] * ((-kk) % SUBLANES), axis=0)
        slabs = []
        for p in range(kk):
            nq = kk // (p + 1)
            for s0 in range(0, nq, SUBLANES):
                blk = a[p] + b_arr[s0:s0 + SUBLANES, :]
                if nq - s0 < SUBLANES:
                    blk = jnp.where(sub < (nq - s0), blk, NEG_INF)
                slabs.append(blk)
        cand = jnp.concatenate(slabs, axis=0)
        cs, _ = _extract_desc(cand, kk)
        tau = 0.5 * (cs[topk - 1] + cs[topk])
        z = jnp.zeros((1, T), F32)
        for r in range(topk):
            z = z + jnp.exp(cs[r] - cs[0])
        b_top = b_arr[0:topk, :]
        c1 = jnp.zeros((nk, T), F32)
        for p in range(topk):
            cnt = jnp.sum((a[p] + b_top > tau).astype(F32), axis=0, keepdims=True)
            c1 = jnp.where(rank1 == float(p), cnt, c1)
        c1_ref[h] = _dup_bf16_words(c1)
        e1_ref[h] = _dup_bf16_words(jnp.exp(s1 - a[0]))
        r2_ref[h] = rank2.astype(BF16)
        e2_ref[h] = (jnp.exp(s2 - b[0]) / z).astype(BF16)
        return carry

    lax.fori_loop(0, H, head, 0)


def _peer_route(x2d, wpq, sk, tr):
    T, D = x2d.shape
    H = PEER_HEADS
    nk = sk.shape[1]
    kern = functools.partial(_route_kernel, H=H, nk=nk, topk=PEER_TOPK)
    shp_w = jax.ShapeDtypeStruct((H, nk, T), jnp.uint32)
    shp_b = jax.ShapeDtypeStruct((H, nk, T), BF16)
    ospec = pl.BlockSpec((H, nk, tr), lambda i: (0, 0, i))
    return pl.pallas_call(
        kern,
        grid=(T // tr,),
        in_specs=[pl.BlockSpec((tr, D), lambda i: (i, 0)),
                  pl.BlockSpec(wpq.shape, lambda i: (0, 0)),
                  pl.BlockSpec(sk.shape, lambda i: (0, 0, 0))],
        out_specs=[ospec, ospec, ospec, ospec],
        out_shape=[shp_w, shp_w, shp_b, shp_b],
        scratch_shapes=[pltpu.VMEM((tr, wpq.shape[1]), F32)],
        compiler_params=_cparams(("parallel",)),
        name="peer_route",
    )(x2d, wpq, sk)


def _peer_kernel(x_ref, c1_ref, e1_ref, r2_ref, e2_ref, u_ref, vta_ref, vtb_ref, g_ref, b_ref, o_ref,
                 xb_scr, act_scr, wa_scr, wb_scr, acc_scr, bc1_scr, be1_scr,
                 *, alpha, H, nk, ib, sub):
    step = pl.program_id(1)
    last = pl.num_programs(1) - 1
    tt = x_ref.shape[0]
    stages = (ib // 2) // sub
    se = sub * nk
    rc = 2 * SUBLANES
    grp = 4

    @pl.when(step == 0)
    def _():
        xb_scr[...] = x_ref[...].astype(BF16)
        acc_scr[...] = jnp.zeros(acc_scr.shape, F32)
        wb_scr[...] = jnp.zeros(wb_scr.shape, BF16)

    def act_mm(s):
        act_scr[s] = lax.dot_general(u_ref[s * se:(s + 1) * se, :], xb_scr[...], _NT,
                                     preferred_element_type=F32)

    def gates(s, w_scr, k):
        for half in range(sub):
            ii = s * sub + half
            par = ii % 2
            for h in range(H):
                bc1_scr[par, h] = pltpu.bitcast(
                    jnp.broadcast_to(c1_ref[h, ii:ii + 1, :], (SUBLANES, tt)), BF16)
                be1_scr[par, h] = pltpu.bitcast(
                    jnp.broadcast_to(e1_ref[h, ii:ii + 1, :], (SUBLANES, tt)), BF16)
            for g in range(nk // (grp * rc)):
                accs = [jnp.zeros((rc, tt), BF16) for _ in range(grp)]
                for h in range(H):
                    cb = bc1_scr[par, h]
                    eb = be1_scr[par, h]
                    for c in range(grp):
                        j0 = (g * grp + c) * rc
                        sel = jnp.where(r2_ref[h, j0:j0 + rc, :] < cb,
                                        e2_ref[h, j0:j0 + rc, :], jnp.zeros((), BF16))
                        accs[c] = accs[c] + sel * eb
                for c in range(grp):
                    e0 = half * nk + (g * grp + c) * rc
                    a = act_scr[s, e0:e0 + rc, :]
                    gel = 0.5 * a * (1.0 + lax.erf(a * (2.0 ** -0.5)))
                    w_scr[k * se + e0:k * se + e0 + rc, :] = accs[c] * gel.astype(BF16)

    @pl.when(step < last)
    def _():
        act_mm(0)
        for phase, (w_scr, vt_ref, wp_scr) in enumerate(((wa_scr, vtb_ref, wb_scr),
                                                         (wb_scr, vta_ref, wa_scr))):
            for k in range(stages):
                s = phase * stages + k
                if s + 1 < 2 * stages:
                    act_mm(s + 1)
                gates(s, w_scr, k)
                if k == 0:
                    acc_scr[...] += jnp.dot(vt_ref[...], wp_scr[...], preferred_element_type=F32)

    @pl.when(step == last)
    def _():
        acc = acc_scr[...] + jnp.dot(vtb_ref[...], wb_scr[...], preferred_element_type=F32)
        y = alpha * x_ref[...] + acc.T
        o_ref[...] = _layer_norm(y, g_ref[...], b_ref[...])


def _peer_experts_ln(x2d, c1, e1, r2, e2, u_bf, vt_bf, g, b, alpha, tt, ib):
    T, D = x2d.shape
    H, nk, _ = c1.shape
    ne = ib * nk
    steps = nk // ib
    sub = 2
    kern = functools.partial(_peer_kernel, alpha=alpha, H=H, nk=nk, ib=ib, sub=sub)

    def cur(s):
        return jnp.minimum(s, steps - 1)

    return pl.pallas_call(
        kern,
        grid=(T // tt, steps + 1),
        in_specs=[pl.BlockSpec((tt, D), lambda t, s: (t, 0)),
                  pl.BlockSpec((H, ib, tt), lambda t, s: (0, cur(s), t)),
                  pl.BlockSpec((H, ib, tt), lambda t, s: (0, cur(s), t)),
                  pl.BlockSpec((H, nk, tt), lambda t, s: (0, 0, t)),
                  pl.BlockSpec((H, nk, tt), lambda t, s: (0, 0, t)),
                  pl.BlockSpec((ne, D), lambda t, s: (cur(s), 0)),
                  pl.BlockSpec((D, ne // 2), lambda t, s: (0, 2 * cur(s))),
                  pl.BlockSpec((D, ne // 2), lambda t, s: (0, jnp.maximum(2 * s - 1, 0))),
                  pl.BlockSpec((1, D), lambda t, s: (0, 0)),
                  pl.BlockSpec((1, D), lambda t, s: (0, 0))],
        out_specs=pl.BlockSpec((tt, D), lambda t, s: (t, 0)),
        out_shape=jax.ShapeDtypeStruct((T, D), F32),
        scratch_shapes=[pltpu.VMEM((tt, D), BF16),
                        pltpu.VMEM((ib // sub, sub * nk, tt), F32),
                        pltpu.VMEM((ne // 2, tt), BF16),
                        pltpu.VMEM((ne // 2, tt), BF16),
                        pltpu.VMEM((D, tt), F32),
                        pltpu.VMEM((2, H, 2 * SUBLANES, tt), BF16),
                        pltpu.VMEM((2, H, 2 * SUBLANES, tt), BF16)],
        compiler_params=_cparams(("parallel", "arbitrary")),
        name="peer_experts_ln",
    )(x2d, c1, e1, r2, e2, u_bf, vt_bf, vt_bf, g, b)


def _tile(n, pref):
    t = min(n, pref)
    assert n % t == 0, (n, pref)
    return t


def kernel(x, mem, w_in, i_bias, f_bias, conv_w, conv_b, lam_qk, da_norm_g, ml_norm_g, w_out,
           ln1_g, ln1_b, wq_mem, wkv_mem, wo_mem, ln2_g, ln2_b, w_pq, sub_keys, u_tab, v_tab,
           ln3_g, ln3_b):
    B, S, D = x.shape
    depth = w_in.shape[0]
    T = B * S
    M = mem.shape[1]
    alpha = (2.0 * depth) ** 0.25
    wda = DA_HEADS * da_norm_g.shape[-1]
    wml = ml_norm_g.shape[-1]
    assert w_in.shape[-1] == 3 * wda + 4 * wml + 2 * ML_HEADS
    assert wda == wml and wml % LANES == 0 and 2 * ML_HEADS <= LANES

    tm = _tile(T, 256)
    tq = _tile(S, 512)
    L = _tile(S, 128)
    tmem = _tile(S, 256)
    tr = _tile(T, 256)
    tt = _tile(T, 512)
    ib = 2 * SUBLANES

    mem2d = mem.reshape(B * M, D)
    o_q, o_k, o_v = 0, wda, 2 * wda
    o_mq = 3 * wda
    o_mk, o_mv, o_mo, o_gt = o_mq + wml, o_mq + 2 * wml, o_mq + 3 * wml, o_mq + 4 * wml

    for l in range(depth):
        lam_init = 0.8 - 0.6 * math.exp(-0.3 * l)
        wl = w_in[l]
        wb = jnp.concatenate([wl[:, o_q:o_mq], wl[:, o_mv:o_mo]], axis=1).astype(BF16)
        wf = jnp.concatenate(
            [wl[:, o_mq:o_mv], wl[:, o_mo:o_gt],
             jnp.pad(wl[:, o_gt:], ((0, 0), (0, LANES - 2 * ML_HEADS)))], axis=1).astype(BF16)
        gbias = jnp.pad(jnp.concatenate([i_bias[l], f_bias[l]]),
                        (0, LANES - 2 * ML_HEADS)).reshape(1, LANES)
        lami = jnp.full((1, da_norm_g.shape[-1]), lam_init, F32)

        x2d = x.reshape(T, D)
        hb, hf = _inproj(x2d, wb, wf, tm)
        hb3 = hb.reshape(B, S, -1)
        hf3 = hf.reshape(B, S, -1)
        da = _diff_attention(hb3, lam_qk[l], da_norm_g[l].reshape(1, -1), lami, tq)
        hm = _mlstm(hf3, hb3, conv_w[l], conv_b[l].reshape(1, -1), gbias,
                    ml_norm_g[l].reshape(1, -1), L)
        x1 = _outproj_ln(x2d, da.reshape(T, -1), hm.reshape(T, -1), w_out[l].astype(BF16),
                         ln1_g[l].reshape(1, D), ln1_b[l].reshape(1, D), alpha, tm)

        kv = _matmul(mem2d, wkv_mem[l].astype(BF16), _tile(B * M, 256), BF16)
        x2 = _memattn_ln(x1.reshape(B, S, D), kv.reshape(B, M, 2 * D), wq_mem[l].astype(BF16),
                         wo_mem[l].astype(BF16), ln2_g[l].reshape(1, D), ln2_b[l].reshape(1, D),
                         alpha, tmem)
        x2d = x2.reshape(T, D)

        c1, e1, r2, e2 = _peer_route(x2d, w_pq[l].astype(BF16), sub_keys[l].astype(BF16), tr)
        x3 = _peer_experts_ln(x2d, c1, e1, r2, e2, u_tab[l].astype(BF16),
                              v_tab[l].T.astype(BF16), ln3_g[l].reshape(1, D),
                              ln3_b[l].reshape(1, D), alpha, tt, ib)
        x = x3.reshape(B, S, D)
    return x
```

```python
import functools
import math

import jax
import jax.numpy as jnp
from jax import lax
from jax.experimental import pallas as pl
from jax.experimental.pallas import tpu as pltpu

F32 = jnp.float32
BF16 = jnp.bfloat16
LN_EPS = 1e-5
NEG_INF = float("-inf")

DA_HEADS = 4
ML_HEADS = 4
MEM_HEADS = 4
PEER_HEADS = 8
PEER_TOPK = 16
CONV_K = 4

VMEM_LIMIT_BYTES = 56 * 1024 * 1024
LANES = 128
SUBLANES = 8

_NT = (((1,), (1,)), ((), ()))


def _cparams(sem, flags=None):
    return pltpu.CompilerParams(dimension_semantics=sem, vmem_limit_bytes=VMEM_LIMIT_BYTES,
                                flags=flags)


def _layer_norm(y, g, b):
    mu = jnp.mean(y, axis=-1, keepdims=True)
    yc = y - mu
    var = jnp.mean(yc * yc, axis=-1, keepdims=True)
    return yc * lax.rsqrt(var + LN_EPS) * g + b


def _inproj_kernel(x_ref, wb_ref, wf_ref, hb_ref, hf_ref):
    xb = x_ref[...].astype(BF16)
    hb_ref[...] = jnp.dot(xb, wb_ref[...], preferred_element_type=F32).astype(BF16)
    hf_ref[...] = jnp.dot(xb, wf_ref[...], preferred_element_type=F32)


def _inproj(x2d, wb, wf, tm):
    T, D = x2d.shape
    nb, nf = wb.shape[1], wf.shape[1]
    return pl.pallas_call(
        _inproj_kernel,
        grid=(T // tm,),
        in_specs=[pl.BlockSpec((tm, D), lambda i: (i, 0)),
                  pl.BlockSpec((D, nb), lambda i: (0, 0)),
                  pl.BlockSpec((D, nf), lambda i: (0, 0))],
        out_specs=[pl.BlockSpec((tm, nb), lambda i: (i, 0)),
                   pl.BlockSpec((tm, nf), lambda i: (i, 0))],
        out_shape=[jax.ShapeDtypeStruct((T, nb), BF16),
                   jax.ShapeDtypeStruct((T, nf), F32)],
        compiler_params=_cparams(("parallel",)),
        name="inproj",
    )(x2d, wb, wf)


def _mm_kernel(x_ref, w_ref, o_ref):
    o_ref[...] = jnp.dot(x_ref[...].astype(BF16), w_ref[...],
                         preferred_element_type=F32).astype(o_ref.dtype)


def _matmul(x2d, w, tm, out_dtype):
    M, K = x2d.shape
    N = w.shape[1]
    return pl.pallas_call(
        _mm_kernel,
        grid=(M // tm,),
        in_specs=[pl.BlockSpec((tm, K), lambda i: (i, 0)),
                  pl.BlockSpec((K, N), lambda i: (0, 0))],
        out_specs=pl.BlockSpec((tm, N), lambda i: (i, 0)),
        out_shape=jax.ShapeDtypeStruct((M, N), out_dtype),
        compiler_params=_cparams(("parallel",)),
        name="matmul",
    )(x2d, w)


def _da_kernel(q_ref, k_ref, v_ref, lq_ref, g_ref, lami_ref, o_ref,
               m_scr, l_scr, a_scr, s_scr, p_scr, al_scr, *, tq, dk):
    qb = pl.program_id(2)
    dv = q_ref.shape[-1]
    rep = tq // LANES

    lane = lax.broadcasted_iota(jnp.int32, (1, dv), 1)
    qf = q_ref[...].astype(F32) * (dk ** -0.5)
    qs = (jnp.where(lane < dk, qf, 0.0).astype(BF16),
          jnp.where(lane >= dk, qf, 0.0).astype(BF16))

    m_scr[...] = jnp.full(m_scr.shape, NEG_INF, F32)
    l_scr[...] = jnp.zeros(l_scr.shape, F32)
    a_scr[...] = jnp.zeros(a_scr.shape, F32)
    p_scr[1] = jnp.zeros(p_scr.shape[1:], BF16)
    al_scr[1] = jnp.ones(al_scr.shape[1:], F32)

    def scores(kb, slot):
        off = pl.multiple_of(kb * tq, tq)
        kblk = k_ref[pl.ds(off, tq), :]
        for c in range(2):
            s_scr[slot, c] = lax.dot_general(qs[c], kblk, _NT, preferred_element_type=F32)

    def softmax(slot, masked):
        if masked:
            row = lax.broadcasted_iota(jnp.int32, (tq, tq), 0)
            col = lax.broadcasted_iota(jnp.int32, (tq, tq), 1)
            keep = col <= row
        for c in range(2):
            s = s_scr[slot, c]
            if masked:
                s = jnp.where(keep, s, NEG_INF)
            m_prev = m_scr[c]
            m_next = jnp.maximum(m_prev, jnp.max(s, axis=1, keepdims=True))
            p = jnp.exp(s - jnp.concatenate([m_next] * rep, axis=1))
            alpha = jnp.exp(m_prev - m_next)
            l_scr[c] = alpha * l_scr[c] + jnp.sum(p, axis=1, keepdims=True)
            m_scr[c] = m_next
            p_scr[slot, c] = p.astype(BF16)
            al_scr[slot, c] = alpha

    def values(kb, slot):
        off = pl.multiple_of(kb * tq, tq)
        vblk = v_ref[pl.ds(off, tq), :]
        for c in range(2):
            a_scr[c] = al_scr[slot, c] * a_scr[c] + jnp.dot(p_scr[slot, c], vblk,
                                                          preferred_element_type=F32)

    def stage(k, slot):
        scores(k + 1, 1 - slot)
        softmax(slot, False)
        values(jnp.maximum(k - 1, 0), 1 - slot)

    def drain(slot):
        softmax(slot, True)
        values(jnp.maximum(qb - 1, 0), 1 - slot)
        values(qb, slot)

    scores(0, 0)

    def body(j, carry):
        stage(2 * j, 0)
        stage(2 * j + 1, 1)
        return carry

    lax.fori_loop(0, qb // 2, body, 0)

    @pl.when(qb % 2 == 0)
    def _():
        drain(0)

    @pl.when(qb % 2 == 1)
    def _():
        stage(qb - 1, 0)
        drain(1)

    lq = lq_ref[...]
    lam_init = lami_ref[...][:, :1]
    lam = (jnp.exp(jnp.sum(lq[0:1] * lq[1:2], axis=1, keepdims=True))
           - jnp.exp(jnp.sum(lq[2:3] * lq[3:4], axis=1, keepdims=True)) + lam_init)
    o = a_scr[0] / l_scr[0] - lam * (a_scr[1] / l_scr[1])
    o = o * lax.rsqrt(jnp.mean(o * o, axis=-1, keepdims=True) + LN_EPS)
    o = o * g_ref[...] * (1.0 - lami_ref[...])
    o_ref[...] = o.astype(o_ref.dtype)


def _diff_attention(hb3, lam_qk, da_g, lami, tq):
    B, S, _ = hb3.shape
    dv = da_g.shape[-1]
    dk = dv // 2
    H = DA_HEADS
    assert dv == LANES
    kern = functools.partial(_da_kernel, tq=tq, dk=dk)
    return pl.pallas_call(
        kern,
        grid=(B, H, S // tq),
        in_specs=[pl.BlockSpec((None, tq, dv), lambda b, h, i: (b, i, h)),
                  pl.BlockSpec((None, S, dv), lambda b, h, i: (b, 0, H + h)),
                  pl.BlockSpec((None, S, dv), lambda b, h, i: (b, 0, 2 * H + h)),
                  pl.BlockSpec(lam_qk.shape, lambda b, h, i: (0, 0)),
                  pl.BlockSpec((1, dv), lambda b, h, i: (0, 0)),
                  pl.BlockSpec((1, dv), lambda b, h, i: (0, 0))],
        out_specs=pl.BlockSpec((None, tq, dv), lambda b, h, i: (b, i, h)),
        out_shape=jax.ShapeDtypeStruct((B, S, H * dv), BF16),
        scratch_shapes=[pltpu.VMEM((2, tq, LANES), F32),
                        pltpu.VMEM((2, tq, LANES), F32),
                        pltpu.VMEM((2, tq, dv), F32),
                        pltpu.VMEM((2, 2, tq, tq), F32),
                        pltpu.VMEM((2, 2, tq, tq), BF16),
                        pltpu.VMEM((2, 2, tq, LANES), F32)],
        compiler_params=_cparams(("parallel", "parallel", "arbitrary")),
        name="diff_attention",
    )(hb3, hb3, hb3, lam_qk, da_g, lami)


def _log_sigmoid(x):
    return -(jnp.maximum(-x, 0.0) + jnp.log1p(jnp.exp(-jnp.abs(x))))


def _ml_kernel(qk_ref, og_ref, gt_ref, v_ref, cw_ref, cb_ref, gb_ref, g_ref, out_ref,
               buf, c_scr, n_scr, m_scr, *, L, H, dh):
    c = pl.program_id(1)
    W = H * dh

    @pl.when(c == 0)
    def _():
        buf[0:SUBLANES, :] = jnp.zeros((SUBLANES, buf.shape[1]), F32)
        c_scr[...] = jnp.zeros(c_scr.shape, F32)
        n_scr[...] = jnp.zeros(n_scr.shape, F32)
        m_scr[...] = jnp.zeros(m_scr.shape, F32)

    buf[SUBLANES:SUBLANES + L, :] = qk_ref[...]
    cw = cw_ref[...]
    y = cb_ref[...]
    for j in range(CONV_K):
        s0 = SUBLANES - (CONV_K - 1) + j
        y = y + cw[j:j + 1, :] * buf[s0:s0 + L, :]
    buf[0:SUBLANES, :] = buf[L:L + SUBLANES, :]
    qk = y * jax.nn.sigmoid(y)

    G = gt_ref[...] + gb_ref[...]
    ls = _log_sigmoid(G)
    row = lax.broadcasted_iota(jnp.int32, (L, L), 0)
    col = lax.broadcasted_iota(jnp.int32, (L, L), 1)
    causal = col <= row
    tri = causal.astype(F32)
    tri_t = (row <= col).astype(F32)
    bcol_all = jnp.dot(tri, ls, precision=lax.Precision.HIGHEST,
                       preferred_element_type=F32)
    GT = G.T
    brow_all = jnp.dot(ls.T, tri_t, precision=lax.Precision.HIGHEST,
                       preferred_element_type=F32)

    for h in range(H):
        b_col = bcol_all[:, H + h:H + h + 1]
        i_col = G[:, h:h + 1]
        b_row = brow_all[H + h:H + h + 1, :]
        i_row = GT[h:h + 1, :]
        b_last = bcol_all[L - 1:L, H + h:H + h + 1]
        m_old = m_scr[h:h + 1, 0:1]

        a = b_col + m_old
        d = jnp.where(causal, b_col - b_row + i_row, NEG_INF)
        m_t = jnp.maximum(a, jnp.max(d, axis=1, keepdims=True))
        w_inter = jnp.exp(a - m_t)

        q = qk[:, h * dh:(h + 1) * dh]
        k = qk[:, W + h * dh:W + (h + 1) * dh] * (dh ** -0.5)
        qb = q.astype(BF16)
        kb = k.astype(BF16)
        vh = v_ref[:, h * dh:(h + 1) * dh]
        s = lax.dot_general(qb, kb, _NT, preferred_element_type=F32)
        w_intra = jnp.exp(d - m_t) * s
        c_old = c_scr[h]
        n_old = n_scr[h:h + 1, :]
        num = (w_inter * jnp.dot(qb, c_old.astype(BF16), preferred_element_type=F32)
               + jnp.dot(w_intra.astype(BF16), vh, preferred_element_type=F32))
        qn = jnp.sum(q * n_old, axis=1, keepdims=True)
        nq = w_inter * qn + jnp.sum(w_intra, axis=1, keepdims=True)
        hh = num / jnp.maximum(jnp.abs(nq), jnp.exp(-m_t))

        g_col = b_last - b_col + i_col
        g_row = b_last - b_row + i_row
        m_new = jnp.maximum(b_last + m_old, jnp.max(g_row, axis=1, keepdims=True))
        decay = jnp.exp(b_last + m_old - m_new)
        wk = jnp.exp(g_col - m_new) * k
        c_scr[h] = decay * c_old + jnp.dot(wk.T.astype(BF16), vh,
                                           preferred_element_type=F32)
        n_scr[h:h + 1, :] = decay * n_old + jnp.sum(wk, axis=0, keepdims=True)
        m_scr[h:h + 1, :] = jnp.broadcast_to(m_new, (1, m_scr.shape[1]))

        mu = jnp.mean(hh, axis=-1, keepdims=True)
        hc = hh - mu
        var = jnp.mean(hc * hc, axis=-1, keepdims=True)
        hn = hc * lax.rsqrt(var + LN_EPS)
        gate = jax.nn.sigmoid(og_ref[:, h * dh:(h + 1) * dh])
        out_ref[:, h * dh:(h + 1) * dh] = (
            hn * g_ref[:, h * dh:(h + 1) * dh] * gate).astype(out_ref.dtype)


def _mlstm(hf3, hb3, conv_w, conv_b, gbias, ml_g, L):
    B, S, _ = hf3.shape
    W = ml_g.shape[-1]
    H = ML_HEADS
    dh = W // H
    kern = functools.partial(_ml_kernel, L=L, H=H, dh=dh)
    return pl.pallas_call(
        kern,
        grid=(B, S // L),
        in_specs=[pl.BlockSpec((None, L, 2 * W), lambda b, c: (b, c, 0)),
                  pl.BlockSpec((None, L, W), lambda b, c: (b, c, 2)),
                  pl.BlockSpec((None, L, LANES), lambda b, c: (b, c, 3 * W // LANES)),
                  pl.BlockSpec((None, L, W), lambda b, c: (b, c, 3)),
                  pl.BlockSpec((CONV_K, 2 * W), lambda b, c: (0, 0)),
                  pl.BlockSpec((1, 2 * W), lambda b, c: (0, 0)),
                  pl.BlockSpec((1, LANES), lambda b, c: (0, 0)),
                  pl.BlockSpec((1, W), lambda b, c: (0, 0))],
        out_specs=pl.BlockSpec((None, L, W), lambda b, c: (b, c, 0)),
        out_shape=jax.ShapeDtypeStruct((B, S, W), BF16),
        scratch_shapes=[pltpu.VMEM((L + SUBLANES, 2 * W), F32),
                        pltpu.VMEM((H, dh, dh), F32),
                        pltpu.VMEM((SUBLANES, dh), F32),
                        pltpu.VMEM((SUBLANES, LANES), F32)],
        compiler_params=_cparams(("parallel", "arbitrary")),
        name="mlstm",
    )(hf3, hf3, hf3, hb3, conv_w, conv_b, gbias, ml_g)


def _outproj_kernel(x_ref, da_ref, hm_ref, w_ref, g_ref, b_ref, o_ref, *, alpha):
    wd = da_ref.shape[-1]
    mix = (jnp.dot(da_ref[...], w_ref[0:wd, :], preferred_element_type=F32)
           + jnp.dot(hm_ref[...], w_ref[wd:, :], preferred_element_type=F32))
    o_ref[...] = _layer_norm(alpha * x_ref[...] + mix, g_ref[...], b_ref[...])


def _outproj_ln(x2d, da2d, hm2d, w_out, g, b, alpha, tm):
    T, D = x2d.shape
    wd, wm = da2d.shape[1], hm2d.shape[1]
    kern = functools.partial(_outproj_kernel, alpha=alpha)
    return pl.pallas_call(
        kern,
        grid=(T // tm,),
        in_specs=[pl.BlockSpec((tm, D), lambda i: (i, 0)),
                  pl.BlockSpec((tm, wd), lambda i: (i, 0)),
                  pl.BlockSpec((tm, wm), lambda i: (i, 0)),
                  pl.BlockSpec((wd + wm, D), lambda i: (0, 0)),
                  pl.BlockSpec((1, D), lambda i: (0, 0)),
                  pl.BlockSpec((1, D), lambda i: (0, 0))],
        out_specs=pl.BlockSpec((tm, D), lambda i: (i, 0)),
        out_shape=jax.ShapeDtypeStruct((T, D), F32),
        compiler_params=_cparams(("parallel",)),
        name="outproj_ln",
    )(x2d, da2d, hm2d, w_out, g, b)


def _memattn_kernel(x_ref, kv_ref, wq_ref, wo_ref, g_ref, b_ref, o_ref, *, alpha, H):
    x = x_ref[...]
    D = x.shape[-1]
    dh = D // H
    q = jnp.dot(x.astype(BF16), wq_ref[...], preferred_element_type=F32) * (dh ** -0.5)
    qb = q.astype(BF16)
    outs = []
    for h in range(H):
        kh = kv_ref[:, h * dh:(h + 1) * dh]
        vh = kv_ref[:, D + h * dh:D + (h + 1) * dh]
        s = lax.dot_general(qb[:, h * dh:(h + 1) * dh], kh, _NT, preferred_element_type=F32)
        s = s - jnp.max(s, axis=-1, keepdims=True)
        e = jnp.exp(s)
        p = e / jnp.sum(e, axis=-1, keepdims=True)
        outs.append(jnp.dot(p.astype(BF16), vh, preferred_element_type=F32))
    o = jnp.concatenate(outs, axis=-1).astype(BF16)
    att = jnp.dot(o, wo_ref[...], preferred_element_type=F32)
    o_ref[...] = _layer_norm(alpha * x + att, g_ref[...], b_ref[...])


def _memattn_ln(x3, kv3, wq, wo, g, b, alpha, tm):
    B, S, D = x3.shape
    M = kv3.shape[1]
    kern = functools.partial(_memattn_kernel, alpha=alpha, H=MEM_HEADS)
    return pl.pallas_call(
        kern,
        grid=(B, S // tm),
        in_specs=[pl.BlockSpec((None, tm, D), lambda bb, i: (bb, i, 0)),
                  pl.BlockSpec((None, M, 2 * D), lambda bb, i: (bb, 0, 0)),
                  pl.BlockSpec((D, D), lambda bb, i: (0, 0)),
                  pl.BlockSpec((D, D), lambda bb, i: (0, 0)),
                  pl.BlockSpec((1, D), lambda bb, i: (0, 0)),
                  pl.BlockSpec((1, D), lambda bb, i: (0, 0))],
        out_specs=pl.BlockSpec((None, tm, D), lambda bb, i: (bb, i, 0)),
        out_shape=jax.ShapeDtypeStruct((B, S, D), F32),
        compiler_params=_cparams(("parallel", "parallel")),
        name="memattn_ln",
    )(x3, kv3, wq, wo, g, b)


NO_RANK = 127.0


def _extract_desc(v, count, n_ranked=0):
    tops = []
    rank = jnp.full(v.shape, NO_RANK, F32) if n_ranked else None
    for r in range(count):
        mx = jnp.max(v, axis=0, keepdims=True)
        tops.append(mx)
        hit = v == mx
        if r < n_ranked:
            rank = jnp.where(hit, float(r), rank)
        if r + 1 < count:
            v = jnp.where(hit, NEG_INF, v)
    return tops, rank


def _dup_bf16_words(x):
    u = lax.bitcast_convert_type(x.astype(BF16).astype(F32), jnp.uint32)
    return u | (u >> 16)


def _route_kernel(x_ref, wpq_ref, sk_ref, c1_ref, e1_ref, r2_ref, e2_ref, q_scr, *, H, nk, topk):
    q_scr[...] = jnp.dot(x_ref[...].astype(BF16), wpq_ref[...], preferred_element_type=F32)
    T = x_ref.shape[0]
    half = sk_ref.shape[-1]
    kk = topk + 1
    sub = lax.broadcasted_iota(jnp.int32, (SUBLANES, T), 0)

    def head(h, carry):
        base = pl.multiple_of(h * 2 * half, 2 * half)
        qa = q_scr[:, pl.ds(base, half)].astype(BF16)
        qb = q_scr[:, pl.ds(base + half, half)].astype(BF16)
        s1 = lax.dot_general(sk_ref[0], qa, _NT, preferred_element_type=F32)
        s2 = lax.dot_general(sk_ref[1], qb, _NT, preferred_element_type=F32)
        a, rank1 = _extract_desc(s1, kk, topk)
        b, rank2 = _extract_desc(s2, kk, topk)
        b_arr = jnp.concatenate(b + [jnp.full((1, T), NEG_INF, F32)] * ((-kk) % SUBLANES), axis=0)
        slabs = []
        for p in range(kk):
            nq = kk // (p + 1)
            for s0 in range(0, nq, SUBLANES):
                blk = a[p] + b_arr[s0:s0 + SUBLANES, :]
                if nq - s0 < SUBLANES:
                    blk = jnp.where(sub < (nq - s0), blk, NEG_INF)
                slabs.append(blk)
        cand = jnp.concatenate(slabs, axis=0)
        cs, _ = _extract_desc(cand, kk)
        tau = 0.5 * (cs[topk - 1] + cs[topk])
        z = jnp.zeros((1, T), F32)
        for r in range(topk):
            z = z + jnp.exp(cs[r] - cs[0])
        b_top = b_arr[0:topk, :]
        c1 = jnp.zeros((nk, T), F32)
        for p in range(topk):
            cnt = jnp.sum((a[p] + b_top > tau).astype(F32), axis=0, keepdims=True)
            c1 = jnp.where(rank1 == float(p), cnt, c1)
        c1_ref[h] = _dup_bf16_words(c1)
        e1_ref[h] = _dup_bf16_words(jnp.exp(s1 - a[0]))
        r2_ref[h] = rank2.astype(BF16)
        e2_ref[h] = (jnp.exp(s2 - b[0]) / z).astype(BF16)
        return carry

    lax.fori_loop(0, H, head, 0)


def _peer_route(x2d, wpq, sk, tr):
    T, D = x2d.shape
    H = PEER_HEADS
    nk = sk.shape[1]
    kern = functools.partial(_route_kernel, H=H, nk=nk, topk=PEER_TOPK)
    shp_w = jax.ShapeDtypeStruct((H, nk, T), jnp.uint32)
    shp_b = jax.ShapeDtypeStruct((H, nk, T), BF16)
    ospec = pl.BlockSpec((H, nk, tr), lambda i: (0, 0, i))
    return pl.pallas_call(
        kern,
        grid=(T // tr,),
        in_specs=[pl.BlockSpec((tr, D), lambda i: (i, 0)),
                  pl.BlockSpec(wpq.shape, lambda i: (0, 0)),
                  pl.BlockSpec(sk.shape, lambda i: (0, 0, 0))],
        out_specs=[ospec, ospec, ospec, ospec],
        out_shape=[shp_w, shp_w, shp_b, shp_b],
        scratch_shapes=[pltpu.VMEM((tr, wpq.shape[1]), F32)],
        compiler_params=_cparams(("parallel",)),
        name="peer_route",
    )(x2d, wpq, sk)


def _peer_kernel(x_ref, c1_ref, e1_ref, r2_ref, e2_ref, u_ref, vta_ref, vtb_ref, g_ref, b_ref, o_ref,
                 xb_scr, act_scr, wa_scr, wb_scr, acc_scr, bc1_scr, be1_scr,
                 *, alpha, H, nk, ib, sub):
    step = pl.program_id(1)
    last = pl.num_programs(1) - 1
    tt = x_ref.shape[0]
    stages = (ib // 2) // sub
    se = sub * nk
    rc = 2 * SUBLANES
    grp = 4

    @pl.when(step == 0)
    def _():
        xb_scr[...] = x_ref[...].astype(BF16)
        acc_scr[...] = jnp.zeros(acc_scr.shape, F32)
        wb_scr[...] = jnp.zeros(wb_scr.shape, BF16)

    def act_mm(s):
        act_scr[s] = lax.dot_general(u_ref[s * se:(s + 1) * se, :], xb_scr[...], _NT,
                                     preferred_element_type=F32)

    def gates(s, w_scr, k):
        for half in range(sub):
            ii = s * sub + half
            par = ii % 2
            for h in range(H):
                bc1_scr[par, h] = pltpu.bitcast(
                    jnp.broadcast_to(c1_ref[h, ii:ii + 1, :], (SUBLANES, tt)), BF16)
                be1_scr[par, h] = pltpu.bitcast(
                    jnp.broadcast_to(e1_ref[h, ii:ii + 1, :], (SUBLANES, tt)), BF16)
            for g in range(nk // (grp * rc)):
                accs = [jnp.zeros((rc, tt), BF16) for _ in range(grp)]
                for h in range(H):
                    cb = bc1_scr[par, h]
                    eb = be1_scr[par, h]
                    for c in range(grp):
                        j0 = (g * grp + c) * rc
                        sel = jnp.where(r2_ref[h, j0:j0 + rc, :] < cb,
                                        e2_ref[h, j0:j0 + rc, :], jnp.zeros((), BF16))
                        accs[c] = accs[c] + sel * eb
                for c in range(grp):
                    e0 = half * nk + (g * grp + c) * rc
                    a = act_scr[s, e0:e0 + rc, :]
                    gel = 0.5 * a * (1.0 + lax.erf(a * (2.0 ** -0.5)))
                    w_scr[k * se + e0:k * se + e0 + rc, :] = accs[c] * gel.astype(BF16)

    @pl.when(step < last)
    def _():
        act_mm(0)
        for phase, (w_scr, vt_ref, wp_scr) in enumerate(((wa_scr, vtb_ref, wb_scr),
                                                         (wb_scr, vta_ref, wa_scr))):
            for k in range(stages):
                s = phase * stages + k
                if s + 1 < 2 * stages:
                    act_mm(s + 1)
                gates(s, w_scr, k)
                if k == 0:
                    acc_scr[...] += jnp.dot(vt_ref[...], wp_scr[...], preferred_element_type=F32)

    @pl.when(step == last)
    def _():
        acc = acc_scr[...] + jnp.dot(vtb_ref[...], wb_scr[...], preferred_element_type=F32)
        y = alpha * x_ref[...] + acc.T
        o_ref[...] = _layer_norm(y, g_ref[...], b_ref[...])


def _peer_experts_ln(x2d, c1, e1, r2, e2, u_bf, vt_bf, g, b, alpha, tt, ib):
    T, D = x2d.shape
    H, nk, _ = c1.shape
    ne = ib * nk
    steps = nk // ib
    sub = 2
    kern = functools.partial(_peer_kernel, alpha=alpha, H=H, nk=nk, ib=ib, sub=sub)

    def cur(s):
        return jnp.minimum(s, steps - 1)

    return pl.pallas_call(
        kern,
        grid=(T // tt, steps + 1),
        in_specs=[pl.BlockSpec((tt, D), lambda t, s: (t, 0)),
                  pl.BlockSpec((H, ib, tt), lambda t, s: (0, cur(s), t)),
                  pl.BlockSpec((H, ib, tt), lambda t, s: (0, cur(s), t)),
                  pl.BlockSpec((H, nk, tt), lambda t, s: (0, 0, t)),
                  pl.BlockSpec((H, nk, tt), lambda t, s: (0, 0, t)),
                  pl.BlockSpec((ne, D), lambda t, s: (cur(s), 0)),
                  pl.BlockSpec((D, ne // 2), lambda t, s: (0, 2 * cur(s))),
                  pl.BlockSpec((D, ne // 2), lambda t, s: (0, jnp.maximum(2 * s - 1, 0))),
                  pl.BlockSpec((1, D), lambda t, s: (0, 0)),
                  pl.BlockSpec((1, D), lambda t, s: (0, 0))],
        out_specs=pl.BlockSpec((tt, D), lambda t, s: (t, 0)),
        out_shape=jax.ShapeDtypeStruct((T, D), F32),
        scratch_shapes=[pltpu.VMEM((tt, D), BF16),
                        pltpu.VMEM((ib // sub, sub * nk, tt), F32),
                        pltpu.VMEM((ne // 2, tt), BF16),
                        pltpu.VMEM((ne // 2, tt), BF16),
                        pltpu.VMEM((D, tt), F32),
                        pltpu.VMEM((2, H, 2 * SUBLANES, tt), BF16),
                        pltpu.VMEM((2, H, 2 * SUBLANES, tt), BF16)],
        compiler_params=_cparams(("parallel", "arbitrary")),
        name="peer_experts_ln",
    )(x2d, c1, e1, r2, e2, u_bf, vt_bf, vt_bf, g, b)


def _tile(n, pref):
    t = min(n, pref)
    assert n % t == 0, (n, pref)
    return t


def kernel(x, mem, w_in, i_bias, f_bias, conv_w, conv_b, lam_qk, da_norm_g, ml_norm_g, w_out,
           ln1_g, ln1_b, wq_mem, wkv_mem, wo_mem, ln2_g, ln2_b, w_pq, sub_keys, u_tab, v_tab,
           ln3_g, ln3_b):
    B, S, D = x.shape
    depth = w_in.shape[0]
    T = B * S
    M = mem.shape[1]
    alpha = (2.0 * depth) ** 0.25
    wda = DA_HEADS * da_norm_g.shape[-1]
    wml = ml_norm_g.shape[-1]
    assert w_in.shape[-1] == 3 * wda + 4 * wml + 2 * ML_HEADS
    assert wda == wml and wml % LANES == 0 and 2 * ML_HEADS <= LANES

    tm = _tile(T, 256)
    tq = _tile(S, 512)
    L = _tile(S, 128)
    tmem = _tile(S, 256)
    tr = _tile(T, 256)
    tt = _tile(T, 512)
    ib = 2 * SUBLANES

    mem2d = mem.reshape(B * M, D)
    o_q, o_k, o_v = 0, wda, 2 * wda
    o_mq = 3 * wda
    o_mk, o_mv, o_mo, o_gt = o_mq + wml, o_mq + 2 * wml, o_mq + 3 * wml, o_mq + 4 * wml

    for l in range(depth):
        lam_init = 0.8 - 0.6 * math.exp(-0.3 * l)
        wl = w_in[l]
        wb = jnp.concatenate([wl[:, o_q:o_mq], wl[:, o_mv:o_mo]], axis=1).astype(BF16)
        wf = jnp.concatenate(
            [wl[:, o_mq:o_mv], wl[:, o_mo:o_gt],
             jnp.pad(wl[:, o_gt:], ((0, 0), (0, LANES - 2 * ML_HEADS)))], axis=1).astype(BF16)
        gbias = jnp.pad(jnp.concatenate([i_bias[l], f_bias[l]]),
                        (0, LANES - 2 * ML_HEADS)).reshape(1, LANES)
        lami = jnp.full((1, da_norm_g.shape[-1]), lam_init, F32)

        x2d = x.reshape(T, D)
        hb, hf = _inproj(x2d, wb, wf, tm)
        hb3 = hb.reshape(B, S, -1)
        hf3 = hf.reshape(B, S, -1)
        da = _diff_attention(hb3, lam_qk[l], da_norm_g[l].reshape(1, -1), lami, tq)
        hm = _mlstm(hf3, hb3, conv_w[l], conv_b[l].reshape(1, -1), gbias,
                    ml_norm_g[l].reshape(1, -1), L)
        x1 = _outproj_ln(x2d, da.reshape(T, -1), hm.reshape(T, -1), w_out[l].astype(BF16),
                         ln1_g[l].reshape(1, D), ln1_b[l].reshape(1, D), alpha, tm)

        kv = _matmul(mem2d, wkv_mem[l].astype(BF16), _tile(B * M, 256), BF16)
        x2 = _memattn_ln(x1.reshape(B, S, D), kv.reshape(B, M, 2 * D), wq_mem[l].astype(BF16),
                         wo_mem[l].astype(BF16), ln2_g[l].reshape(1, D), ln2_b[l].reshape(1, D),
                         alpha, tmem)
        x2d = x2.reshape(T, D)

        c1, e1, r2, e2 = _peer_route(x2d, w_pq[l].astype(BF16), sub_keys[l].astype(BF16), tr)
        x3 = _peer_experts_ln(x2d, c1, e1, r2, e2, u_tab[l].astype(BF16),
                              v_tab[l].T.astype(BF16), ln3_g[l].reshape(1, D),
                              ln3_b[l].reshape(1, D), alpha, tt, ib)
        x = x3.reshape(B, S, D)
    return x
```

```python
import functools
import math

import jax
import jax.numpy as jnp
from jax import lax
from jax.experimental import pallas as pl
from jax.experimental.pallas import tpu as pltpu

F32 = jnp.float32
BF16 = jnp.bfloat16
LN_EPS = 1e-5
NEG_INF = float("-inf")
RSQRT2 = 2.0 ** -0.5

DA_HEADS = 4
ML_HEADS = 4
MEM_HEADS = 4
PEER_HEADS = 8
PEER_TOPK = 16
CONV_K = 4

VMEM_LIMIT_BYTES = 56 * 1024 * 1024
LANES = 128
SUBLANES = 8

_NT = (((1,), (1,)), ((), ()))


def _cparams(sem):
    return pltpu.CompilerParams(dimension_semantics=sem, vmem_limit_bytes=VMEM_LIMIT_BYTES)


def _layer_norm(y, g, b):
    mu = jnp.mean(y, axis=-1, keepdims=True)
    yc = y - mu
    var = jnp.mean(yc * yc, axis=-1, keepdims=True)
    return yc * lax.rsqrt(var + LN_EPS) * g + b


def _inproj_kernel(x_ref, wb_ref, wf_ref, hb_ref, hf_ref):
    xb = x_ref[...].astype(BF16)
    hb_ref[...] = jnp.dot(xb, wb_ref[...], preferred_element_type=F32).astype(BF16)
    hf_ref[...] = jnp.dot(xb, wf_ref[...], preferred_element_type=F32)


def _inproj(x2d, wb, wf, tm):
    T, D = x2d.shape
    nb, nf = wb.shape[1], wf.shape[1]
    return pl.pallas_call(
        _inproj_kernel,
        grid=(T // tm,),
        in_specs=[pl.BlockSpec((tm, D), lambda i: (i, 0)),
                  pl.BlockSpec((D, nb), lambda i: (0, 0)),
                  pl.BlockSpec((D, nf), lambda i: (0, 0))],
        out_specs=[pl.BlockSpec((tm, nb), lambda i: (i, 0)),
                   pl.BlockSpec((tm, nf), lambda i: (i, 0))],
        out_shape=[jax.ShapeDtypeStruct((T, nb), BF16),
                   jax.ShapeDtypeStruct((T, nf), F32)],
        compiler_params=_cparams(("parallel",)),
        name="inproj",
    )(x2d, wb, wf)


def _mm_kernel(x_ref, w_ref, o_ref):
    o_ref[...] = jnp.dot(x_ref[...].astype(BF16), w_ref[...],
                         preferred_element_type=F32).astype(o_ref.dtype)


def _matmul(x2d, w, tm, out_dtype):
    M, K = x2d.shape
    N = w.shape[1]
    return pl.pallas_call(
        _mm_kernel,
        grid=(M // tm,),
        in_specs=[pl.BlockSpec((tm, K), lambda i: (i, 0)),
                  pl.BlockSpec((K, N), lambda i: (0, 0))],
        out_specs=pl.BlockSpec((tm, N), lambda i: (i, 0)),
        out_shape=jax.ShapeDtypeStruct((M, N), out_dtype),
        compiler_params=_cparams(("parallel",)),
        name="matmul",
    )(x2d, w)


def _da_kernel(q_ref, k_ref, v_ref, lq_ref, g_ref, lami_ref, o_ref,
               m1, l1, a1, m2, l2, a2, *, tq, dk):
    qb = pl.program_id(2)
    dv = q_ref.shape[-1]
    rep = tq // LANES

    lane = lax.broadcasted_iota(jnp.int32, (1, dv), 1)
    qf = q_ref[...].astype(F32) * (dk ** -0.5)
    q1 = jnp.where(lane < dk, qf, 0.0).astype(BF16)
    q2 = jnp.where(lane >= dk, qf, 0.0).astype(BF16)

    m1[...] = jnp.full(m1.shape, NEG_INF, F32)
    m2[...] = jnp.full(m2.shape, NEG_INF, F32)
    l1[...] = jnp.zeros(l1.shape, F32)
    l2[...] = jnp.zeros(l2.shape, F32)
    a1[...] = jnp.zeros(a1.shape, F32)
    a2[...] = jnp.zeros(a2.shape, F32)

    def update(s, vblk, m_ref, l_ref, a_ref):
        m_prev = m_ref[...]
        m_cur = jnp.max(s, axis=1, keepdims=True)
        m_next = jnp.maximum(m_prev, m_cur)
        p = jnp.exp(s - jnp.concatenate([m_next] * rep, axis=1))
        alpha = jnp.exp(m_prev - m_next)
        l_ref[...] = alpha * l_ref[...] + jnp.sum(p, axis=1, keepdims=True)
        a_ref[...] = alpha * a_ref[...] + jnp.dot(p.astype(BF16), vblk,
                                                  preferred_element_type=F32)
        m_ref[...] = m_next

    def block(kb, masked):
        off = pl.multiple_of(kb * tq, tq)
        kblk = k_ref[pl.ds(off, tq), :]
        vblk = v_ref[pl.ds(off, tq), :]
        s1 = lax.dot_general(q1, kblk, _NT, preferred_element_type=F32)
        s2 = lax.dot_general(q2, kblk, _NT, preferred_element_type=F32)
        if masked:
            row = lax.broadcasted_iota(jnp.int32, (tq, tq), 0)
            col = lax.broadcasted_iota(jnp.int32, (tq, tq), 1)
            keep = col <= row
            s1 = jnp.where(keep, s1, NEG_INF)
            s2 = jnp.where(keep, s2, NEG_INF)
        update(s1, vblk, m1, l1, a1)
        update(s2, vblk, m2, l2, a2)

    def body(kb, carry):
        block(kb, False)
        return carry

    lax.fori_loop(0, qb, body, 0)
    block(qb, True)

    lq = lq_ref[...]
    lam_init = lami_ref[...][:, :1]
    lam = (jnp.exp(jnp.sum(lq[0:1] * lq[1:2], axis=1, keepdims=True))
           - jnp.exp(jnp.sum(lq[2:3] * lq[3:4], axis=1, keepdims=True)) + lam_init)
    o = a1[...] / l1[...] - lam * (a2[...] / l2[...])
    o = o * lax.rsqrt(jnp.mean(o * o, axis=-1, keepdims=True) + LN_EPS)
    o = o * g_ref[...] * (1.0 - lami_ref[...])
    o_ref[...] = o.astype(o_ref.dtype)


def _diff_attention(hb3, lam_qk, da_g, lami, tq):
    B, S, _ = hb3.shape
    dv = da_g.shape[-1]
    dk = dv // 2
    H = DA_HEADS
    assert dv == LANES
    kern = functools.partial(_da_kernel, tq=tq, dk=dk)
    return pl.pallas_call(
        kern,
        grid=(B, H, S // tq),
        in_specs=[pl.BlockSpec((None, tq, dv), lambda b, h, i: (b, i, h)),
                  pl.BlockSpec((None, S, dv), lambda b, h, i: (b, 0, H + h)),
                  pl.BlockSpec((None, S, dv), lambda b, h, i: (b, 0, 2 * H + h)),
                  pl.BlockSpec(lam_qk.shape, lambda b, h, i: (0, 0)),
                  pl.BlockSpec((1, dv), lambda b, h, i: (0, 0)),
                  pl.BlockSpec((1, dv), lambda b, h, i: (0, 0))],
        out_specs=pl.BlockSpec((None, tq, dv), lambda b, h, i: (b, i, h)),
        out_shape=jax.ShapeDtypeStruct((B, S, H * dv), BF16),
        scratch_shapes=[pltpu.VMEM((tq, LANES), F32), pltpu.VMEM((tq, LANES), F32),
                        pltpu.VMEM((tq, dv), F32),
                        pltpu.VMEM((tq, LANES), F32), pltpu.VMEM((tq, LANES), F32),
                        pltpu.VMEM((tq, dv), F32)],
        compiler_params=_cparams(("parallel", "parallel", "arbitrary")),
        name="diff_attention",
    )(hb3, hb3, hb3, lam_qk, da_g, lami)


def _log_sigmoid(x):
    return -(jnp.maximum(-x, 0.0) + jnp.log1p(jnp.exp(-jnp.abs(x))))


def _ml_kernel(qk_ref, og_ref, gt_ref, v_ref, cw_ref, cb_ref, gb_ref, g_ref, out_ref,
               buf, c_scr, n_scr, m_scr, *, L, H, dh):
    c = pl.program_id(1)
    W = H * dh

    @pl.when(c == 0)
    def _():
        buf[0:SUBLANES, :] = jnp.zeros((SUBLANES, buf.shape[1]), F32)
        c_scr[...] = jnp.zeros(c_scr.shape, F32)
        n_scr[...] = jnp.zeros(n_scr.shape, F32)
        m_scr[...] = jnp.zeros(m_scr.shape, F32)

    buf[SUBLANES:SUBLANES + L, :] = qk_ref[...]
    cw = cw_ref[...]
    y = cb_ref[...]
    for j in range(CONV_K):
        s0 = SUBLANES - (CONV_K - 1) + j
        y = y + cw[j:j + 1, :] * buf[s0:s0 + L, :]
    buf[0:SUBLANES, :] = buf[L:L + SUBLANES, :]
    qk = y * jax.nn.sigmoid(y)

    G = gt_ref[...] + gb_ref[...]
    ls = _log_sigmoid(G)
    row = lax.broadcasted_iota(jnp.int32, (L, L), 0)
    col = lax.broadcasted_iota(jnp.int32, (L, L), 1)
    causal = col <= row
    tri = causal.astype(F32)
    tri_t = (row <= col).astype(F32)
    bcol_all = jnp.dot(tri, ls, precision=lax.Precision.HIGHEST,
                       preferred_element_type=F32)
    GT = G.T
    brow_all = jnp.dot(ls.T, tri_t, precision=lax.Precision.HIGHEST,
                       preferred_element_type=F32)

    for h in range(H):
        b_col = bcol_all[:, H + h:H + h + 1]
        i_col = G[:, h:h + 1]
        b_row = brow_all[H + h:H + h + 1, :]
        i_row = GT[h:h + 1, :]
        b_last = bcol_all[L - 1:L, H + h:H + h + 1]
        m_old = m_scr[h:h + 1, 0:1]

        a = b_col + m_old
        d = jnp.where(causal, b_col - b_row + i_row, NEG_INF)
        m_t = jnp.maximum(a, jnp.max(d, axis=1, keepdims=True))
        w_inter = jnp.exp(a - m_t)

        q = qk[:, h * dh:(h + 1) * dh]
        k = qk[:, W + h * dh:W + (h + 1) * dh] * (dh ** -0.5)
        qb = q.astype(BF16)
        kb = k.astype(BF16)
        vh = v_ref[:, h * dh:(h + 1) * dh]
        s = lax.dot_general(qb, kb, _NT, preferred_element_type=F32)
        w_intra = jnp.exp(d - m_t) * s
        c_old = c_scr[h]
        n_old = n_scr[h:h + 1, :]
        num = (w_inter * jnp.dot(qb, c_old.astype(BF16), preferred_element_type=F32)
               + jnp.dot(w_intra.astype(BF16), vh, preferred_element_type=F32))
        qn = jnp.sum(q * n_old, axis=1, keepdims=True)
        nq = w_inter * qn + jnp.sum(w_intra, axis=1, keepdims=True)
        hh = num / jnp.maximum(jnp.abs(nq), jnp.exp(-m_t))

        g_col = b_last - b_col + i_col
        g_row = b_last - b_row + i_row
        m_new = jnp.maximum(b_last + m_old, jnp.max(g_row, axis=1, keepdims=True))
        decay = jnp.exp(b_last + m_old - m_new)
        wk = jnp.exp(g_col - m_new) * k
        c_scr[h] = decay * c_old + jnp.dot(wk.T.astype(BF16), vh,
                                           preferred_element_type=F32)
        n_scr[h:h + 1, :] = decay * n_old + jnp.sum(wk, axis=0, keepdims=True)
        m_scr[h:h + 1, :] = jnp.broadcast_to(m_new, (1, m_scr.shape[1]))

        mu = jnp.mean(hh, axis=-1, keepdims=True)
        hc = hh - mu
        var = jnp.mean(hc * hc, axis=-1, keepdims=True)
        hn = hc * lax.rsqrt(var + LN_EPS)
        gate = jax.nn.sigmoid(og_ref[:, h * dh:(h + 1) * dh])
        out_ref[:, h * dh:(h + 1) * dh] = (
            hn * g_ref[:, h * dh:(h + 1) * dh] * gate).astype(out_ref.dtype)


def _mlstm(hf3, hb3, conv_w, conv_b, gbias, ml_g, L):
    B, S, _ = hf3.shape
    W = ml_g.shape[-1]
    H = ML_HEADS
    dh = W // H
    kern = functools.partial(_ml_kernel, L=L, H=H, dh=dh)
    return pl.pallas_call(
        kern,
        grid=(B, S // L),
        in_specs=[pl.BlockSpec((None, L, 2 * W), lambda b, c: (b, c, 0)),
                  pl.BlockSpec((None, L, W), lambda b, c: (b, c, 2)),
                  pl.BlockSpec((None, L, LANES), lambda b, c: (b, c, 3 * W // LANES)),
                  pl.BlockSpec((None, L, W), lambda b, c: (b, c, 3)),
                  pl.BlockSpec((CONV_K, 2 * W), lambda b, c: (0, 0)),
                  pl.BlockSpec((1, 2 * W), lambda b, c: (0, 0)),
                  pl.BlockSpec((1, LANES), lambda b, c: (0, 0)),
                  pl.BlockSpec((1, W), lambda b, c: (0, 0))],
        out_specs=pl.BlockSpec((None, L, W), lambda b, c: (b, c, 0)),
        out_shape=jax.ShapeDtypeStruct((B, S, W), BF16),
        scratch_shapes=[pltpu.VMEM((L + SUBLANES, 2 * W), F32),
                        pltpu.VMEM((H, dh, dh), F32),
                        pltpu.VMEM((SUBLANES, dh), F32),
                        pltpu.VMEM((SUBLANES, LANES), F32)],
        compiler_params=_cparams(("parallel", "arbitrary")),
        name="mlstm",
    )(hf3, hf3, hf3, hb3, conv_w, conv_b, gbias, ml_g)


def _outproj_kernel(x_ref, da_ref, hm_ref, w_ref, g_ref, b_ref, o_ref, *, alpha):
    wd = da_ref.shape[-1]
    mix = (jnp.dot(da_ref[...], w_ref[0:wd, :], preferred_element_type=F32)
           + jnp.dot(hm_ref[...], w_ref[wd:, :], preferred_element_type=F32))
    o_ref[...] = _layer_norm(alpha * x_ref[...] + mix, g_ref[...], b_ref[...])


def _outproj_ln(x2d, da2d, hm2d, w_out, g, b, alpha, tm):
    T, D = x2d.shape
    wd, wm = da2d.shape[1], hm2d.shape[1]
    kern = functools.partial(_outproj_kernel, alpha=alpha)
    return pl.pallas_call(
        kern,
        grid=(T // tm,),
        in_specs=[pl.BlockSpec((tm, D), lambda i: (i, 0)),
                  pl.BlockSpec((tm, wd), lambda i: (i, 0)),
                  pl.BlockSpec((tm, wm), lambda i: (i, 0)),
                  pl.BlockSpec((wd + wm, D), lambda i: (0, 0)),
                  pl.BlockSpec((1, D), lambda i: (0, 0)),
                  pl.BlockSpec((1, D), lambda i: (0, 0))],
        out_specs=pl.BlockSpec((tm, D), lambda i: (i, 0)),
        out_shape=jax.ShapeDtypeStruct((T, D), F32),
        compiler_params=_cparams(("parallel",)),
        name="outproj_ln",
    )(x2d, da2d, hm2d, w_out, g, b)


def _memattn_kernel(x_ref, kv_ref, wq_ref, wo_ref, g_ref, b_ref, o_ref, *, alpha, H):
    x = x_ref[...]
    D = x.shape[-1]
    dh = D // H
    q = jnp.dot(x.astype(BF16), wq_ref[...], preferred_element_type=F32) * (dh ** -0.5)
    qb = q.astype(BF16)
    outs = []
    for h in range(H):
        kh = kv_ref[:, h * dh:(h + 1) * dh]
        vh = kv_ref[:, D + h * dh:D + (h + 1) * dh]
        s = lax.dot_general(qb[:, h * dh:(h + 1) * dh], kh, _NT, preferred_element_type=F32)
        s = s - jnp.max(s, axis=-1, keepdims=True)
        e = jnp.exp(s)
        p = e / jnp.sum(e, axis=-1, keepdims=True)
        outs.append(jnp.dot(p.astype(BF16), vh, preferred_element_type=F32))
    o = jnp.concatenate(outs, axis=-1).astype(BF16)
    att = jnp.dot(o, wo_ref[...], preferred_element_type=F32)
    o_ref[...] = _layer_norm(alpha * x + att, g_ref[...], b_ref[...])


def _memattn_ln(x3, kv3, wq, wo, g, b, alpha, tm):
    B, S, D = x3.shape
    M = kv3.shape[1]
    kern = functools.partial(_memattn_kernel, alpha=alpha, H=MEM_HEADS)
    return pl.pallas_call(
        kern,
        grid=(B, S // tm),
        in_specs=[pl.BlockSpec((None, tm, D), lambda bb, i: (bb, i, 0)),
                  pl.BlockSpec((None, M, 2 * D), lambda bb, i: (bb, 0, 0)),
                  pl.BlockSpec((D, D), lambda bb, i: (0, 0)),
                  pl.BlockSpec((D, D), lambda bb, i: (0, 0)),
                  pl.BlockSpec((1, D), lambda bb, i: (0, 0)),
                  pl.BlockSpec((1, D), lambda bb, i: (0, 0))],
        out_specs=pl.BlockSpec((None, tm, D), lambda bb, i: (bb, i, 0)),
        out_shape=jax.ShapeDtypeStruct((B, S, D), F32),
        compiler_params=_cparams(("parallel", "parallel")),
        name="memattn_ln",
    )(x3, kv3, wq, wo, g, b)


def _sort_network(n):
    def merge(lo, hi, r):
        step = r * 2
        if step < hi - lo:
            yield from merge(lo, hi, step)
            yield from merge(lo + r, hi, step)
            yield from [(i, i + r) for i in range(lo + r, hi - r, step)]
        else:
            yield (lo, lo + r)

    def sort(lo, hi):
        if hi - lo >= 1:
            mid = lo + (hi - lo) // 2
            yield from sort(lo, mid)
            yield from sort(mid + 1, hi)
            yield from merge(lo, hi, 1)

    return list(sort(0, n - 1))


def _top_desc_sorted(s, count):
    rows = s.shape[0]
    n = rows // SUBLANES
    lists = [s[g * SUBLANES:(g + 1) * SUBLANES, :] for g in range(n)]
    for i, j in _sort_network(n):
        hi = jnp.maximum(lists[i], lists[j])
        lo = jnp.minimum(lists[i], lists[j])
        lists[i], lists[j] = hi, lo
    tops = []
    for r in range(count):
        mx = jnp.max(lists[0], axis=0, keepdims=True)
        tops.append(mx)
        if r + 1 < count:
            hit = lists[0] == mx
            depth = min(n, count - 1 - r)
            for l in range(depth):
                nxt = lists[l + 1] if l + 1 < n else NEG_INF
                lists[l] = jnp.where(hit, nxt, lists[l])
    return tops


def _extract_desc(v, count):
    tops = []
    for r in range(count):
        mx = jnp.max(v, axis=0, keepdims=True)
        tops.append(mx)
        if r + 1 < count:
            v = jnp.where(v == mx, NEG_INF, v)
    return tops


def _count_greater(x, t):
    assert len(t) == 16
    m8 = t[7] > x
    m4 = jnp.where(m8, t[11], t[3]) > x
    m2 = jnp.where(m8, jnp.where(m4, t[13], t[9]), jnp.where(m4, t[5], t[1])) > x
    th = jnp.where(m8,
                   jnp.where(m4, jnp.where(m2, t[14], t[12]), jnp.where(m2, t[10], t[8])),
                   jnp.where(m4, jnp.where(m2, t[6], t[4]), jnp.where(m2, t[2], t[0])))
    m1 = th > x
    cnt = (jnp.where(m8, 8.0, 0.0) + jnp.where(m4, 4.0, 0.0)
           + jnp.where(m2, 2.0, 0.0) + jnp.where(m1, 1.0, 0.0))
    return cnt + jnp.where(t[15] > x, 1.0, 0.0)


def _dup_bf16_words(x):
    u = lax.bitcast_convert_type(x.astype(BF16).astype(F32), jnp.uint32)
    return u | (u >> 16)


def _route_kernel(x_ref, wpq_ref, sk_ref, c1_ref, e1_ref, r2_ref, e2_ref, q_scr, *, H, nk, topk):
    q_scr[...] = jnp.dot(x_ref[...].astype(BF16), wpq_ref[...], preferred_element_type=F32)
    T = x_ref.shape[0]
    half = sk_ref.shape[-1]
    kk = topk + 1
    sub = lax.broadcasted_iota(jnp.int32, (SUBLANES, T), 0)

    def head(h, carry):
        base = pl.multiple_of(h * 2 * half, 2 * half)
        qa = q_scr[:, pl.ds(base, half)].astype(BF16)
        qb = q_scr[:, pl.ds(base + half, half)].astype(BF16)
        s1 = lax.dot_general(sk_ref[0], qa, _NT, preferred_element_type=F32)
        s2 = lax.dot_general(sk_ref[1], qb, _NT, preferred_element_type=F32)
        a = _top_desc_sorted(s1, kk)
        b = _top_desc_sorted(s2, kk)
        pad = [jnp.full((1, T), NEG_INF, F32)] * ((-kk) % SUBLANES)
        a_arr = jnp.concatenate(a + pad, axis=0)
        b_arr = jnp.concatenate(b + pad, axis=0)
        slabs = []
        for p in range(2):
            for s0 in range(0, kk // (p + 1), SUBLANES):
                slabs.append(a[p] + b_arr[s0:s0 + SUBLANES, :])
        for q in range(2):
            for s0 in range(0, kk // (q + 1), SUBLANES):
                blk = b[q] + a_arr[s0:s0 + SUBLANES, :]
                slabs.append(jnp.where(sub >= 2, blk, NEG_INF) if s0 == 0 else blk)
        rest = [a[p] + b[q] for p in range(2, kk) for q in range(2, kk) if (p + 1) * (q + 1) <= kk]
        rest = rest + [jnp.full((1, T), NEG_INF, F32)] * ((-len(rest)) % SUBLANES)
        slabs.append(jnp.concatenate(rest, axis=0))
        cs = _extract_desc(jnp.concatenate(slabs, axis=0), kk)
        tau = 0.5 * (cs[topk - 1] + cs[topk])
        z = jnp.zeros((1, T), F32)
        for r in range(topk):
            z = z + jnp.exp(cs[r] - cs[0])
        c1_ref[h] = _dup_bf16_words(_count_greater(tau - s1, b[:topk]))
        e1_ref[h] = _dup_bf16_words(jnp.exp(s1 - a[0]))
        r2_ref[h] = _count_greater(s2, b[:topk]).astype(BF16)
        e2_ref[h] = (jnp.exp(s2 - b[0]) / z).astype(BF16)
        return carry

    lax.fori_loop(0, H, head, 0)


def _peer_route(x2d, wpq, sk, tr):
    T, D = x2d.shape
    H = PEER_HEADS
    nk = sk.shape[1]
    kern = functools.partial(_route_kernel, H=H, nk=nk, topk=PEER_TOPK)
    shp_w = jax.ShapeDtypeStruct((H, nk, T), jnp.uint32)
    shp_b = jax.ShapeDtypeStruct((H, nk, T), BF16)
    ospec = pl.BlockSpec((H, nk, tr), lambda i: (0, 0, i))
    return pl.pallas_call(
        kern,
        grid=(T // tr,),
        in_specs=[pl.BlockSpec((tr, D), lambda i: (i, 0)),
                  pl.BlockSpec(wpq.shape, lambda i: (0, 0)),
                  pl.BlockSpec(sk.shape, lambda i: (0, 0, 0))],
        out_specs=[ospec, ospec, ospec, ospec],
        out_shape=[shp_w, shp_w, shp_b, shp_b],
        scratch_shapes=[pltpu.VMEM((tr, wpq.shape[1]), F32)],
        compiler_params=_cparams(("parallel",)),
        name="peer_route",
    )(x2d, wpq, sk)


def _peer_kernel(x_ref, c1_ref, e1_ref, r2_ref, e2_ref, u_ref, vt_ref, g_ref, b_ref, o_ref,
                 xb_scr, act_scr, w_scr, acc_scr, bc1_scr, be1_scr, *, alpha, H, nk, ib, sub):
    step = pl.program_id(1)
    tt = x_ref.shape[0]
    stages = ib // sub
    se = sub * nk
    rc = 2 * SUBLANES
    grp = 4

    @pl.when(step == 0)
    def _():
        xb_scr[...] = x_ref[...].astype(BF16)
        acc_scr[...] = jnp.zeros(acc_scr.shape, F32)

    def act_mm(k):
        act_scr[k] = lax.dot_general(u_ref[k * se:(k + 1) * se, :], xb_scr[...], _NT,
                                     preferred_element_type=F32)

    def out_mm(k):
        acc_scr[...] += jnp.dot(vt_ref[:, k * se:(k + 1) * se], w_scr[k],
                                preferred_element_type=F32)

    def gates(k):
        for half in range(sub):
            ii = k * sub + half
            par = ii % 2
            for h in range(H):
                bc1_scr[par, h] = pltpu.bitcast(
                    jnp.broadcast_to(c1_ref[h, ii:ii + 1, :], (SUBLANES, tt)), BF16)
                be1_scr[par, h] = pltpu.bitcast(
                    jnp.broadcast_to(e1_ref[h, ii:ii + 1, :], (SUBLANES, tt)), BF16)
            for g in range(nk // (grp * rc)):
                accs = [jnp.zeros((rc, tt), BF16) for _ in range(grp)]
                for h in range(H):
                    cb = bc1_scr[par, h]
                    eb = be1_scr[par, h]
                    for c in range(grp):
                        j0 = (g * grp + c) * rc
                        sel = jnp.where(r2_ref[h, j0:j0 + rc, :] < cb,
                                        e2_ref[h, j0:j0 + rc, :], jnp.zeros((), BF16))
                        accs[c] = accs[c] + sel * eb
                for c in range(grp):
                    e0 = half * nk + (g * grp + c) * rc
                    a = act_scr[k, e0:e0 + rc, :]
                    gel = a * (1.0 + lax.erf(a))
                    w_scr[k, e0:e0 + rc, :] = accs[c] * gel.astype(BF16)

    act_mm(0)
    for k in range(stages):
        if k + 1 < stages:
            act_mm(k + 1)
        gates(k)
        if k >= 1:
            out_mm(k - 1)
    out_mm(stages - 1)

    @pl.when(step == pl.num_programs(1) - 1)
    def _():
        y = alpha * x_ref[...] + acc_scr[...].T
        o_ref[...] = _layer_norm(y, g_ref[...], b_ref[...])


def _peer_experts_ln(x2d, c1, e1, r2, e2, u_bf, vt_bf, g, b, alpha, tt, ib):
    T, D = x2d.shape
    H, nk, _ = c1.shape
    ne = ib * nk
    steps = nk // ib
    sub = 4
    kern = functools.partial(_peer_kernel, alpha=alpha, H=H, nk=nk, ib=ib, sub=sub)
    return pl.pallas_call(
        kern,
        grid=(T // tt, steps),
        in_specs=[pl.BlockSpec((tt, D), lambda t, s: (t, 0)),
                  pl.BlockSpec((H, ib, tt), lambda t, s: (0, s, t)),
                  pl.BlockSpec((H, ib, tt), lambda t, s: (0, s, t)),
                  pl.BlockSpec((H, nk, tt), lambda t, s: (0, 0, t)),
                  pl.BlockSpec((H, nk, tt), lambda t, s: (0, 0, t)),
                  pl.BlockSpec((ne, D), lambda t, s: (s, 0)),
                  pl.BlockSpec((D, ne), lambda t, s: (0, s)),
                  pl.BlockSpec((1, D), lambda t, s: (0, 0)),
                  pl.BlockSpec((1, D), lambda t, s: (0, 0))],
        out_specs=pl.BlockSpec((tt, D), lambda t, s: (t, 0)),
        out_shape=jax.ShapeDtypeStruct((T, D), F32),
        scratch_shapes=[pltpu.VMEM((tt, D), BF16),
                        pltpu.VMEM((ib // sub, sub * nk, tt), F32),
                        pltpu.VMEM((ib // sub, sub * nk, tt), BF16),
                        pltpu.VMEM((D, tt), F32),
                        pltpu.VMEM((2, H, 2 * SUBLANES, tt), BF16),
                        pltpu.VMEM((2, H, 2 * SUBLANES, tt), BF16)],
        compiler_params=_cparams(("parallel", "arbitrary")),
        name="peer_experts_ln",
    )(x2d, c1, e1, r2, e2, u_bf, vt_bf, g, b)


def _tile(n, pref):
    t = min(n, pref)
    assert n % t == 0, (n, pref)
    return t


def kernel(x, mem, w_in, i_bias, f_bias, conv_w, conv_b, lam_qk, da_norm_g, ml_norm_g, w_out,
           ln1_g, ln1_b, wq_mem, wkv_mem, wo_mem, ln2_g, ln2_b, w_pq, sub_keys, u_tab, v_tab,
           ln3_g, ln3_b):
    B, S, D = x.shape
    depth = w_in.shape[0]
    T = B * S
    M = mem.shape[1]
    alpha = (2.0 * depth) ** 0.25
    wda = DA_HEADS * da_norm_g.shape[-1]
    wml = ml_norm_g.shape[-1]
    assert w_in.shape[-1] == 3 * wda + 4 * wml + 2 * ML_HEADS
    assert wda == wml and wml % LANES == 0 and 2 * ML_HEADS <= LANES

    tm = _tile(T, 256)
    tq = _tile(S, 512)
    L = _tile(S, 128)
    tmem = _tile(S, 256)
    tr = _tile(T, 256)
    tt = _tile(T, 512)
    ib = 2 * SUBLANES

    mem2d = mem.reshape(B * M, D)
    o_q, o_k, o_v = 0, wda, 2 * wda
    o_mq = 3 * wda
    o_mk, o_mv, o_mo, o_gt = o_mq + wml, o_mq + 2 * wml, o_mq + 3 * wml, o_mq + 4 * wml

    for l in range(depth):
        lam_init = 0.8 - 0.6 * math.exp(-0.3 * l)
        wl = w_in[l]
        wb = jnp.concatenate([wl[:, o_q:o_mq], wl[:, o_mv:o_mo]], axis=1).astype(BF16)
        wf = jnp.concatenate(
            [wl[:, o_mq:o_mv], wl[:, o_mo:o_gt],
             jnp.pad(wl[:, o_gt:], ((0, 0), (0, LANES - 2 * ML_HEADS)))], axis=1).astype(BF16)
        gbias = jnp.pad(jnp.concatenate([i_bias[l], f_bias[l]]),
                        (0, LANES - 2 * ML_HEADS)).reshape(1, LANES)
        lami = jnp.full((1, da_norm_g.shape[-1]), lam_init, F32)

        x2d = x.reshape(T, D)
        hb, hf = _inproj(x2d, wb, wf, tm)
        hb3 = hb.reshape(B, S, -1)
        hf3 = hf.reshape(B, S, -1)
        da = _diff_attention(hb3, lam_qk[l], da_norm_g[l].reshape(1, -1), lami, tq)
        hm = _mlstm(hf3, hb3, conv_w[l], conv_b[l].reshape(1, -1), gbias,
                    ml_norm_g[l].reshape(1, -1), L)
        x1 = _outproj_ln(x2d, da.reshape(T, -1), hm.reshape(T, -1), w_out[l].astype(BF16),
                         ln1_g[l].reshape(1, D), ln1_b[l].reshape(1, D), alpha, tm)

        kv = _matmul(mem2d, wkv_mem[l].astype(BF16), _tile(B * M, 256), BF16)
        x2 = _memattn_ln(x1.reshape(B, S, D), kv.reshape(B, M, 2 * D), wq_mem[l].astype(BF16),
                         wo_mem[l].astype(BF16), ln2_g[l].reshape(1, D), ln2_b[l].reshape(1, D),
                         alpha, tmem)
        x2d = x2.reshape(T, D)

        c1, e1, r2, e2 = _peer_route(x2d, w_pq[l].astype(BF16), sub_keys[l].astype(BF16), tr)
        x3 = _peer_experts_ln(x2d, c1, e1, r2, e2, (u_tab[l] * RSQRT2).astype(BF16),
                              (v_tab[l].T * RSQRT2).astype(BF16), ln3_g[l].reshape(1, D),
                              ln3_b[l].reshape(1, D), alpha, tt, ib)
        x = x3.reshape(B, S, D)
    return x
```

```python
import functools
import math

import jax
import jax.numpy as jnp
from jax import lax
from jax.experimental import pallas as pl
from jax.experimental.pallas import tpu as pltpu

F32 = jnp.float32
BF16 = jnp.bfloat16
LN_EPS = 1e-5
NEG_INF = float("-inf")
RSQRT2 = 2.0 ** -0.5

DA_HEADS = 4
ML_HEADS = 4
MEM_HEADS = 4
PEER_HEADS = 8
PEER_TOPK = 16
CONV_K = 4

VMEM_LIMIT_BYTES = 56 * 1024 * 1024
LANES = 128
SUBLANES = 8

_NT = (((1,), (1,)), ((), ()))


def _cparams(sem):
    return pltpu.CompilerParams(dimension_semantics=sem, vmem_limit_bytes=VMEM_LIMIT_BYTES)


def _layer_norm(y, g, b):
    mu = jnp.mean(y, axis=-1, keepdims=True)
    yc = y - mu
    var = jnp.mean(yc * yc, axis=-1, keepdims=True)
    return yc * lax.rsqrt(var + LN_EPS) * g + b


def _inproj_kernel(x_ref, wb_ref, wf_ref, hb_ref, hf_ref):
    xb = x_ref[...].astype(BF16)
    hb_ref[...] = jnp.dot(xb, wb_ref[...], preferred_element_type=F32).astype(BF16)
    hf_ref[...] = jnp.dot(xb, wf_ref[...], preferred_element_type=F32)


def _inproj(x2d, wb, wf, tm):
    T, D = x2d.shape
    nb, nf = wb.shape[1], wf.shape[1]
    return pl.pallas_call(
        _inproj_kernel,
        grid=(T // tm,),
        in_specs=[pl.BlockSpec((tm, D), lambda i: (i, 0)),
                  pl.BlockSpec((D, nb), lambda i: (0, 0)),
                  pl.BlockSpec((D, nf), lambda i: (0, 0))],
        out_specs=[pl.BlockSpec((tm, nb), lambda i: (i, 0)),
                   pl.BlockSpec((tm, nf), lambda i: (i, 0))],
        out_shape=[jax.ShapeDtypeStruct((T, nb), BF16),
                   jax.ShapeDtypeStruct((T, nf), F32)],
        compiler_params=_cparams(("parallel",)),
        name="inproj",
    )(x2d, wb, wf)


def _mm_kernel(x_ref, w_ref, o_ref):
    o_ref[...] = jnp.dot(x_ref[...].astype(BF16), w_ref[...],
                         preferred_element_type=F32).astype(o_ref.dtype)


def _matmul(x2d, w, tm, out_dtype):
    M, K = x2d.shape
    N = w.shape[1]
    return pl.pallas_call(
        _mm_kernel,
        grid=(M // tm,),
        in_specs=[pl.BlockSpec((tm, K), lambda i: (i, 0)),
                  pl.BlockSpec((K, N), lambda i: (0, 0))],
        out_specs=pl.BlockSpec((tm, N), lambda i: (i, 0)),
        out_shape=jax.ShapeDtypeStruct((M, N), out_dtype),
        compiler_params=_cparams(("parallel",)),
        name="matmul",
    )(x2d, w)


def _da_kernel(q_ref, k_ref, v_ref, lq_ref, g_ref, lami_ref, o_ref,
               m1, a1, m2, a2, *, tq, ts, dk):
    qb = pl.program_id(2)
    dv = q_ref.shape[-1]
    rep = tq // LANES

    lane = lax.broadcasted_iota(jnp.int32, (1, dv), 1)
    qf = q_ref[...].astype(F32) * (dk ** -0.5 * math.log2(math.e))
    q1 = jnp.where(lane < dk, qf, 0.0).astype(BF16)
    q2 = jnp.where(lane >= dk, qf, 0.0).astype(BF16)
    ones = jnp.ones((tq, dv), BF16)

    m1[...] = jnp.full(m1.shape, NEG_INF, F32)
    m2[...] = jnp.full(m2.shape, NEG_INF, F32)
    a1[...] = jnp.zeros(a1.shape, F32)
    a2[...] = jnp.zeros(a2.shape, F32)

    def update(s, vones, m_ref, a_ref, r0):
        m_prev = m_ref[r0:r0 + ts, :]
        m_cur = jnp.max(s, axis=1, keepdims=True)
        m_next = jnp.maximum(m_prev, m_cur)
        p = jnp.exp2(s - jnp.concatenate([m_next] * rep, axis=1))
        alpha = jnp.exp2(m_prev - m_next)
        a_ref[r0:r0 + ts, :] = (jnp.concatenate([alpha, alpha], axis=1) * a_ref[r0:r0 + ts, :]
                                + jnp.dot(p.astype(BF16), vones, preferred_element_type=F32))
        m_ref[r0:r0 + ts, :] = m_next

    def block(kb, masked):
        off = pl.multiple_of(kb * tq, tq)
        kblk = k_ref[pl.ds(off, tq), :]
        vblk = jnp.concatenate([v_ref[pl.ds(off, tq), :], ones], axis=1)
        for st in range(tq // ts):
            r0 = st * ts
            s1 = lax.dot_general(q1[r0:r0 + ts], kblk, _NT, preferred_element_type=F32)
            s2 = lax.dot_general(q2[r0:r0 + ts], kblk, _NT, preferred_element_type=F32)
            if masked:
                row = lax.broadcasted_iota(jnp.int32, (ts, tq), 0) + r0
                col = lax.broadcasted_iota(jnp.int32, (ts, tq), 1)
                keep = col <= row
                s1 = jnp.where(keep, s1, NEG_INF)
                s2 = jnp.where(keep, s2, NEG_INF)
            update(s1, vblk, m1, a1, r0)
            update(s2, vblk, m2, a2, r0)

    def body(kb, carry):
        block(kb, False)
        return carry

    lax.fori_loop(0, qb, body, 0)
    block(qb, True)

    lq = lq_ref[...]
    lam_init = lami_ref[...][:, :1]
    lam = (jnp.exp(jnp.sum(lq[0:1] * lq[1:2], axis=1, keepdims=True))
           - jnp.exp(jnp.sum(lq[2:3] * lq[3:4], axis=1, keepdims=True)) + lam_init)
    o = a1[:, :dv] / a1[:, dv:] - lam * (a2[:, :dv] / a2[:, dv:])
    o = o * lax.rsqrt(jnp.mean(o * o, axis=-1, keepdims=True) + LN_EPS)
    o = o * g_ref[...] * (1.0 - lami_ref[...])
    o_ref[...] = o.astype(o_ref.dtype)


def _diff_attention(hb3, lam_qk, da_g, lami, tq):
    B, S, _ = hb3.shape
    dv = da_g.shape[-1]
    dk = dv // 2
    H = DA_HEADS
    assert dv == LANES
    kern = functools.partial(_da_kernel, tq=tq, ts=tq, dk=dk)
    return pl.pallas_call(
        kern,
        grid=(B, H, S // tq),
        in_specs=[pl.BlockSpec((None, tq, dv), lambda b, h, i: (b, i, h)),
                  pl.BlockSpec((None, S, dv), lambda b, h, i: (b, 0, H + h)),
                  pl.BlockSpec((None, S, dv), lambda b, h, i: (b, 0, 2 * H + h)),
                  pl.BlockSpec(lam_qk.shape, lambda b, h, i: (0, 0)),
                  pl.BlockSpec((1, dv), lambda b, h, i: (0, 0)),
                  pl.BlockSpec((1, dv), lambda b, h, i: (0, 0))],
        out_specs=pl.BlockSpec((None, tq, dv), lambda b, h, i: (b, i, h)),
        out_shape=jax.ShapeDtypeStruct((B, S, H * dv), BF16),
        scratch_shapes=[pltpu.VMEM((tq, LANES), F32), pltpu.VMEM((tq, 2 * dv), F32),
                        pltpu.VMEM((tq, LANES), F32), pltpu.VMEM((tq, 2 * dv), F32)],
        compiler_params=_cparams(("parallel", "parallel", "arbitrary")),
        name="diff_attention",
    )(hb3, hb3, hb3, lam_qk, da_g, lami)


def _log_sigmoid(x):
    return -(jnp.maximum(-x, 0.0) + jnp.log1p(jnp.exp(-jnp.abs(x))))


def _ml_kernel(qk_ref, og_ref, gt_ref, v_ref, cw_ref, cb_ref, gb_ref, g_ref, out_ref,
               buf, c_scr, n_scr, m_scr, *, L, H, dh):
    c = pl.program_id(1)
    W = H * dh

    @pl.when(c == 0)
    def _():
        buf[0:SUBLANES, :] = jnp.zeros((SUBLANES, buf.shape[1]), F32)
        c_scr[...] = jnp.zeros(c_scr.shape, F32)
        n_scr[...] = jnp.zeros(n_scr.shape, F32)
        m_scr[...] = jnp.zeros(m_scr.shape, F32)

    buf[SUBLANES:SUBLANES + L, :] = qk_ref[...]
    cw = cw_ref[...]
    y = cb_ref[...]
    for j in range(CONV_K):
        s0 = SUBLANES - (CONV_K - 1) + j
        y = y + cw[j:j + 1, :] * buf[s0:s0 + L, :]
    buf[0:SUBLANES, :] = buf[L:L + SUBLANES, :]
    qk = y * jax.nn.sigmoid(y)

    G = gt_ref[...] + gb_ref[...]
    ls = _log_sigmoid(G)
    row = lax.broadcasted_iota(jnp.int32, (L, L), 0)
    col = lax.broadcasted_iota(jnp.int32, (L, L), 1)
    causal = col <= row
    tri = causal.astype(F32)
    tri_t = (row <= col).astype(F32)
    bcol_all = jnp.dot(tri, ls, precision=lax.Precision.HIGHEST,
                       preferred_element_type=F32)
    GT = G.T
    brow_all = jnp.dot(ls.T, tri_t, precision=lax.Precision.HIGHEST,
                       preferred_element_type=F32)

    for h in range(H):
        b_col = bcol_all[:, H + h:H + h + 1]
        i_col = G[:, h:h + 1]
        b_row = brow_all[H + h:H + h + 1, :]
        i_row = GT[h:h + 1, :]
        b_last = bcol_all[L - 1:L, H + h:H + h + 1]
        m_old = m_scr[h:h + 1, 0:1]

        a = b_col + m_old
        d = jnp.where(causal, b_col - b_row + i_row, NEG_INF)
        m_t = jnp.maximum(a, jnp.max(d, axis=1, keepdims=True))
        w_inter = jnp.exp(a - m_t)

        q = qk[:, h * dh:(h + 1) * dh]
        k = qk[:, W + h * dh:W + (h + 1) * dh] * (dh ** -0.5)
        qb = q.astype(BF16)
        kb = k.astype(BF16)
        vh = v_ref[:, h * dh:(h + 1) * dh]
        s = lax.dot_general(qb, kb, _NT, preferred_element_type=F32)
        w_intra = jnp.exp(d - m_t) * s
        c_old = c_scr[h]
        n_old = n_scr[h:h + 1, :]
        num = (w_inter * jnp.dot(qb, c_old.astype(BF16), preferred_element_type=F32)
               + jnp.dot(w_intra.astype(BF16), vh, preferred_element_type=F32))
        qn = jnp.sum(q * n_old, axis=1, keepdims=True)
        nq = w_inter * qn + jnp.sum(w_intra, axis=1, keepdims=True)
        hh = num / jnp.maximum(jnp.abs(nq), jnp.exp(-m_t))

        g_col = b_last - b_col + i_col
        g_row = b_last - b_row + i_row
        m_new = jnp.maximum(b_last + m_old, jnp.max(g_row, axis=1, keepdims=True))
        decay = jnp.exp(b_last + m_old - m_new)
        wk = jnp.exp(g_col - m_new) * k
        c_scr[h] = decay * c_old + jnp.dot(wk.T.astype(BF16), vh,
                                           preferred_element_type=F32)
        n_scr[h:h + 1, :] = decay * n_old + jnp.sum(wk, axis=0, keepdims=True)
        m_scr[h:h + 1, :] = jnp.broadcast_to(m_new, (1, m_scr.shape[1]))

        mu = jnp.mean(hh, axis=-1, keepdims=True)
        hc = hh - mu
        var = jnp.mean(hc * hc, axis=-1, keepdims=True)
        hn = hc * lax.rsqrt(var + LN_EPS)
        gate = jax.nn.sigmoid(og_ref[:, h * dh:(h + 1) * dh])
        out_ref[:, h * dh:(h + 1) * dh] = (
            hn * g_ref[:, h * dh:(h + 1) * dh] * gate).astype(out_ref.dtype)


def _mlstm(hf3, hb3, conv_w, conv_b, gbias, ml_g, L):
    B, S, _ = hf3.shape
    W = ml_g.shape[-1]
    H = ML_HEADS
    dh = W // H
    kern = functools.partial(_ml_kernel, L=L, H=H, dh=dh)
    return pl.pallas_call(
        kern,
        grid=(B, S // L),
        in_specs=[pl.BlockSpec((None, L, 2 * W), lambda b, c: (b, c, 0)),
                  pl.BlockSpec((None, L, W), lambda b, c: (b, c, 2)),
                  pl.BlockSpec((None, L, LANES), lambda b, c: (b, c, 3 * W // LANES)),
                  pl.BlockSpec((None, L, W), lambda b, c: (b, c, 3)),
                  pl.BlockSpec((CONV_K, 2 * W), lambda b, c: (0, 0)),
                  pl.BlockSpec((1, 2 * W), lambda b, c: (0, 0)),
                  pl.BlockSpec((1, LANES), lambda b, c: (0, 0)),
                  pl.BlockSpec((1, W), lambda b, c: (0, 0))],
        out_specs=pl.BlockSpec((None, L, W), lambda b, c: (b, c, 0)),
        out_shape=jax.ShapeDtypeStruct((B, S, W), BF16),
        scratch_shapes=[pltpu.VMEM((L + SUBLANES, 2 * W), F32),
                        pltpu.VMEM((H, dh, dh), F32),
                        pltpu.VMEM((SUBLANES, dh), F32),
                        pltpu.VMEM((SUBLANES, LANES), F32)],
        compiler_params=_cparams(("parallel", "arbitrary")),
        name="mlstm",
    )(hf3, hf3, hf3, hb3, conv_w, conv_b, gbias, ml_g)


def _outproj_kernel(x_ref, da_ref, hm_ref, w_ref, g_ref, b_ref, o_ref, *, alpha):
    wd = da_ref.shape[-1]
    mix = (jnp.dot(da_ref[...], w_ref[0:wd, :], preferred_element_type=F32)
           + jnp.dot(hm_ref[...], w_ref[wd:, :], preferred_element_type=F32))
    o_ref[...] = _layer_norm(alpha * x_ref[...] + mix, g_ref[...], b_ref[...])


def _outproj_ln(x2d, da2d, hm2d, w_out, g, b, alpha, tm):
    T, D = x2d.shape
    wd, wm = da2d.shape[1], hm2d.shape[1]
    kern = functools.partial(_outproj_kernel, alpha=alpha)
    return pl.pallas_call(
        kern,
        grid=(T // tm,),
        in_specs=[pl.BlockSpec((tm, D), lambda i: (i, 0)),
                  pl.BlockSpec((tm, wd), lambda i: (i, 0)),
                  pl.BlockSpec((tm, wm), lambda i: (i, 0)),
                  pl.BlockSpec((wd + wm, D), lambda i: (0, 0)),
                  pl.BlockSpec((1, D), lambda i: (0, 0)),
                  pl.BlockSpec((1, D), lambda i: (0, 0))],
        out_specs=pl.BlockSpec((tm, D), lambda i: (i, 0)),
        out_shape=jax.ShapeDtypeStruct((T, D), F32),
        compiler_params=_cparams(("parallel",)),
        name="outproj_ln",
    )(x2d, da2d, hm2d, w_out, g, b)


def _memattn_kernel(x_ref, kv_ref, wq_ref, wo_ref, g_ref, b_ref, o_ref, *, alpha, H):
    x = x_ref[...]
    D = x.shape[-1]
    dh = D // H
    q = jnp.dot(x.astype(BF16), wq_ref[...], preferred_element_type=F32) * (dh ** -0.5)
    qb = q.astype(BF16)
    outs = []
    for h in range(H):
        kh = kv_ref[:, h * dh:(h + 1) * dh]
        vh = kv_ref[:, D + h * dh:D + (h + 1) * dh]
        s = lax.dot_general(qb[:, h * dh:(h + 1) * dh], kh, _NT, preferred_element_type=F32)
        s = s - jnp.max(s, axis=-1, keepdims=True)
        e = jnp.exp(s)
        p = e / jnp.sum(e, axis=-1, keepdims=True)
        outs.append(jnp.dot(p.astype(BF16), vh, preferred_element_type=F32))
    o = jnp.concatenate(outs, axis=-1).astype(BF16)
    att = jnp.dot(o, wo_ref[...], preferred_element_type=F32)
    o_ref[...] = _layer_norm(alpha * x + att, g_ref[...], b_ref[...])


def _memattn_ln(x3, kv3, wq, wo, g, b, alpha, tm):
    B, S, D = x3.shape
    M = kv3.shape[1]
    kern = functools.partial(_memattn_kernel, alpha=alpha, H=MEM_HEADS)
    return pl.pallas_call(
        kern,
        grid=(B, S // tm),
        in_specs=[pl.BlockSpec((None, tm, D), lambda bb, i: (bb, i, 0)),
                  pl.BlockSpec((None, M, 2 * D), lambda bb, i: (bb, 0, 0)),
                  pl.BlockSpec((D, D), lambda bb, i: (0, 0)),
                  pl.BlockSpec((D, D), lambda bb, i: (0, 0)),
                  pl.BlockSpec((1, D), lambda bb, i: (0, 0)),
                  pl.BlockSpec((1, D), lambda bb, i: (0, 0))],
        out_specs=pl.BlockSpec((None, tm, D), lambda bb, i: (bb, i, 0)),
        out_shape=jax.ShapeDtypeStruct((B, S, D), F32),
        compiler_params=_cparams(("parallel", "parallel")),
        name="memattn_ln",
    )(x3, kv3, wq, wo, g, b)


def _sort_network(n):
    def merge(lo, hi, r):
        step = r * 2
        if step < hi - lo:
            yield from merge(lo, hi, step)
            yield from merge(lo + r, hi, step)
            yield from [(i, i + r) for i in range(lo + r, hi - r, step)]
        else:
            yield (lo, lo + r)

    def sort(lo, hi):
        if hi - lo >= 1:
            mid = lo + (hi - lo) // 2
            yield from sort(lo, mid)
            yield from sort(mid + 1, hi)
            yield from merge(lo, hi, 1)

    return list(sort(0, n - 1))


def _top_desc_sorted(s, count):
    rows = s.shape[0]
    n = rows // SUBLANES
    lists = [s[g * SUBLANES:(g + 1) * SUBLANES, :] for g in range(n)]
    for i, j in _sort_network(n):
        hi = jnp.maximum(lists[i], lists[j])
        lo = jnp.minimum(lists[i], lists[j])
        lists[i], lists[j] = hi, lo
    tops = []
    for r in range(count):
        mx = jnp.max(lists[0], axis=0, keepdims=True)
        tops.append(mx)
        if r + 1 < count:
            hit = lists[0] == mx
            depth = min(n, count - 1 - r)
            for l in range(depth):
                nxt = lists[l + 1] if l + 1 < n else NEG_INF
                lists[l] = jnp.where(hit, nxt, lists[l])
    return tops


def _extract_desc(v, count):
    tops = []
    for r in range(count):
        mx = jnp.max(v, axis=0, keepdims=True)
        tops.append(mx)
        if r + 1 < count:
            v = jnp.where(v == mx, NEG_INF, v)
    return tops


def _count_greater(x, t):
    assert len(t) == 16
    m8 = t[7] > x
    m4 = jnp.where(m8, t[11], t[3]) > x
    m2 = jnp.where(m8, jnp.where(m4, t[13], t[9]), jnp.where(m4, t[5], t[1])) > x
    th = jnp.where(m8,
                   jnp.where(m4, jnp.where(m2, t[14], t[12]), jnp.where(m2, t[10], t[8])),
                   jnp.where(m4, jnp.where(m2, t[6], t[4]), jnp.where(m2, t[2], t[0])))
    m1 = th > x
    cnt = (jnp.where(m8, 8.0, 0.0) + jnp.where(m4, 4.0, 0.0)
           + jnp.where(m2, 2.0, 0.0) + jnp.where(m1, 1.0, 0.0))
    return cnt + jnp.where(t[15] > x, 1.0, 0.0)


def _dup_bf16_words(x):
    u = lax.bitcast_convert_type(x.astype(BF16).astype(F32), jnp.uint32)
    return u | (u >> 16)


def _route_kernel(x_ref, wpq_ref, sk_ref, c1_ref, e1_ref, r2_ref, e2_ref, q_scr, *, H, nk, topk):
    q_scr[...] = jnp.dot(x_ref[...].astype(BF16), wpq_ref[...], preferred_element_type=F32)
    T = x_ref.shape[0]
    half = sk_ref.shape[-1]
    kk = topk + 1
    sub = lax.broadcasted_iota(jnp.int32, (SUBLANES, T), 0)

    def head(h, carry):
        base = pl.multiple_of(h * 2 * half, 2 * half)
        qa = q_scr[:, pl.ds(base, half)].astype(BF16)
        qb = q_scr[:, pl.ds(base + half, half)].astype(BF16)
        s1 = lax.dot_general(sk_ref[0], qa, _NT, preferred_element_type=F32)
        s2 = lax.dot_general(sk_ref[1], qb, _NT, preferred_element_type=F32)
        a = _top_desc_sorted(s1, kk)
        b = _top_desc_sorted(s2, kk)
        pad = [jnp.full((1, T), NEG_INF, F32)] * ((-kk) % SUBLANES)
        a_arr = jnp.concatenate(a + pad, axis=0)
        b_arr = jnp.concatenate(b + pad, axis=0)
        slabs = []
        for p in range(2):
            for s0 in range(0, kk // (p + 1), SUBLANES):
                slabs.append(a[p] + b_arr[s0:s0 + SUBLANES, :])
        for q in range(2):
            for s0 in range(0, kk // (q + 1), SUBLANES):
                blk = b[q] + a_arr[s0:s0 + SUBLANES, :]
                slabs.append(jnp.where(sub >= 2, blk, NEG_INF) if s0 == 0 else blk)
        rest = [a[p] + b[q] for p in range(2, kk) for q in range(2, kk) if (p + 1) * (q + 1) <= kk]
        rest = rest + [jnp.full((1, T), NEG_INF, F32)] * ((-len(rest)) % SUBLANES)
        slabs.append(jnp.concatenate(rest, axis=0))
        cs = _extract_desc(jnp.concatenate(slabs, axis=0), kk)
        tau = 0.5 * (cs[topk - 1] + cs[topk])
        z = jnp.zeros((1, T), F32)
        for r in range(topk):
            z = z + jnp.exp(cs[r] - cs[0])
        c1_ref[h] = _dup_bf16_words(_count_greater(tau - s1, b[:topk]))
        e1_ref[h] = _dup_bf16_words(jnp.exp(s1 - a[0]))
        r2_ref[h] = _count_greater(s2, b[:topk]).astype(BF16)
        e2_ref[h] = (jnp.exp(s2 - b[0]) / z).astype(BF16)
        return carry

    lax.fori_loop(0, H, head, 0)


def _peer_route(x2d, wpq, sk, tr):
    T, D = x2d.shape
    H = PEER_HEADS
    nk = sk.shape[1]
    kern = functools.partial(_route_kernel, H=H, nk=nk, topk=PEER_TOPK)
    shp_w = jax.ShapeDtypeStruct((H, nk, T), jnp.uint32)
    shp_b = jax.ShapeDtypeStruct((H, nk, T), BF16)
    ospec = pl.BlockSpec((H, nk, tr), lambda i: (0, 0, i))
    return pl.pallas_call(
        kern,
        grid=(T // tr,),
        in_specs=[pl.BlockSpec((tr, D), lambda i: (i, 0)),
                  pl.BlockSpec(wpq.shape, lambda i: (0, 0)),
                  pl.BlockSpec(sk.shape, lambda i: (0, 0, 0))],
        out_specs=[ospec, ospec, ospec, ospec],
        out_shape=[shp_w, shp_w, shp_b, shp_b],
        scratch_shapes=[pltpu.VMEM((tr, wpq.shape[1]), F32)],
        compiler_params=_cparams(("parallel",)),
        name="peer_route",
    )(x2d, wpq, sk)


def _peer_kernel(x_ref, c1_ref, e1_ref, r2_ref, e2_ref, u_ref, vt_ref, g_ref, b_ref, o_ref,
                 xb_scr, act_scr, w_scr, acc_scr, bc1_scr, be1_scr, *, alpha, H, nk, ib, sub):
    step = pl.program_id(1)
    tt = x_ref.shape[0]
    stages = ib // sub
    se = sub * nk
    rc = 2 * SUBLANES
    grp = 4

    @pl.when(step == 0)
    def _():
        xb_scr[...] = x_ref[...].astype(BF16)
        acc_scr[...] = jnp.zeros(acc_scr.shape, F32)

    def act_mm(k):
        act_scr[k] = lax.dot_general(u_ref[k * se:(k + 1) * se, :], xb_scr[...], _NT,
                                     preferred_element_type=F32)

    def out_mm(k):
        acc_scr[...] += jnp.dot(vt_ref[:, k * se:(k + 1) * se], w_scr[k],
                                preferred_element_type=F32)

    def gates(k):
        for half in range(sub):
            ii = k * sub + half
            par = ii % 2
            for h in range(H):
                bc1_scr[par, h] = pltpu.bitcast(
                    jnp.broadcast_to(c1_ref[h, ii:ii + 1, :], (SUBLANES, tt)), BF16)
                be1_scr[par, h] = pltpu.bitcast(
                    jnp.broadcast_to(e1_ref[h, ii:ii + 1, :], (SUBLANES, tt)), BF16)
            for g in range(nk // (grp * rc)):
                accs = [jnp.zeros((rc, tt), BF16) for _ in range(grp)]
                for h in range(H):
                    cb = bc1_scr[par, h]
                    eb = be1_scr[par, h]
                    for c in range(grp):
                        j0 = (g * grp + c) * rc
                        sel = jnp.where(r2_ref[h, j0:j0 + rc, :] < cb,
                                        e2_ref[h, j0:j0 + rc, :], jnp.zeros((), BF16))
                        accs[c] = accs[c] + sel * eb
                for c in range(grp):
                    e0 = half * nk + (g * grp + c) * rc
                    a = act_scr[k, e0:e0 + rc, :]
                    gel = a * (1.0 + lax.erf(a))
                    w_scr[k, e0:e0 + rc, :] = accs[c] * gel.astype(BF16)

    act_mm(0)
    for k in range(stages):
        if k + 1 < stages:
            act_mm(k + 1)
        gates(k)
        if k >= 1:
            out_mm(k - 1)
    out_mm(stages - 1)

    @pl.when(step == pl.num_programs(1) - 1)
    def _():
        y = alpha * x_ref[...] + acc_scr[...].T
        o_ref[...] = _layer_norm(y, g_ref[...], b_ref[...])


def _peer_experts_ln(x2d, c1, e1, r2, e2, u_bf, vt_bf, g, b, alpha, tt, ib):
    T, D = x2d.shape
    H, nk, _ = c1.shape
    ne = ib * nk
    steps = nk // ib
    sub = 4
    kern = functools.partial(_peer_kernel, alpha=alpha, H=H, nk=nk, ib=ib, sub=sub)
    return pl.pallas_call(
        kern,
        grid=(T // tt, steps),
        in_specs=[pl.BlockSpec((tt, D), lambda t, s: (t, 0)),
                  pl.BlockSpec((H, ib, tt), lambda t, s: (0, s, t)),
                  pl.BlockSpec((H, ib, tt), lambda t, s: (0, s, t)),
                  pl.BlockSpec((H, nk, tt), lambda t, s: (0, 0, t)),
                  pl.BlockSpec((H, nk, tt), lambda t, s: (0, 0, t)),
                  pl.BlockSpec((ne, D), lambda t, s: (s, 0)),
                  pl.BlockSpec((D, ne), lambda t, s: (0, s)),
                  pl.BlockSpec((1, D), lambda t, s: (0, 0)),
                  pl.BlockSpec((1, D), lambda t, s: (0, 0))],
        out_specs=pl.BlockSpec((tt, D), lambda t, s: (t, 0)),
        out_shape=jax.ShapeDtypeStruct((T, D), F32),
        scratch_shapes=[pltpu.VMEM((tt, D), BF16),
                        pltpu.VMEM((ib // sub, sub * nk, tt), F32),
                        pltpu.VMEM((ib // sub, sub * nk, tt), BF16),
                        pltpu.VMEM((D, tt), F32),
                        pltpu.VMEM((2, H, 2 * SUBLANES, tt), BF16),
                        pltpu.VMEM((2, H, 2 * SUBLANES, tt), BF16)],
        compiler_params=_cparams(("parallel", "arbitrary")),
        name="peer_experts_ln",
    )(x2d, c1, e1, r2, e2, u_bf, vt_bf, g, b)


def _tile(n, pref):
    t = min(n, pref)
    assert n % t == 0, (n, pref)
    return t


def kernel(x, mem, w_in, i_bias, f_bias, conv_w, conv_b, lam_qk, da_norm_g, ml_norm_g, w_out,
           ln1_g, ln1_b, wq_mem, wkv_mem, wo_mem, ln2_g, ln2_b, w_pq, sub_keys, u_tab, v_tab,
           ln3_g, ln3_b):
    B, S, D = x.shape
    depth = w_in.shape[0]
    T = B * S
    M = mem.shape[1]
    alpha = (2.0 * depth) ** 0.25
    wda = DA_HEADS * da_norm_g.shape[-1]
    wml = ml_norm_g.shape[-1]
    assert w_in.shape[-1] == 3 * wda + 4 * wml + 2 * ML_HEADS
    assert wda == wml and wml % LANES == 0 and 2 * ML_HEADS <= LANES

    tm = _tile(T, 256)
    tq = _tile(S, 1024)
    L = _tile(S, 128)
    tmem = _tile(S, 256)
    tr = _tile(T, 256)
    tt = _tile(T, 512)
    ib = 2 * SUBLANES

    mem2d = mem.reshape(B * M, D)
    o_q, o_k, o_v = 0, wda, 2 * wda
    o_mq = 3 * wda
    o_mk, o_mv, o_mo, o_gt = o_mq + wml, o_mq + 2 * wml, o_mq + 3 * wml, o_mq + 4 * wml

    for l in range(depth):
        lam_init = 0.8 - 0.6 * math.exp(-0.3 * l)
        wl = w_in[l]
        wb = jnp.concatenate([wl[:, o_q:o_mq], wl[:, o_mv:o_mo]], axis=1).astype(BF16)
        wf = jnp.concatenate(
            [wl[:, o_mq:o_mv], wl[:, o_mo:o_gt],
             jnp.pad(wl[:, o_gt:], ((0, 0), (0, LANES - 2 * ML_HEADS)))], axis=1).astype(BF16)
        gbias = jnp.pad(jnp.concatenate([i_bias[l], f_bias[l]]),
                        (0, LANES - 2 * ML_HEADS)).reshape(1, LANES)
        lami = jnp.full((1, da_norm_g.shape[-1]), lam_init, F32)

        x2d = x.reshape(T, D)
        hb, hf = _inproj(x2d, wb, wf, tm)
        hb3 = hb.reshape(B, S, -1)
        hf3 = hf.reshape(B, S, -1)
        da = _diff_attention(hb3, lam_qk[l], da_norm_g[l].reshape(1, -1), lami, tq)
        hm = _mlstm(hf3, hb3, conv_w[l], conv_b[l].reshape(1, -1), gbias,
                    ml_norm_g[l].reshape(1, -1), L)
        x1 = _outproj_ln(x2d, da.reshape(T, -1), hm.reshape(T, -1), w_out[l].astype(BF16),
                         ln1_g[l].reshape(1, D), ln1_b[l].reshape(1, D), alpha, tm)

        kv = _matmul(mem2d, wkv_mem[l].astype(BF16), _tile(B * M, 256), BF16)
        x2 = _memattn_ln(x1.reshape(B, S, D), kv.reshape(B, M, 2 * D), wq_mem[l].astype(BF16),
                         wo_mem[l].astype(BF16), ln2_g[l].reshape(1, D), ln2_b[l].reshape(1, D),
                         alpha, tmem)
        x2d = x2.reshape(T, D)

        c1, e1, r2, e2 = _peer_route(x2d, w_pq[l].astype(BF16), sub_keys[l].astype(BF16), tr)
        x3 = _peer_experts_ln(x2d, c1, e1, r2, e2, (u_tab[l] * RSQRT2).astype(BF16),
                              (v_tab[l].T * RSQRT2).astype(BF16), ln3_g[l].reshape(1, D),
                              ln3_b[l].reshape(1, D), alpha, tt, ib)
        x = x3.reshape(B, S, D)
    return x
```

```python
import functools
import math

import jax
import jax.numpy as jnp
from jax import lax
from jax.experimental import pallas as pl
from jax.experimental.pallas import tpu as pltpu

F32 = jnp.float32
BF16 = jnp.bfloat16
LN_EPS = 1e-5
NEG_INF = float("-inf")
RSQRT2 = 2.0 ** -0.5

DA_HEADS = 4
ML_HEADS = 4
MEM_HEADS = 4
PEER_HEADS = 8
PEER_TOPK = 16
CONV_K = 4

VMEM_LIMIT_BYTES = 56 * 1024 * 1024
LANES = 128
SUBLANES = 8

_NT = (((1,), (1,)), ((), ()))


def _cparams(sem):
    return pltpu.CompilerParams(dimension_semantics=sem, vmem_limit_bytes=VMEM_LIMIT_BYTES)


def _layer_norm(y, g, b):
    mu = jnp.mean(y, axis=-1, keepdims=True)
    yc = y - mu
    var = jnp.mean(yc * yc, axis=-1, keepdims=True)
    return yc * lax.rsqrt(var + LN_EPS) * g + b


def _inproj_kernel(x_ref, wb_ref, wf_ref, hb_ref, hf_ref):
    xb = x_ref[...].astype(BF16)
    hb_ref[...] = jnp.dot(xb, wb_ref[...], preferred_element_type=F32).astype(BF16)
    hf_ref[...] = jnp.dot(xb, wf_ref[...], preferred_element_type=F32)


def _inproj(x2d, wb, wf, tm):
    T, D = x2d.shape
    nb, nf = wb.shape[1], wf.shape[1]
    return pl.pallas_call(
        _inproj_kernel,
        grid=(T // tm,),
        in_specs=[pl.BlockSpec((tm, D), lambda i: (i, 0)),
                  pl.BlockSpec((D, nb), lambda i: (0, 0)),
                  pl.BlockSpec((D, nf), lambda i: (0, 0))],
        out_specs=[pl.BlockSpec((tm, nb), lambda i: (i, 0)),
                   pl.BlockSpec((tm, nf), lambda i: (i, 0))],
        out_shape=[jax.ShapeDtypeStruct((T, nb), BF16),
                   jax.ShapeDtypeStruct((T, nf), F32)],
        compiler_params=_cparams(("parallel",)),
        name="inproj",
    )(x2d, wb, wf)


def _mm_kernel(x_ref, w_ref, o_ref):
    o_ref[...] = jnp.dot(x_ref[...].astype(BF16), w_ref[...],
                         preferred_element_type=F32).astype(o_ref.dtype)


def _matmul(x2d, w, tm, out_dtype):
    M, K = x2d.shape
    N = w.shape[1]
    return pl.pallas_call(
        _mm_kernel,
        grid=(M // tm,),
        in_specs=[pl.BlockSpec((tm, K), lambda i: (i, 0)),
                  pl.BlockSpec((K, N), lambda i: (0, 0))],
        out_specs=pl.BlockSpec((tm, N), lambda i: (i, 0)),
        out_shape=jax.ShapeDtypeStruct((M, N), out_dtype),
        compiler_params=_cparams(("parallel",)),
        name="matmul",
    )(x2d, w)


def _da_kernel(q_ref, k_ref, v_ref, lq_ref, g_ref, lami_ref, o_ref,
               m1, a1, m2, a2, *, tq, ts, dk):
    qb = pl.program_id(2)
    dv = q_ref.shape[-1]
    rep = tq // LANES

    lane = lax.broadcasted_iota(jnp.int32, (1, dv), 1)
    qf = q_ref[...].astype(F32) * (dk ** -0.5 * math.log2(math.e))
    q1 = jnp.where(lane < dk, qf, 0.0).astype(BF16)
    q2 = jnp.where(lane >= dk, qf, 0.0).astype(BF16)
    ones = jnp.ones((tq, dv), BF16)

    m1[...] = jnp.full(m1.shape, NEG_INF, F32)
    m2[...] = jnp.full(m2.shape, NEG_INF, F32)
    a1[...] = jnp.zeros(a1.shape, F32)
    a2[...] = jnp.zeros(a2.shape, F32)

    def update(s, vones, m_ref, a_ref, r0):
        m_prev = m_ref[r0:r0 + ts, :]
        m_cur = jnp.max(s, axis=1, keepdims=True)
        m_next = jnp.maximum(m_prev, m_cur)
        p = jnp.exp2(s - jnp.concatenate([m_next] * rep, axis=1))
        alpha = jnp.exp2(m_prev - m_next)
        a_ref[r0:r0 + ts, :] = (jnp.concatenate([alpha, alpha], axis=1) * a_ref[r0:r0 + ts, :]
                                + jnp.dot(p.astype(BF16), vones, preferred_element_type=F32))
        m_ref[r0:r0 + ts, :] = m_next

    def block(kb, masked):
        off = pl.multiple_of(kb * tq, tq)
        kblk = k_ref[pl.ds(off, tq), :]
        vblk = jnp.concatenate([v_ref[pl.ds(off, tq), :], ones], axis=1)
        for st in range(tq // ts):
            r0 = st * ts
            s1 = lax.dot_general(q1[r0:r0 + ts], kblk, _NT, preferred_element_type=F32)
            s2 = lax.dot_general(q2[r0:r0 + ts], kblk, _NT, preferred_element_type=F32)
            if masked:
                row = lax.broadcasted_iota(jnp.int32, (ts, tq), 0) + r0
                col = lax.broadcasted_iota(jnp.int32, (ts, tq), 1)
                keep = col <= row
                s1 = jnp.where(keep, s1, NEG_INF)
                s2 = jnp.where(keep, s2, NEG_INF)
            update(s1, vblk, m1, a1, r0)
            update(s2, vblk, m2, a2, r0)

    def body(kb, carry):
        block(kb, False)
        return carry

    lax.fori_loop(0, qb, body, 0)
    block(qb, True)

    lq = lq_ref[...]
    lam_init = lami_ref[...][:, :1]
    lam = (jnp.exp(jnp.sum(lq[0:1] * lq[1:2], axis=1, keepdims=True))
           - jnp.exp(jnp.sum(lq[2:3] * lq[3:4], axis=1, keepdims=True)) + lam_init)
    o = a1[:, :dv] / a1[:, dv:] - lam * (a2[:, :dv] / a2[:, dv:])
    o = o * lax.rsqrt(jnp.mean(o * o, axis=-1, keepdims=True) + LN_EPS)
    o = o * g_ref[...] * (1.0 - lami_ref[...])
    o_ref[...] = o.astype(o_ref.dtype)


def _diff_attention(hb3, lam_qk, da_g, lami, tq):
    B, S, _ = hb3.shape
    dv = da_g.shape[-1]
    dk = dv // 2
    H = DA_HEADS
    assert dv == LANES
    kern = functools.partial(_da_kernel, tq=tq, ts=tq, dk=dk)
    return pl.pallas_call(
        kern,
        grid=(B, H, S // tq),
        in_specs=[pl.BlockSpec((None, tq, dv), lambda b, h, i: (b, i, h)),
                  pl.BlockSpec((None, S, dv), lambda b, h, i: (b, 0, H + h)),
                  pl.BlockSpec((None, S, dv), lambda b, h, i: (b, 0, 2 * H + h)),
                  pl.BlockSpec(lam_qk.shape, lambda b, h, i: (0, 0)),
                  pl.BlockSpec((1, dv), lambda b, h, i: (0, 0)),
                  pl.BlockSpec((1, dv), lambda b, h, i: (0, 0))],
        out_specs=pl.BlockSpec((None, tq, dv), lambda b, h, i: (b, i, h)),
        out_shape=jax.ShapeDtypeStruct((B, S, H * dv), BF16),
        scratch_shapes=[pltpu.VMEM((tq, LANES), F32), pltpu.VMEM((tq, 2 * dv), F32),
                        pltpu.VMEM((tq, LANES), F32), pltpu.VMEM((tq, 2 * dv), F32)],
        compiler_params=_cparams(("parallel", "parallel", "arbitrary")),
        name="diff_attention",
    )(hb3, hb3, hb3, lam_qk, da_g, lami)


def _log_sigmoid(x):
    return -(jnp.maximum(-x, 0.0) + jnp.log1p(jnp.exp(-jnp.abs(x))))


def _ml_kernel(qk_ref, og_ref, gt_ref, v_ref, cw_ref, cb_ref, gb_ref, g_ref, out_ref,
               buf, c_scr, n_scr, m_scr, *, L, H, dh):
    c = pl.program_id(1)
    W = H * dh

    @pl.when(c == 0)
    def _():
        buf[0:SUBLANES, :] = jnp.zeros((SUBLANES, buf.shape[1]), F32)
        c_scr[...] = jnp.zeros(c_scr.shape, F32)
        n_scr[...] = jnp.zeros(n_scr.shape, F32)
        m_scr[...] = jnp.zeros(m_scr.shape, F32)

    buf[SUBLANES:SUBLANES + L, :] = qk_ref[...]
    cw = cw_ref[...]
    y = cb_ref[...]
    for j in range(CONV_K):
        s0 = SUBLANES - (CONV_K - 1) + j
        y = y + cw[j:j + 1, :] * buf[s0:s0 + L, :]
    buf[0:SUBLANES, :] = buf[L:L + SUBLANES, :]
    qk = y * jax.nn.sigmoid(y)

    G = gt_ref[...] + gb_ref[...]
    ls = _log_sigmoid(G)
    row = lax.broadcasted_iota(jnp.int32, (L, L), 0)
    col = lax.broadcasted_iota(jnp.int32, (L, L), 1)
    causal = col <= row
    tri = causal.astype(F32)
    tri_t = (row <= col).astype(F32)
    bcol_all = jnp.dot(tri, ls, precision=lax.Precision.HIGHEST,
                       preferred_element_type=F32)
    GT = G.T
    brow_all = jnp.dot(ls.T, tri_t, precision=lax.Precision.HIGHEST,
                       preferred_element_type=F32)

    for h in range(H):
        b_col = bcol_all[:, H + h:H + h + 1]
        i_col = G[:, h:h + 1]
        b_row = brow_all[H + h:H + h + 1, :]
        i_row = GT[h:h + 1, :]
        b_last = bcol_all[L - 1:L, H + h:H + h + 1]
        m_old = m_scr[h:h + 1, 0:1]

        a = b_col + m_old
        d = jnp.where(causal, b_col - b_row + i_row, NEG_INF)
        m_t = jnp.maximum(a, jnp.max(d, axis=1, keepdims=True))
        w_inter = jnp.exp(a - m_t)

        q = qk[:, h * dh:(h + 1) * dh]
        k = qk[:, W + h * dh:W + (h + 1) * dh] * (dh ** -0.5)
        qb = q.astype(BF16)
        kb = k.astype(BF16)
        vh = v_ref[:, h * dh:(h + 1) * dh]
        s = lax.dot_general(qb, kb, _NT, preferred_element_type=F32)
        w_intra = jnp.exp(d - m_t) * s
        c_old = c_scr[h]
        n_old = n_scr[h:h + 1, :]
        num = (w_inter * jnp.dot(qb, c_old.astype(BF16), preferred_element_type=F32)
               + jnp.dot(w_intra.astype(BF16), vh, preferred_element_type=F32))
        qn = jnp.sum(q * n_old, axis=1, keepdims=True)
        nq = w_inter * qn + jnp.sum(w_intra, axis=1, keepdims=True)
        hh = num / jnp.maximum(jnp.abs(nq), jnp.exp(-m_t))

        g_col = b_last - b_col + i_col
        g_row = b_last - b_row + i_row
        m_new = jnp.maximum(b_last + m_old, jnp.max(g_row, axis=1, keepdims=True))
        decay = jnp.exp(b_last + m_old - m_new)
        wk = jnp.exp(g_col - m_new) * k
        c_scr[h] = decay * c_old + jnp.dot(wk.T.astype(BF16), vh,
                                           preferred_element_type=F32)
        n_scr[h:h + 1, :] = decay * n_old + jnp.sum(wk, axis=0, keepdims=True)
        m_scr[h:h + 1, :] = jnp.broadcast_to(m_new, (1, m_scr.shape[1]))

        mu = jnp.mean(hh, axis=-1, keepdims=True)
        hc = hh - mu
        var = jnp.mean(hc * hc, axis=-1, keepdims=True)
        hn = hc * lax.rsqrt(var + LN_EPS)
        gate = jax.nn.sigmoid(og_ref[:, h * dh:(h + 1) * dh])
        out_ref[:, h * dh:(h + 1) * dh] = (
            hn * g_ref[:, h * dh:(h + 1) * dh] * gate).astype(out_ref.dtype)


def _mlstm(hf3, hb3, conv_w, conv_b, gbias, ml_g, L):
    B, S, _ = hf3.shape
    W = ml_g.shape[-1]
    H = ML_HEADS
    dh = W // H
    kern = functools.partial(_ml_kernel, L=L, H=H, dh=dh)
    return pl.pallas_call(
        kern,
        grid=(B, S // L),
        in_specs=[pl.BlockSpec((None, L, 2 * W), lambda b, c: (b, c, 0)),
                  pl.BlockSpec((None, L, W), lambda b, c: (b, c, 2)),
                  pl.BlockSpec((None, L, LANES), lambda b, c: (b, c, 3 * W // LANES)),
                  pl.BlockSpec((None, L, W), lambda b, c: (b, c, 3)),
                  pl.BlockSpec((CONV_K, 2 * W), lambda b, c: (0, 0)),
                  pl.BlockSpec((1, 2 * W), lambda b, c: (0, 0)),
                  pl.BlockSpec((1, LANES), lambda b, c: (0, 0)),
                  pl.BlockSpec((1, W), lambda b, c: (0, 0))],
        out_specs=pl.BlockSpec((None, L, W), lambda b, c: (b, c, 0)),
        out_shape=jax.ShapeDtypeStruct((B, S, W), BF16),
        scratch_shapes=[pltpu.VMEM((L + SUBLANES, 2 * W), F32),
                        pltpu.VMEM((H, dh, dh), F32),
                        pltpu.VMEM((SUBLANES, dh), F32),
                        pltpu.VMEM((SUBLANES, LANES), F32)],
        compiler_params=_cparams(("parallel", "arbitrary")),
        name="mlstm",
    )(hf3, hf3, hf3, hb3, conv_w, conv_b, gbias, ml_g)


def _outproj_kernel(x_ref, da_ref, hm_ref, w_ref, g_ref, b_ref, o_ref, *, alpha):
    wd = da_ref.shape[-1]
    mix = (jnp.dot(da_ref[...], w_ref[0:wd, :], preferred_element_type=F32)
           + jnp.dot(hm_ref[...], w_ref[wd:, :], preferred_element_type=F32))
    o_ref[...] = _layer_norm(alpha * x_ref[...] + mix, g_ref[...], b_ref[...])


def _outproj_ln(x2d, da2d, hm2d, w_out, g, b, alpha, tm):
    T, D = x2d.shape
    wd, wm = da2d.shape[1], hm2d.shape[1]
    kern = functools.partial(_outproj_kernel, alpha=alpha)
    return pl.pallas_call(
        kern,
        grid=(T // tm,),
        in_specs=[pl.BlockSpec((tm, D), lambda i: (i, 0)),
                  pl.BlockSpec((tm, wd), lambda i: (i, 0)),
                  pl.BlockSpec((tm, wm), lambda i: (i, 0)),
                  pl.BlockSpec((wd + wm, D), lambda i: (0, 0)),
                  pl.BlockSpec((1, D), lambda i: (0, 0)),
                  pl.BlockSpec((1, D), lambda i: (0, 0))],
        out_specs=pl.BlockSpec((tm, D), lambda i: (i, 0)),
        out_shape=jax.ShapeDtypeStruct((T, D), F32),
        compiler_params=_cparams(("parallel",)),
        name="outproj_ln",
    )(x2d, da2d, hm2d, w_out, g, b)


def _memattn_kernel(x_ref, kv_ref, wq_ref, wo_ref, g_ref, b_ref, o_ref, *, alpha, H):
    x = x_ref[...]
    D = x.shape[-1]
    dh = D // H
    q = jnp.dot(x.astype(BF16), wq_ref[...], preferred_element_type=F32) * (dh ** -0.5)
    qb = q.astype(BF16)
    outs = []
    for h in range(H):
        kh = kv_ref[:, h * dh:(h + 1) * dh]
        vh = kv_ref[:, D + h * dh:D + (h + 1) * dh]
        s = lax.dot_general(qb[:, h * dh:(h + 1) * dh], kh, _NT, preferred_element_type=F32)
        s = s - jnp.max(s, axis=-1, keepdims=True)
        e = jnp.exp(s)
        p = e / jnp.sum(e, axis=-1, keepdims=True)
        outs.append(jnp.dot(p.astype(BF16), vh, preferred_element_type=F32))
    o = jnp.concatenate(outs, axis=-1).astype(BF16)
    att = jnp.dot(o, wo_ref[...], preferred_element_type=F32)
    o_ref[...] = _layer_norm(alpha * x + att, g_ref[...], b_ref[...])


def _memattn_ln(x3, kv3, wq, wo, g, b, alpha, tm):
    B, S, D = x3.shape
    M = kv3.shape[1]
    kern = functools.partial(_memattn_kernel, alpha=alpha, H=MEM_HEADS)
    return pl.pallas_call(
        kern,
        grid=(B, S // tm),
        in_specs=[pl.BlockSpec((None, tm, D), lambda bb, i: (bb, i, 0)),
                  pl.BlockSpec((None, M, 2 * D), lambda bb, i: (bb, 0, 0)),
                  pl.BlockSpec((D, D), lambda bb, i: (0, 0)),
                  pl.BlockSpec((D, D), lambda bb, i: (0, 0)),
                  pl.BlockSpec((1, D), lambda bb, i: (0, 0)),
                  pl.BlockSpec((1, D), lambda bb, i: (0, 0))],
        out_specs=pl.BlockSpec((None, tm, D), lambda bb, i: (bb, i, 0)),
        out_shape=jax.ShapeDtypeStruct((B, S, D), F32),
        compiler_params=_cparams(("parallel", "parallel")),
        name="memattn_ln",
    )(x3, kv3, wq, wo, g, b)


def _sort_network(n):
    def merge(lo, hi, r):
        step = r * 2
        if step < hi - lo:
            yield from merge(lo, hi, step)
            yield from merge(lo + r, hi, step)
            yield from [(i, i + r) for i in range(lo + r, hi - r, step)]
        else:
            yield (lo, lo + r)

    def sort(lo, hi):
        if hi - lo >= 1:
            mid = lo + (hi - lo) // 2
            yield from sort(lo, mid)
            yield from sort(mid + 1, hi)
            yield from merge(lo, hi, 1)

    return list(sort(0, n - 1))


def _top_desc_sorted(s, count):
    rows = s.shape[0]
    n = rows // SUBLANES
    lists = [s[g * SUBLANES:(g + 1) * SUBLANES, :] for g in range(n)]
    for i, j in _sort_network(n):
        hi = jnp.maximum(lists[i], lists[j])
        lo = jnp.minimum(lists[i], lists[j])
        lists[i], lists[j] = hi, lo
    tops = []
    for r in range(count):
        mx = jnp.max(lists[0], axis=0, keepdims=True)
        tops.append(mx)
        if r + 1 < count:
            hit = lists[0] == mx
            depth = min(n, count - 1 - r)
            for l in range(depth):
                nxt = lists[l + 1] if l + 1 < n else NEG_INF
                lists[l] = jnp.where(hit, nxt, lists[l])
    return tops


def _extract_desc(v, count):
    tops = []
    for r in range(count):
        mx = jnp.max(v, axis=0, keepdims=True)
        tops.append(mx)
        if r + 1 < count:
            v = jnp.where(v == mx, NEG_INF, v)
    return tops


def _count_greater(x, t):
    assert len(t) == 16
    m8 = t[7] > x
    m4 = jnp.where(m8, t[11], t[3]) > x
    m2 = jnp.where(m8, jnp.where(m4, t[13], t[9]), jnp.where(m4, t[5], t[1])) > x
    th = jnp.where(m8,
                   jnp.where(m4, jnp.where(m2, t[14], t[12]), jnp.where(m2, t[10], t[8])),
                   jnp.where(m4, jnp.where(m2, t[6], t[4]), jnp.where(m2, t[2], t[0])))
    m1 = th > x
    cnt = (jnp.where(m8, 8.0, 0.0) + jnp.where(m4, 4.0, 0.0)
           + jnp.where(m2, 2.0, 0.0) + jnp.where(m1, 1.0, 0.0))
    return cnt + jnp.where(t[15] > x, 1.0, 0.0)


def _dup_bf16_words(x):
    u = lax.bitcast_convert_type(x.astype(BF16).astype(F32), jnp.uint32)
    return u | (u >> 16)


def _route_kernel(x_ref, wpq_ref, sk_ref, c1_ref, e1_ref, r2_ref, e2_ref, q_scr, *, H, nk, topk):
    q_scr[...] = jnp.dot(x_ref[...].astype(BF16), wpq_ref[...], preferred_element_type=F32)
    T = x_ref.shape[0]
    half = sk_ref.shape[-1]
    kk = topk + 1
    sub = lax.broadcasted_iota(jnp.int32, (SUBLANES, T), 0)

    def head(h, carry):
        base = pl.multiple_of(h * 2 * half, 2 * half)
        qa = q_scr[:, pl.ds(base, half)].astype(BF16)
        qb = q_scr[:, pl.ds(base + half, half)].astype(BF16)
        s1 = lax.dot_general(sk_ref[0], qa, _NT, preferred_element_type=F32)
        s2 = lax.dot_general(sk_ref[1], qb, _NT, preferred_element_type=F32)
        a = _top_desc_sorted(s1, kk)
        b = _top_desc_sorted(s2, kk)
        pad = [jnp.full((1, T), NEG_INF, F32)] * ((-kk) % SUBLANES)
        a_arr = jnp.concatenate(a + pad, axis=0)
        b_arr = jnp.concatenate(b + pad, axis=0)
        slabs = []
        for p in range(2):
            for s0 in range(0, kk // (p + 1), SUBLANES):
                slabs.append(a[p] + b_arr[s0:s0 + SUBLANES, :])
        for q in range(2):
            for s0 in range(0, kk // (q + 1), SUBLANES):
                blk = b[q] + a_arr[s0:s0 + SUBLANES, :]
                slabs.append(jnp.where(sub >= 2, blk, NEG_INF) if s0 == 0 else blk)
        rest = [a[p] + b[q] for p in range(2, kk) for q in range(2, kk) if (p + 1) * (q + 1) <= kk]
        rest = rest + [jnp.full((1, T), NEG_INF, F32)] * ((-len(rest)) % SUBLANES)
        slabs.append(jnp.concatenate(rest, axis=0))
        cs = _extract_desc(jnp.concatenate(slabs, axis=0), kk)
        tau = 0.5 * (cs[topk - 1] + cs[topk])
        z = jnp.zeros((1, T), F32)
        for r in range(topk):
            z = z + jnp.exp(cs[r] - cs[0])
        c1_ref[h] = _dup_bf16_words(_count_greater(tau - s1, b[:topk]))
        e1_ref[h] = _dup_bf16_words(jnp.exp(s1 - a[0]))
        r2_ref[h] = _count_greater(s2, b[:topk]).astype(BF16)
        e2_ref[h] = (jnp.exp(s2 - b[0]) / z).astype(BF16)
        return carry

    lax.fori_loop(0, H, head, 0)


def _peer_route(x2d, wpq, sk, tr):
    T, D = x2d.shape
    H = PEER_HEADS
    nk = sk.shape[1]
    kern = functools.partial(_route_kernel, H=H, nk=nk, topk=PEER_TOPK)
    shp_w = jax.ShapeDtypeStruct((H, nk, T), jnp.uint32)
    shp_b = jax.ShapeDtypeStruct((H, nk, T), BF16)
    ospec = pl.BlockSpec((H, nk, tr), lambda i: (0, 0, i))
    return pl.pallas_call(
        kern,
        grid=(T // tr,),
        in_specs=[pl.BlockSpec((tr, D), lambda i: (i, 0)),
                  pl.BlockSpec(wpq.shape, lambda i: (0, 0)),
                  pl.BlockSpec(sk.shape, lambda i: (0, 0, 0))],
        out_specs=[ospec, ospec, ospec, ospec],
        out_shape=[shp_w, shp_w, shp_b, shp_b],
        scratch_shapes=[pltpu.VMEM((tr, wpq.shape[1]), F32)],
        compiler_params=_cparams(("parallel",)),
        name="peer_route",
    )(x2d, wpq, sk)


def _peer_kernel(x_ref, c1_ref, e1_ref, r2_ref, e2_ref, u_ref, vt_ref, g_ref, b_ref, o_ref,
                 xb_scr, act_scr, w_scr, acc_scr, bc1_scr, be1_scr, *, alpha, H, nk, ib, sub):
    step = pl.program_id(1)
    tt = x_ref.shape[0]
    stages = ib // sub
    se = sub * nk
    rc = 2 * SUBLANES
    grp = 4

    @pl.when(step == 0)
    def _():
        xb_scr[...] = x_ref[...].astype(BF16)
        acc_scr[...] = jnp.zeros(acc_scr.shape, F32)

    def act_mm(k):
        act_scr[k] = lax.dot_general(u_ref[k * se:(k + 1) * se, :], xb_scr[...], _NT,
                                     preferred_element_type=F32)

    def out_mm(k):
        acc_scr[...] += jnp.dot(vt_ref[:, k * se:(k + 1) * se], w_scr[k],
                                preferred_element_type=F32)

    def gates(k):
        for half in range(sub):
            ii = k * sub + half
            par = ii % 2
            for h in range(H):
                bc1_scr[par, h] = pltpu.bitcast(
                    jnp.broadcast_to(c1_ref[h, ii:ii + 1, :], (SUBLANES, tt)), BF16)
                be1_scr[par, h] = pltpu.bitcast(
                    jnp.broadcast_to(e1_ref[h, ii:ii + 1, :], (SUBLANES, tt)), BF16)
            for g in range(nk // (grp * rc)):
                accs = [jnp.zeros((rc, tt), BF16) for _ in range(grp)]
                for h in range(H):
                    cb = bc1_scr[par, h]
                    eb = be1_scr[par, h]
                    for c in range(grp):
                        j0 = (g * grp + c) * rc
                        sel = jnp.where(r2_ref[h, j0:j0 + rc, :] < cb,
                                        e2_ref[h, j0:j0 + rc, :], jnp.zeros((), BF16))
                        accs[c] = accs[c] + sel * eb
                for c in range(grp):
                    e0 = half * nk + (g * grp + c) * rc
                    a = act_scr[k, e0:e0 + rc, :]
                    gel = a * (1.0 + lax.erf(a))
                    w_scr[k, e0:e0 + rc, :] = accs[c] * gel.astype(BF16)

    act_mm(0)
    for k in range(stages):
        if k + 1 < stages:
            act_mm(k + 1)
        gates(k)
        if k >= 1:
            out_mm(k - 1)
    out_mm(stages - 1)

    @pl.when(step == pl.num_programs(1) - 1)
    def _():
        y = alpha * x_ref[...] + acc_scr[...].T
        o_ref[...] = _layer_norm(y, g_ref[...], b_ref[...])


def _peer_experts_ln(x2d, c1, e1, r2, e2, u_bf, vt_bf, g, b, alpha, tt, ib):
    T, D = x2d.shape
    H, nk, _ = c1.shape
    ne = ib * nk
    steps = nk // ib
    sub = 4
    kern = functools.partial(_peer_kernel, alpha=alpha, H=H, nk=nk, ib=ib, sub=sub)
    return pl.pallas_call(
        kern,
        grid=(T // tt, steps),
        in_specs=[pl.BlockSpec((tt, D), lambda t, s: (t, 0)),
                  pl.BlockSpec((H, ib, tt), lambda t, s: (0, s, t)),
                  pl.BlockSpec((H, ib, tt), lambda t, s: (0, s, t)),
                  pl.BlockSpec((H, nk, tt), lambda t, s: (0, 0, t)),
                  pl.BlockSpec((H, nk, tt), lambda t, s: (0, 0, t)),
                  pl.BlockSpec((ne, D), lambda t, s: (s, 0)),
                  pl.BlockSpec((D, ne), lambda t, s: (0, s)),
                  pl.BlockSpec((1, D), lambda t, s: (0, 0)),
                  pl.BlockSpec((1, D), lambda t, s: (0, 0))],
        out_specs=pl.BlockSpec((tt, D), lambda t, s: (t, 0)),
        out_shape=jax.ShapeDtypeStruct((T, D), F32),
        scratch_shapes=[pltpu.VMEM((tt, D), BF16),
                        pltpu.VMEM((ib // sub, sub * nk, tt), F32),
                        pltpu.VMEM((ib // sub, sub * nk, tt), BF16),
                        pltpu.VMEM((D, tt), F32),
                        pltpu.VMEM((2, H, 2 * SUBLANES, tt), BF16),
                        pltpu.VMEM((2, H, 2 * SUBLANES, tt), BF16)],
        compiler_params=_cparams(("parallel", "arbitrary")),
        name="peer_experts_ln",
    )(x2d, c1, e1, r2, e2, u_bf, vt_bf, g, b)


def _tile(n, pref):
    t = min(n, pref)
    assert n % t == 0, (n, pref)
    return t


def kernel(x, mem, w_in, i_bias, f_bias, conv_w, conv_b, lam_qk, da_norm_g, ml_norm_g, w_out,
           ln1_g, ln1_b, wq_mem, wkv_mem, wo_mem, ln2_g, ln2_b, w_pq, sub_keys, u_tab, v_tab,
           ln3_g, ln3_b):
    B, S, D = x.shape
    depth = w_in.shape[0]
    T = B * S
    M = mem.shape[1]
    alpha = (2.0 * depth) ** 0.25
    wda = DA_HEADS * da_norm_g.shape[-1]
    wml = ml_norm_g.shape[-1]
    assert w_in.shape[-1] == 3 * wda + 4 * wml + 2 * ML_HEADS
    assert wda == wml and wml % LANES == 0 and 2 * ML_HEADS <= LANES

    tm = _tile(T, 512)
    tq = _tile(S, 1024)
    L = _tile(S, 256)
    tmem = _tile(S, 512)
    tr = _tile(T, 256)
    tt = _tile(T, 512)
    ib = 2 * SUBLANES

    mem2d = mem.reshape(B * M, D)
    o_q, o_k, o_v = 0, wda, 2 * wda
    o_mq = 3 * wda
    o_mk, o_mv, o_mo, o_gt = o_mq + wml, o_mq + 2 * wml, o_mq + 3 * wml, o_mq + 4 * wml

    for l in range(depth):
        lam_init = 0.8 - 0.6 * math.exp(-0.3 * l)
        wl = w_in[l]
        wb = jnp.concatenate([wl[:, o_q:o_mq], wl[:, o_mv:o_mo]], axis=1).astype(BF16)
        wf = jnp.concatenate(
            [wl[:, o_mq:o_mv], wl[:, o_mo:o_gt],
             jnp.pad(wl[:, o_gt:], ((0, 0), (0, LANES - 2 * ML_HEADS)))], axis=1).astype(BF16)
        gbias = jnp.pad(jnp.concatenate([i_bias[l], f_bias[l]]),
                        (0, LANES - 2 * ML_HEADS)).reshape(1, LANES)
        lami = jnp.full((1, da_norm_g.shape[-1]), lam_init, F32)

        x2d = x.reshape(T, D)
        hb, hf = _inproj(x2d, wb, wf, tm)
        hb3 = hb.reshape(B, S, -1)
        hf3 = hf.reshape(B, S, -1)
        da = _diff_attention(hb3, lam_qk[l], da_norm_g[l].reshape(1, -1), lami, tq)
        hm = _mlstm(hf3, hb3, conv_w[l], conv_b[l].reshape(1, -1), gbias,
                    ml_norm_g[l].reshape(1, -1), L)
        x1 = _outproj_ln(x2d, da.reshape(T, -1), hm.reshape(T, -1), w_out[l].astype(BF16),
                         ln1_g[l].reshape(1, D), ln1_b[l].reshape(1, D), alpha, tm)

        kv = _matmul(mem2d, wkv_mem[l].astype(BF16), _tile(B * M, 256), BF16)
        x2 = _memattn_ln(x1.reshape(B, S, D), kv.reshape(B, M, 2 * D), wq_mem[l].astype(BF16),
                         wo_mem[l].astype(BF16), ln2_g[l].reshape(1, D), ln2_b[l].reshape(1, D),
                         alpha, tmem)
        x2d = x2.reshape(T, D)

        c1, e1, r2, e2 = _peer_route(x2d, w_pq[l].astype(BF16), sub_keys[l].astype(BF16), tr)
        x3 = _peer_experts_ln(x2d, c1, e1, r2, e2, (u_tab[l] * RSQRT2).astype(BF16),
                              (v_tab[l].T * RSQRT2).astype(BF16), ln3_g[l].reshape(1, D),
                              ln3_b[l].reshape(1, D), alpha, tt, ib)
        x = x3.reshape(B, S, D)
    return x
```

```python
import functools
import math

import jax
import jax.numpy as jnp
from jax import lax
from jax.experimental import pallas as pl
from jax.experimental.pallas import tpu as pltpu

F32 = jnp.float32
BF16 = jnp.bfloat16
LN_EPS = 1e-5
NEG_INF = float("-inf")
RSQRT2 = 2.0 ** -0.5

DA_HEADS = 4
ML_HEADS = 4
MEM_HEADS = 4
PEER_HEADS = 8
PEER_TOPK = 16
CONV_K = 4

VMEM_LIMIT_BYTES = 56 * 1024 * 1024
LANES = 128
SUBLANES = 8

_NT = (((1,), (1,)), ((), ()))


def _cparams(sem):
    return pltpu.CompilerParams(dimension_semantics=sem, vmem_limit_bytes=VMEM_LIMIT_BYTES)


def _layer_norm(y, g, b):
    mu = jnp.mean(y, axis=-1, keepdims=True)
    yc = y - mu
    var = jnp.mean(yc * yc, axis=-1, keepdims=True)
    return yc * lax.rsqrt(var + LN_EPS) * g + b


def _inproj_kernel(x_ref, wb_ref, wf_ref, hb_ref, hf_ref):
    xb = x_ref[...].astype(BF16)
    hb_ref[...] = jnp.dot(xb, wb_ref[...], preferred_element_type=F32).astype(BF16)
    hf_ref[...] = jnp.dot(xb, wf_ref[...], preferred_element_type=F32)


def _inproj(x2d, wb, wf, tm):
    T, D = x2d.shape
    nb, nf = wb.shape[1], wf.shape[1]
    return pl.pallas_call(
        _inproj_kernel,
        grid=(T // tm,),
        in_specs=[pl.BlockSpec((tm, D), lambda i: (i, 0)),
                  pl.BlockSpec((D, nb), lambda i: (0, 0)),
                  pl.BlockSpec((D, nf), lambda i: (0, 0))],
        out_specs=[pl.BlockSpec((tm, nb), lambda i: (i, 0)),
                   pl.BlockSpec((tm, nf), lambda i: (i, 0))],
        out_shape=[jax.ShapeDtypeStruct((T, nb), BF16),
                   jax.ShapeDtypeStruct((T, nf), F32)],
        compiler_params=_cparams(("parallel",)),
        name="inproj",
    )(x2d, wb, wf)


def _mm_kernel(x_ref, w_ref, o_ref):
    o_ref[...] = jnp.dot(x_ref[...].astype(BF16), w_ref[...],
                         preferred_element_type=F32).astype(o_ref.dtype)


def _matmul(x2d, w, tm, out_dtype):
    M, K = x2d.shape
    N = w.shape[1]
    return pl.pallas_call(
        _mm_kernel,
        grid=(M // tm,),
        in_specs=[pl.BlockSpec((tm, K), lambda i: (i, 0)),
                  pl.BlockSpec((K, N), lambda i: (0, 0))],
        out_specs=pl.BlockSpec((tm, N), lambda i: (i, 0)),
        out_shape=jax.ShapeDtypeStruct((M, N), out_dtype),
        compiler_params=_cparams(("parallel",)),
        name="matmul",
    )(x2d, w)


def _da_kernel(q_ref, k_ref, v_ref, lq_ref, g_ref, lami_ref, o_ref,
               m1, a1, m2, a2, *, tq, ts, dk):
    qb = pl.program_id(2)
    dv = q_ref.shape[-1]
    rep = tq // LANES

    lane = lax.broadcasted_iota(jnp.int32, (1, dv), 1)
    qf = q_ref[...].astype(F32) * (dk ** -0.5 * math.log2(math.e))
    q1 = jnp.where(lane < dk, qf, 0.0).astype(BF16)
    q2 = jnp.where(lane >= dk, qf, 0.0).astype(BF16)
    ones = jnp.ones((tq, dv), BF16)

    m1[...] = jnp.full(m1.shape, NEG_INF, F32)
    m2[...] = jnp.full(m2.shape, NEG_INF, F32)
    a1[...] = jnp.zeros(a1.shape, F32)
    a2[...] = jnp.zeros(a2.shape, F32)

    def update(s, vones, m_ref, a_ref, r0):
        m_prev = m_ref[r0:r0 + ts, :]
        m_cur = jnp.max(s, axis=1, keepdims=True)
        m_next = jnp.maximum(m_prev, m_cur)
        p = jnp.exp2(s - jnp.concatenate([m_next] * rep, axis=1))
        alpha = jnp.exp2(m_prev - m_next)
        a_ref[r0:r0 + ts, :] = (jnp.concatenate([alpha, alpha], axis=1) * a_ref[r0:r0 + ts, :]
                                + jnp.dot(p.astype(BF16), vones, preferred_element_type=F32))
        m_ref[r0:r0 + ts, :] = m_next

    def block(kb, masked):
        off = pl.multiple_of(kb * tq, tq)
        kblk = k_ref[pl.ds(off, tq), :]
        vblk = jnp.concatenate([v_ref[pl.ds(off, tq), :], ones], axis=1)
        for st in range(tq // ts):
            r0 = st * ts
            s1 = lax.dot_general(q1[r0:r0 + ts], kblk, _NT, preferred_element_type=F32)
            s2 = lax.dot_general(q2[r0:r0 + ts], kblk, _NT, preferred_element_type=F32)
            if masked:
                row = lax.broadcasted_iota(jnp.int32, (ts, tq), 0) + r0
                col = lax.broadcasted_iota(jnp.int32, (ts, tq), 1)
                keep = col <= row
                s1 = jnp.where(keep, s1, NEG_INF)
                s2 = jnp.where(keep, s2, NEG_INF)
            update(s1, vblk, m1, a1, r0)
            update(s2, vblk, m2, a2, r0)

    def body(kb, carry):
        block(kb, False)
        return carry

    lax.fori_loop(0, qb, body, 0)
    block(qb, True)

    lq = lq_ref[...]
    lam_init = lami_ref[...][:, :1]
    lam = (jnp.exp(jnp.sum(lq[0:1] * lq[1:2], axis=1, keepdims=True))
           - jnp.exp(jnp.sum(lq[2:3] * lq[3:4], axis=1, keepdims=True)) + lam_init)
    o = a1[:, :dv] / a1[:, dv:] - lam * (a2[:, :dv] / a2[:, dv:])
    o = o * lax.rsqrt(jnp.mean(o * o, axis=-1, keepdims=True) + LN_EPS)
    o = o * g_ref[...] * (1.0 - lami_ref[...])
    o_ref[...] = o.astype(o_ref.dtype)


def _diff_attention(hb3, lam_qk, da_g, lami, tq):
    B, S, _ = hb3.shape
    dv = da_g.shape[-1]
    dk = dv // 2
    H = DA_HEADS
    assert dv == LANES
    kern = functools.partial(_da_kernel, tq=tq, ts=tq, dk=dk)
    return pl.pallas_call(
        kern,
        grid=(B, H, S // tq),
        in_specs=[pl.BlockSpec((None, tq, dv), lambda b, h, i: (b, i, h)),
                  pl.BlockSpec((None, S, dv), lambda b, h, i: (b, 0, H + h)),
                  pl.BlockSpec((None, S, dv), lambda b, h, i: (b, 0, 2 * H + h)),
                  pl.BlockSpec(lam_qk.shape, lambda b, h, i: (0, 0)),
                  pl.BlockSpec((1, dv), lambda b, h, i: (0, 0)),
                  pl.BlockSpec((1, dv), lambda b, h, i: (0, 0))],
        out_specs=pl.BlockSpec((None, tq, dv), lambda b, h, i: (b, i, h)),
        out_shape=jax.ShapeDtypeStruct((B, S, H * dv), BF16),
        scratch_shapes=[pltpu.VMEM((tq, LANES), F32), pltpu.VMEM((tq, 2 * dv), F32),
                        pltpu.VMEM((tq, LANES), F32), pltpu.VMEM((tq, 2 * dv), F32)],
        compiler_params=_cparams(("parallel", "parallel", "arbitrary")),
        name="diff_attention",
    )(hb3, hb3, hb3, lam_qk, da_g, lami)


def _log_sigmoid(x):
    return -(jnp.maximum(-x, 0.0) + jnp.log1p(jnp.exp(-jnp.abs(x))))


def _ml_kernel(qk_ref, og_ref, gt_ref, v_ref, cw_ref, cb_ref, gb_ref, g_ref, out_ref,
               buf, c_scr, n_scr, m_scr, *, L, H, dh):
    c = pl.program_id(1)
    W = H * dh

    @pl.when(c == 0)
    def _():
        buf[0:SUBLANES, :] = jnp.zeros((SUBLANES, buf.shape[1]), F32)
        c_scr[...] = jnp.zeros(c_scr.shape, F32)
        n_scr[...] = jnp.zeros(n_scr.shape, F32)
        m_scr[...] = jnp.zeros(m_scr.shape, F32)

    buf[SUBLANES:SUBLANES + L, :] = qk_ref[...]
    cw = cw_ref[...]
    y = cb_ref[...]
    for j in range(CONV_K):
        s0 = SUBLANES - (CONV_K - 1) + j
        y = y + cw[j:j + 1, :] * buf[s0:s0 + L, :]
    buf[0:SUBLANES, :] = buf[L:L + SUBLANES, :]
    qk = y * jax.nn.sigmoid(y)

    G = gt_ref[...] + gb_ref[...]
    ls = _log_sigmoid(G)
    row = lax.broadcasted_iota(jnp.int32, (L, L), 0)
    col = lax.broadcasted_iota(jnp.int32, (L, L), 1)
    causal = col <= row
    tri = causal.astype(F32)
    tri_t = (row <= col).astype(F32)
    bcol_all = jnp.dot(tri, ls, precision=lax.Precision.HIGHEST,
                       preferred_element_type=F32)
    GT = G.T
    brow_all = jnp.dot(ls.T, tri_t, precision=lax.Precision.HIGHEST,
                       preferred_element_type=F32)

    for h in range(H):
        b_col = bcol_all[:, H + h:H + h + 1]
        i_col = G[:, h:h + 1]
        b_row = brow_all[H + h:H + h + 1, :]
        i_row = GT[h:h + 1, :]
        b_last = bcol_all[L - 1:L, H + h:H + h + 1]
        m_old = m_scr[h:h + 1, 0:1]

        a = b_col + m_old
        d = jnp.where(causal, b_col - b_row + i_row, NEG_INF)
        m_t = jnp.maximum(a, jnp.max(d, axis=1, keepdims=True))
        w_inter = jnp.exp(a - m_t)

        q = qk[:, h * dh:(h + 1) * dh]
        k = qk[:, W + h * dh:W + (h + 1) * dh] * (dh ** -0.5)
        qb = q.astype(BF16)
        kb = k.astype(BF16)
        vh = v_ref[:, h * dh:(h + 1) * dh]
        s = lax.dot_general(qb, kb, _NT, preferred_element_type=F32)
        w_intra = jnp.exp(d - m_t) * s
        c_old = c_scr[h]
        n_old = n_scr[h:h + 1, :]
        num = (w_inter * jnp.dot(qb, c_old.astype(BF16), preferred_element_type=F32)
               + jnp.dot(w_intra.astype(BF16), vh, preferred_element_type=F32))
        qn = jnp.sum(q * n_old, axis=1, keepdims=True)
        nq = w_inter * qn + jnp.sum(w_intra, axis=1, keepdims=True)
        hh = num / jnp.maximum(jnp.abs(nq), jnp.exp(-m_t))

        g_col = b_last - b_col + i_col
        g_row = b_last - b_row + i_row
        m_new = jnp.maximum(b_last + m_old, jnp.max(g_row, axis=1, keepdims=True))
        decay = jnp.exp(b_last + m_old - m_new)
        wk = jnp.exp(g_col - m_new) * k
        c_scr[h] = decay * c_old + jnp.dot(wk.T.astype(BF16), vh,
                                           preferred_element_type=F32)
        n_scr[h:h + 1, :] = decay * n_old + jnp.sum(wk, axis=0, keepdims=True)
        m_scr[h:h + 1, :] = jnp.broadcast_to(m_new, (1, m_scr.shape[1]))

        mu = jnp.mean(hh, axis=-1, keepdims=True)
        hc = hh - mu
        var = jnp.mean(hc * hc, axis=-1, keepdims=True)
        hn = hc * lax.rsqrt(var + LN_EPS)
        gate = jax.nn.sigmoid(og_ref[:, h * dh:(h + 1) * dh])
        out_ref[:, h * dh:(h + 1) * dh] = (
            hn * g_ref[:, h * dh:(h + 1) * dh] * gate).astype(out_ref.dtype)


def _mlstm(hf3, hb3, conv_w, conv_b, gbias, ml_g, L):
    B, S, _ = hf3.shape
    W = ml_g.shape[-1]
    H = ML_HEADS
    dh = W // H
    kern = functools.partial(_ml_kernel, L=L, H=H, dh=dh)
    return pl.pallas_call(
        kern,
        grid=(B, S // L),
        in_specs=[pl.BlockSpec((None, L, 2 * W), lambda b, c: (b, c, 0)),
                  pl.BlockSpec((None, L, W), lambda b, c: (b, c, 2)),
                  pl.BlockSpec((None, L, LANES), lambda b, c: (b, c, 3 * W // LANES)),
                  pl.BlockSpec((None, L, W), lambda b, c: (b, c, 3)),
                  pl.BlockSpec((CONV_K, 2 * W), lambda b, c: (0, 0)),
                  pl.BlockSpec((1, 2 * W), lambda b, c: (0, 0)),
                  pl.BlockSpec((1, LANES), lambda b, c: (0, 0)),
                  pl.BlockSpec((1, W), lambda b, c: (0, 0))],
        out_specs=pl.BlockSpec((None, L, W), lambda b, c: (b, c, 0)),
        out_shape=jax.ShapeDtypeStruct((B, S, W), BF16),
        scratch_shapes=[pltpu.VMEM((L + SUBLANES, 2 * W), F32),
                        pltpu.VMEM((H, dh, dh), F32),
                        pltpu.VMEM((SUBLANES, dh), F32),
                        pltpu.VMEM((SUBLANES, LANES), F32)],
        compiler_params=_cparams(("parallel", "arbitrary")),
        name="mlstm",
    )(hf3, hf3, hf3, hb3, conv_w, conv_b, gbias, ml_g)


def _outproj_kernel(x_ref, da_ref, hm_ref, w_ref, g_ref, b_ref, o_ref, *, alpha):
    wd = da_ref.shape[-1]
    mix = (jnp.dot(da_ref[...], w_ref[0:wd, :], preferred_element_type=F32)
           + jnp.dot(hm_ref[...], w_ref[wd:, :], preferred_element_type=F32))
    o_ref[...] = _layer_norm(alpha * x_ref[...] + mix, g_ref[...], b_ref[...])


def _outproj_ln(x2d, da2d, hm2d, w_out, g, b, alpha, tm):
    T, D = x2d.shape
    wd, wm = da2d.shape[1], hm2d.shape[1]
    kern = functools.partial(_outproj_kernel, alpha=alpha)
    return pl.pallas_call(
        kern,
        grid=(T // tm,),
        in_specs=[pl.BlockSpec((tm, D), lambda i: (i, 0)),
                  pl.BlockSpec((tm, wd), lambda i: (i, 0)),
                  pl.BlockSpec((tm, wm), lambda i: (i, 0)),
                  pl.BlockSpec((wd + wm, D), lambda i: (0, 0)),
                  pl.BlockSpec((1, D), lambda i: (0, 0)),
                  pl.BlockSpec((1, D), lambda i: (0, 0))],
        out_specs=pl.BlockSpec((tm, D), lambda i: (i, 0)),
        out_shape=jax.ShapeDtypeStruct((T, D), F32),
        compiler_params=_cparams(("parallel",)),
        name="outproj_ln",
    )(x2d, da2d, hm2d, w_out, g, b)


def _memattn_kernel(x_ref, kv_ref, wq_ref, wo_ref, g_ref, b_ref, o_ref, *, alpha, H):
    x = x_ref[...]
    D = x.shape[-1]
    dh = D // H
    q = jnp.dot(x.astype(BF16), wq_ref[...], preferred_element_type=F32) * (dh ** -0.5)
    qb = q.astype(BF16)
    outs = []
    for h in range(H):
        kh = kv_ref[:, h * dh:(h + 1) * dh]
        vh = kv_ref[:, D + h * dh:D + (h + 1) * dh]
        s = lax.dot_general(qb[:, h * dh:(h + 1) * dh], kh, _NT, preferred_element_type=F32)
        s = s - jnp.max(s, axis=-1, keepdims=True)
        e = jnp.exp(s)
        p = e / jnp.sum(e, axis=-1, keepdims=True)
        outs.append(jnp.dot(p.astype(BF16), vh, preferred_element_type=F32))
    o = jnp.concatenate(outs, axis=-1).astype(BF16)
    att = jnp.dot(o, wo_ref[...], preferred_element_type=F32)
    o_ref[...] = _layer_norm(alpha * x + att, g_ref[...], b_ref[...])


def _memattn_ln(x3, kv3, wq, wo, g, b, alpha, tm):
    B, S, D = x3.shape
    M = kv3.shape[1]
    kern = functools.partial(_memattn_kernel, alpha=alpha, H=MEM_HEADS)
    return pl.pallas_call(
        kern,
        grid=(B, S // tm),
        in_specs=[pl.BlockSpec((None, tm, D), lambda bb, i: (bb, i, 0)),
                  pl.BlockSpec((None, M, 2 * D), lambda bb, i: (bb, 0, 0)),
                  pl.BlockSpec((D, D), lambda bb, i: (0, 0)),
                  pl.BlockSpec((D, D), lambda bb, i: (0, 0)),
                  pl.BlockSpec((1, D), lambda bb, i: (0, 0)),
                  pl.BlockSpec((1, D), lambda bb, i: (0, 0))],
        out_specs=pl.BlockSpec((None, tm, D), lambda bb, i: (bb, i, 0)),
        out_shape=jax.ShapeDtypeStruct((B, S, D), F32),
        compiler_params=_cparams(("parallel", "parallel")),
        name="memattn_ln",
    )(x3, kv3, wq, wo, g, b)


def _sort_network(n):
    def merge(lo, hi, r):
        step = r * 2
        if step < hi - lo:
            yield from merge(lo, hi, step)
            yield from merge(lo + r, hi, step)
            yield from [(i, i + r) for i in range(lo + r, hi - r, step)]
        else:
            yield (lo, lo + r)

    def sort(lo, hi):
        if hi - lo >= 1:
            mid = lo + (hi - lo) // 2
            yield from sort(lo, mid)
            yield from sort(mid + 1, hi)
            yield from merge(lo, hi, 1)

    return list(sort(0, n - 1))


def _top_desc_sorted(s, count):
    rows = s.shape[0]
    n = rows // SUBLANES
    lists = [s[g * SUBLANES:(g + 1) * SUBLANES, :] for g in range(n)]
    for i, j in _sort_network(n):
        hi = jnp.maximum(lists[i], lists[j])
        lo = jnp.minimum(lists[i], lists[j])
        lists[i], lists[j] = hi, lo
    tops = []
    for r in range(count):
        mx = jnp.max(lists[0], axis=0, keepdims=True)
        tops.append(mx)
        if r + 1 < count:
            hit = lists[0] == mx
            depth = min(n, count - 1 - r)
            for l in range(depth):
                nxt = lists[l + 1] if l + 1 < n else NEG_INF
                lists[l] = jnp.where(hit, nxt, lists[l])
    return tops


def _extract_desc(v, count):
    tops = []
    for r in range(count):
        mx = jnp.max(v, axis=0, keepdims=True)
        tops.append(mx)
        if r + 1 < count:
            v = jnp.where(v == mx, NEG_INF, v)
    return tops


def _count_greater(x, t):
    assert len(t) == 16
    m8 = t[7] > x
    m4 = jnp.where(m8, t[11], t[3]) > x
    m2 = jnp.where(m8, jnp.where(m4, t[13], t[9]), jnp.where(m4, t[5], t[1])) > x
    th = jnp.where(m8,
                   jnp.where(m4, jnp.where(m2, t[14], t[12]), jnp.where(m2, t[10], t[8])),
                   jnp.where(m4, jnp.where(m2, t[6], t[4]), jnp.where(m2, t[2], t[0])))
    m1 = th > x
    cnt = (jnp.where(m8, 8.0, 0.0) + jnp.where(m4, 4.0, 0.0)
           + jnp.where(m2, 2.0, 0.0) + jnp.where(m1, 1.0, 0.0))
    return cnt + jnp.where(t[15] > x, 1.0, 0.0)


def _dup_bf16_words(x):
    u = lax.bitcast_convert_type(x.astype(BF16).astype(F32), jnp.uint32)
    return u | (u >> 16)


def _route_kernel(x_ref, wpq_ref, sk_ref, c1_ref, e1_ref, r2_ref, e2_ref, q_scr, *, H, nk, topk):
    q_scr[...] = jnp.dot(x_ref[...].astype(BF16), wpq_ref[...], preferred_element_type=F32)
    T = x_ref.shape[0]
    half = sk_ref.shape[-1]
    kk = topk + 1
    sub = lax.broadcasted_iota(jnp.int32, (SUBLANES, T), 0)

    def head(h, carry):
        base = pl.multiple_of(h * 2 * half, 2 * half)
        qa = q_scr[:, pl.ds(base, half)].astype(BF16)
        qb = q_scr[:, pl.ds(base + half, half)].astype(BF16)
        s1 = lax.dot_general(sk_ref[0], qa, _NT, preferred_element_type=F32)
        s2 = lax.dot_general(sk_ref[1], qb, _NT, preferred_element_type=F32)
        a = _top_desc_sorted(s1, kk)
        b = _top_desc_sorted(s2, kk)
        pad = [jnp.full((1, T), NEG_INF, F32)] * ((-kk) % SUBLANES)
        a_arr = jnp.concatenate(a + pad, axis=0)
        b_arr = jnp.concatenate(b + pad, axis=0)
        slabs = []
        for p in range(2):
            for s0 in range(0, kk // (p + 1), SUBLANES):
                slabs.append(a[p] + b_arr[s0:s0 + SUBLANES, :])
        for q in range(2):
            for s0 in range(0, kk // (q + 1), SUBLANES):
                blk = b[q] + a_arr[s0:s0 + SUBLANES, :]
                slabs.append(jnp.where(sub >= 2, blk, NEG_INF) if s0 == 0 else blk)
        rest = [a[p] + b[q] for p in range(2, kk) for q in range(2, kk) if (p + 1) * (q + 1) <= kk]
        rest = rest + [jnp.full((1, T), NEG_INF, F32)] * ((-len(rest)) % SUBLANES)
        slabs.append(jnp.concatenate(rest, axis=0))
        cs = _extract_desc(jnp.concatenate(slabs, axis=0), kk)
        tau = 0.5 * (cs[topk - 1] + cs[topk])
        z = jnp.zeros((1, T), F32)
        for r in range(topk):
            z = z + jnp.exp(cs[r] - cs[0])
        c1_ref[h] = _dup_bf16_words(_count_greater(tau - s1, b[:topk]))
        e1_ref[h] = _dup_bf16_words(jnp.exp(s1 - a[0]))
        r2_ref[h] = _count_greater(s2, b[:topk]).astype(BF16)
        e2_ref[h] = (jnp.exp(s2 - b[0]) / z).astype(BF16)
        return carry

    lax.fori_loop(0, H, head, 0)


def _peer_route(x2d, wpq, sk, tr):
    T, D = x2d.shape
    H = PEER_HEADS
    nk = sk.shape[1]
    kern = functools.partial(_route_kernel, H=H, nk=nk, topk=PEER_TOPK)
    shp_w = jax.ShapeDtypeStruct((H, nk, T), jnp.uint32)
    shp_b = jax.ShapeDtypeStruct((H, nk, T), BF16)
    ospec = pl.BlockSpec((H, nk, tr), lambda i: (0, 0, i))
    return pl.pallas_call(
        kern,
        grid=(T // tr,),
        in_specs=[pl.BlockSpec((tr, D), lambda i: (i, 0)),
                  pl.BlockSpec(wpq.shape, lambda i: (0, 0)),
                  pl.BlockSpec(sk.shape, lambda i: (0, 0, 0))],
        out_specs=[ospec, ospec, ospec, ospec],
        out_shape=[shp_w, shp_w, shp_b, shp_b],
        scratch_shapes=[pltpu.VMEM((tr, wpq.shape[1]), F32)],
        compiler_params=_cparams(("parallel",)),
        name="peer_route",
    )(x2d, wpq, sk)


def _peer_kernel(x_ref, c1_ref, e1_ref, r2_ref, e2_ref, u_ref, vt_ref, g_ref, b_ref, o_ref,
                 xb_scr, act_scr, w_scr, acc_scr, bc1_scr, be1_scr, *, alpha, H, nk, ib, sub):
    step = pl.program_id(1)
    tt = x_ref.shape[0]
    stages = ib // sub
    se = sub * nk
    rc = 2 * SUBLANES
    grp = 4

    @pl.when(step == 0)
    def _():
        xb_scr[...] = x_ref[...].astype(BF16)
        acc_scr[...] = jnp.zeros(acc_scr.shape, F32)

    def act_mm(k):
        act_scr[k] = lax.dot_general(u_ref[k * se:(k + 1) * se, :], xb_scr[...], _NT,
                                     preferred_element_type=F32)

    def out_mm(k):
        acc_scr[...] += jnp.dot(vt_ref[:, k * se:(k + 1) * se], w_scr[k],
                                preferred_element_type=F32)

    def gates(k):
        for half in range(sub):
            ii = k * sub + half
            par = ii % 2
            for h in range(H):
                bc1_scr[par, h] = pltpu.bitcast(
                    jnp.broadcast_to(c1_ref[h, ii:ii + 1, :], (SUBLANES, tt)), BF16)
                be1_scr[par, h] = pltpu.bitcast(
                    jnp.broadcast_to(e1_ref[h, ii:ii + 1, :], (SUBLANES, tt)), BF16)
            for g in range(nk // (grp * rc)):
                accs = [jnp.zeros((rc, tt), BF16) for _ in range(grp)]
                for h in range(H):
                    cb = bc1_scr[par, h]
                    eb = be1_scr[par, h]
                    for c in range(grp):
                        j0 = (g * grp + c) * rc
                        sel = jnp.where(r2_ref[h, j0:j0 + rc, :] < cb,
                                        e2_ref[h, j0:j0 + rc, :], jnp.zeros((), BF16))
                        accs[c] = accs[c] + sel * eb
                for c in range(grp):
                    e0 = half * nk + (g * grp + c) * rc
                    a = act_scr[k, e0:e0 + rc, :]
                    gel = a * (1.0 + lax.erf(a))
                    w_scr[k, e0:e0 + rc, :] = accs[c] * gel.astype(BF16)

    act_mm(0)
    for k in range(stages):
        if k + 1 < stages:
            act_mm(k + 1)
        gates(k)
        if k >= 1:
            out_mm(k - 1)
    out_mm(stages - 1)

    @pl.when(step == pl.num_programs(1) - 1)
    def _():
        y = alpha * x_ref[...] + RSQRT2 * acc_scr[...].T
        o_ref[...] = _layer_norm(y, g_ref[...], b_ref[...])


def _peer_experts_ln(x2d, c1, e1, r2, e2, u_bf, vt_bf, g, b, alpha, tt, ib):
    T, D = x2d.shape
    H, nk, _ = c1.shape
    ne = ib * nk
    steps = nk // ib
    sub = 4
    kern = functools.partial(_peer_kernel, alpha=alpha, H=H, nk=nk, ib=ib, sub=sub)
    return pl.pallas_call(
        kern,
        grid=(T // tt, steps),
        in_specs=[pl.BlockSpec((tt, D), lambda t, s: (t, 0)),
                  pl.BlockSpec((H, ib, tt), lambda t, s: (0, s, t)),
                  pl.BlockSpec((H, ib, tt), lambda t, s: (0, s, t)),
                  pl.BlockSpec((H, nk, tt), lambda t, s: (0, 0, t)),
                  pl.BlockSpec((H, nk, tt), lambda t, s: (0, 0, t)),
                  pl.BlockSpec((ne, D), lambda t, s: (s, 0)),
                  pl.BlockSpec((D, ne), lambda t, s: (0, s)),
                  pl.BlockSpec((1, D), lambda t, s: (0, 0)),
                  pl.BlockSpec((1, D), lambda t, s: (0, 0))],
        out_specs=pl.BlockSpec((tt, D), lambda t, s: (t, 0)),
        out_shape=jax.ShapeDtypeStruct((T, D), F32),
        scratch_shapes=[pltpu.VMEM((tt, D), BF16),
                        pltpu.VMEM((ib // sub, sub * nk, tt), F32),
                        pltpu.VMEM((ib // sub, sub * nk, tt), BF16),
                        pltpu.VMEM((D, tt), F32),
                        pltpu.VMEM((2, H, 2 * SUBLANES, tt), BF16),
                        pltpu.VMEM((2, H, 2 * SUBLANES, tt), BF16)],
        compiler_params=_cparams(("parallel", "arbitrary")),
        name="peer_experts_ln",
    )(x2d, c1, e1, r2, e2, u_bf, vt_bf, g, b)


def _tile(n, pref):
    t = min(n, pref)
    assert n % t == 0, (n, pref)
    return t


def kernel(x, mem, w_in, i_bias, f_bias, conv_w, conv_b, lam_qk, da_norm_g, ml_norm_g, w_out,
           ln1_g, ln1_b, wq_mem, wkv_mem, wo_mem, ln2_g, ln2_b, w_pq, sub_keys, u_tab, v_tab,
           ln3_g, ln3_b):
    B, S, D = x.shape
    depth = w_in.shape[0]
    T = B * S
    M = mem.shape[1]
    alpha = (2.0 * depth) ** 0.25
    wda = DA_HEADS * da_norm_g.shape[-1]
    wml = ml_norm_g.shape[-1]
    assert w_in.shape[-1] == 3 * wda + 4 * wml + 2 * ML_HEADS
    assert wda == wml and wml % LANES == 0 and 2 * ML_HEADS <= LANES

    tm = _tile(T, 512)
    tq = _tile(S, 1024)
    L = _tile(S, 256)
    tmem = _tile(S, 512)
    tr = _tile(T, 256)
    tt = _tile(T, 512)
    ib = 2 * SUBLANES

    mem2d = mem.reshape(B * M, D)
    o_q, o_k, o_v = 0, wda, 2 * wda
    o_mq = 3 * wda
    o_mk, o_mv, o_mo, o_gt = o_mq + wml, o_mq + 2 * wml, o_mq + 3 * wml, o_mq + 4 * wml

    for l in range(depth):
        lam_init = 0.8 - 0.6 * math.exp(-0.3 * l)
        wl = w_in[l]
        wb = jnp.concatenate([wl[:, o_q:o_mq], wl[:, o_mv:o_mo]], axis=1).astype(BF16)
        wf = jnp.concatenate(
            [wl[:, o_mq:o_mv], wl[:, o_mo:o_gt],
             jnp.pad(wl[:, o_gt:], ((0, 0), (0, LANES - 2 * ML_HEADS)))], axis=1).astype(BF16)
        gbias = jnp.pad(jnp.concatenate([i_bias[l], f_bias[l]]),
                        (0, LANES - 2 * ML_HEADS)).reshape(1, LANES)
        lami = jnp.full((1, da_norm_g.shape[-1]), lam_init, F32)

        x2d = x.reshape(T, D)
        hb, hf = _inproj(x2d, wb, wf, tm)
        hb3 = hb.reshape(B, S, -1)
        hf3 = hf.reshape(B, S, -1)
        da = _diff_attention(hb3, lam_qk[l], da_norm_g[l].reshape(1, -1), lami, tq)
        hm = _mlstm(hf3, hb3, conv_w[l], conv_b[l].reshape(1, -1), gbias,
                    ml_norm_g[l].reshape(1, -1), L)
        x1 = _outproj_ln(x2d, da.reshape(T, -1), hm.reshape(T, -1), w_out[l].astype(BF16),
                         ln1_g[l].reshape(1, D), ln1_b[l].reshape(1, D), alpha, tm)

        kv = _matmul(mem2d, wkv_mem[l].astype(BF16), _tile(B * M, 256), BF16)
        x2 = _memattn_ln(x1.reshape(B, S, D), kv.reshape(B, M, 2 * D), wq_mem[l].astype(BF16),
                         wo_mem[l].astype(BF16), ln2_g[l].reshape(1, D), ln2_b[l].reshape(1, D),
                         alpha, tmem)
        x2d = x2.reshape(T, D)

        c1, e1, r2, e2 = _peer_route(x2d, w_pq[l].astype(BF16), sub_keys[l].astype(BF16), tr)
        x3 = _peer_experts_ln(x2d, c1, e1, r2, e2, (u_tab[l] * RSQRT2).astype(BF16),
                              v_tab[l].T.astype(BF16), ln3_g[l].reshape(1, D),
                              ln3_b[l].reshape(1, D), alpha, tt, ib)
        x = x3.reshape(B, S, D)
    return x
```

```python
import functools
import math

import jax
import jax.numpy as jnp
from jax import lax
from jax.experimental import pallas as pl
from jax.experimental.pallas import tpu as pltpu

F32 = jnp.float32
BF16 = jnp.bfloat16
LN_EPS = 1e-5
NEG_INF = float("-inf")
RSQRT2 = 2.0 ** -0.5

DA_HEADS = 4
ML_HEADS = 4
MEM_HEADS = 4
PEER_HEADS = 8
PEER_TOPK = 16
CONV_K = 4

VMEM_LIMIT_BYTES = 56 * 1024 * 1024
LANES = 128
SUBLANES = 8

_NT = (((1,), (1,)), ((), ()))


def _cparams(sem):
    return pltpu.CompilerParams(dimension_semantics=sem, vmem_limit_bytes=VMEM_LIMIT_BYTES)


def _layer_norm(y, g, b):
    mu = jnp.mean(y, axis=-1, keepdims=True)
    yc = y - mu
    var = jnp.mean(yc * yc, axis=-1, keepdims=True)
    return yc * lax.rsqrt(var + LN_EPS) * g + b


def _inproj_kernel(x_ref, wb_ref, wf_ref, hb_ref, hf_ref):
    xb = x_ref[...].astype(BF16)
    hb_ref[...] = jnp.dot(xb, wb_ref[...], preferred_element_type=F32).astype(BF16)
    hf_ref[...] = jnp.dot(xb, wf_ref[...], preferred_element_type=F32)


def _inproj(x2d, wb, wf, tm):
    T, D = x2d.shape
    nb, nf = wb.shape[1], wf.shape[1]
    return pl.pallas_call(
        _inproj_kernel,
        grid=(T // tm,),
        in_specs=[pl.BlockSpec((tm, D), lambda i: (i, 0)),
                  pl.BlockSpec((D, nb), lambda i: (0, 0)),
                  pl.BlockSpec((D, nf), lambda i: (0, 0))],
        out_specs=[pl.BlockSpec((tm, nb), lambda i: (i, 0)),
                   pl.BlockSpec((tm, nf), lambda i: (i, 0))],
        out_shape=[jax.ShapeDtypeStruct((T, nb), BF16),
                   jax.ShapeDtypeStruct((T, nf), F32)],
        compiler_params=_cparams(("parallel",)),
        name="inproj",
    )(x2d, wb, wf)


def _mm_kernel(x_ref, w_ref, o_ref):
    o_ref[...] = jnp.dot(x_ref[...].astype(BF16), w_ref[...],
                         preferred_element_type=F32).astype(o_ref.dtype)


def _matmul(x2d, w, tm, out_dtype):
    M, K = x2d.shape
    N = w.shape[1]
    return pl.pallas_call(
        _mm_kernel,
        grid=(M // tm,),
        in_specs=[pl.BlockSpec((tm, K), lambda i: (i, 0)),
                  pl.BlockSpec((K, N), lambda i: (0, 0))],
        out_specs=pl.BlockSpec((tm, N), lambda i: (i, 0)),
        out_shape=jax.ShapeDtypeStruct((M, N), out_dtype),
        compiler_params=_cparams(("parallel",)),
        name="matmul",
    )(x2d, w)


def _da_kernel(q_ref, k_ref, v_ref, lq_ref, g_ref, lami_ref, o_ref,
               m1, a1, m2, a2, *, tq, ts, dk):
    qb = pl.program_id(2)
    dv = q_ref.shape[-1]
    rep = tq // LANES

    lane = lax.broadcasted_iota(jnp.int32, (1, dv), 1)
    qf = q_ref[...].astype(F32) * (dk ** -0.5 * math.log2(math.e))
    q1 = jnp.where(lane < dk, qf, 0.0).astype(BF16)
    q2 = jnp.where(lane >= dk, qf, 0.0).astype(BF16)
    ones = jnp.ones((tq, dv), BF16)

    m1[...] = jnp.full(m1.shape, NEG_INF, F32)
    m2[...] = jnp.full(m2.shape, NEG_INF, F32)
    a1[...] = jnp.zeros(a1.shape, F32)
    a2[...] = jnp.zeros(a2.shape, F32)

    def update(s, vones, m_ref, a_ref, r0):
        m_prev = m_ref[r0:r0 + ts, :]
        m_cur = jnp.max(s, axis=1, keepdims=True)
        m_next = jnp.maximum(m_prev, m_cur)
        p = jnp.exp2(s - jnp.concatenate([m_next] * rep, axis=1))
        alpha = jnp.exp2(m_prev - m_next)
        a_ref[r0:r0 + ts, :] = (jnp.concatenate([alpha, alpha], axis=1) * a_ref[r0:r0 + ts, :]
                                + jnp.dot(p.astype(BF16), vones, preferred_element_type=F32))
        m_ref[r0:r0 + ts, :] = m_next

    def block(kb, masked):
        off = pl.multiple_of(kb * tq, tq)
        kblk = k_ref[pl.ds(off, tq), :]
        vblk = jnp.concatenate([v_ref[pl.ds(off, tq), :], ones], axis=1)
        for st in range(tq // ts):
            r0 = st * ts
            s1 = lax.dot_general(q1[r0:r0 + ts], kblk, _NT, preferred_element_type=F32)
            s2 = lax.dot_general(q2[r0:r0 + ts], kblk, _NT, preferred_element_type=F32)
            if masked:
                row = lax.broadcasted_iota(jnp.int32, (ts, tq), 0) + r0
                col = lax.broadcasted_iota(jnp.int32, (ts, tq), 1)
                keep = col <= row
                s1 = jnp.where(keep, s1, NEG_INF)
                s2 = jnp.where(keep, s2, NEG_INF)
            update(s1, vblk, m1, a1, r0)
            update(s2, vblk, m2, a2, r0)

    def body(kb, carry):
        block(kb, False)
        return carry

    lax.fori_loop(0, qb, body, 0)
    block(qb, True)

    lq = lq_ref[...]
    lam_init = lami_ref[...][:, :1]
    lam = (jnp.exp(jnp.sum(lq[0:1] * lq[1:2], axis=1, keepdims=True))
           - jnp.exp(jnp.sum(lq[2:3] * lq[3:4], axis=1, keepdims=True)) + lam_init)
    o = a1[:, :dv] / a1[:, dv:] - lam * (a2[:, :dv] / a2[:, dv:])
    o = o * lax.rsqrt(jnp.mean(o * o, axis=-1, keepdims=True) + LN_EPS)
    o = o * g_ref[...] * (1.0 - lami_ref[...])
    o_ref[...] = o.astype(o_ref.dtype)


def _diff_attention(hb3, lam_qk, da_g, lami, tq):
    B, S, _ = hb3.shape
    dv = da_g.shape[-1]
    dk = dv // 2
    H = DA_HEADS
    assert dv == LANES
    kern = functools.partial(_da_kernel, tq=tq, ts=tq, dk=dk)
    return pl.pallas_call(
        kern,
        grid=(B, H, S // tq),
        in_specs=[pl.BlockSpec((None, tq, dv), lambda b, h, i: (b, i, h)),
                  pl.BlockSpec((None, S, dv), lambda b, h, i: (b, 0, H + h)),
                  pl.BlockSpec((None, S, dv), lambda b, h, i: (b, 0, 2 * H + h)),
                  pl.BlockSpec(lam_qk.shape, lambda b, h, i: (0, 0)),
                  pl.BlockSpec((1, dv), lambda b, h, i: (0, 0)),
                  pl.BlockSpec((1, dv), lambda b, h, i: (0, 0))],
        out_specs=pl.BlockSpec((None, tq, dv), lambda b, h, i: (b, i, h)),
        out_shape=jax.ShapeDtypeStruct((B, S, H * dv), BF16),
        scratch_shapes=[pltpu.VMEM((tq, LANES), F32), pltpu.VMEM((tq, 2 * dv), F32),
                        pltpu.VMEM((tq, LANES), F32), pltpu.VMEM((tq, 2 * dv), F32)],
        compiler_params=_cparams(("parallel", "parallel", "arbitrary")),
        name="diff_attention",
    )(hb3, hb3, hb3, lam_qk, da_g, lami)


def _log_sigmoid(x):
    return -(jnp.maximum(-x, 0.0) + jnp.log1p(jnp.exp(-jnp.abs(x))))


def _ml_kernel(qk_ref, og_ref, gt_ref, v_ref, cw_ref, cb_ref, gb_ref, g_ref, out_ref,
               buf, c_scr, n_scr, m_scr, *, L, H, dh):
    c = pl.program_id(1)
    W = H * dh

    @pl.when(c == 0)
    def _():
        buf[0:SUBLANES, :] = jnp.zeros((SUBLANES, buf.shape[1]), F32)
        c_scr[...] = jnp.zeros(c_scr.shape, F32)
        n_scr[...] = jnp.zeros(n_scr.shape, F32)
        m_scr[...] = jnp.zeros(m_scr.shape, F32)

    buf[SUBLANES:SUBLANES + L, :] = qk_ref[...]
    cw = cw_ref[...]
    y = cb_ref[...]
    for j in range(CONV_K):
        s0 = SUBLANES - (CONV_K - 1) + j
        y = y + cw[j:j + 1, :] * buf[s0:s0 + L, :]
    buf[0:SUBLANES, :] = buf[L:L + SUBLANES, :]
    qk = y * jax.nn.sigmoid(y)

    G = gt_ref[...] + gb_ref[...]
    ls = _log_sigmoid(G)
    row = lax.broadcasted_iota(jnp.int32, (L, L), 0)
    col = lax.broadcasted_iota(jnp.int32, (L, L), 1)
    causal = col <= row
    tri = causal.astype(F32)
    tri_t = (row <= col).astype(F32)
    bcol_all = jnp.dot(tri, ls, precision=lax.Precision.HIGHEST,
                       preferred_element_type=F32)
    GT = G.T
    brow_all = jnp.dot(ls.T, tri_t, precision=lax.Precision.HIGHEST,
                       preferred_element_type=F32)

    for h in range(H):
        b_col = bcol_all[:, H + h:H + h + 1]
        i_col = G[:, h:h + 1]
        b_row = brow_all[H + h:H + h + 1, :]
        i_row = GT[h:h + 1, :]
        b_last = bcol_all[L - 1:L, H + h:H + h + 1]
        m_old = m_scr[h:h + 1, 0:1]

        a = b_col + m_old
        d = jnp.where(causal, b_col - b_row + i_row, NEG_INF)
        m_t = jnp.maximum(a, jnp.max(d, axis=1, keepdims=True))
        w_inter = jnp.exp(a - m_t)

        q = qk[:, h * dh:(h + 1) * dh]
        k = qk[:, W + h * dh:W + (h + 1) * dh] * (dh ** -0.5)
        qb = q.astype(BF16)
        kb = k.astype(BF16)
        vh = v_ref[:, h * dh:(h + 1) * dh]
        s = lax.dot_general(qb, kb, _NT, preferred_element_type=F32)
        w_intra = jnp.exp(d - m_t) * s
        c_old = c_scr[h]
        n_old = n_scr[h:h + 1, :]
        num = (w_inter * jnp.dot(qb, c_old.astype(BF16), preferred_element_type=F32)
               + jnp.dot(w_intra.astype(BF16), vh, preferred_element_type=F32))
        qn = jnp.sum(q * n_old, axis=1, keepdims=True)
        nq = w_inter * qn + jnp.sum(w_intra, axis=1, keepdims=True)
        hh = num / jnp.maximum(jnp.abs(nq), jnp.exp(-m_t))

        g_col = b_last - b_col + i_col
        g_row = b_last - b_row + i_row
        m_new = jnp.maximum(b_last + m_old, jnp.max(g_row, axis=1, keepdims=True))
        decay = jnp.exp(b_last + m_old - m_new)
        wk = jnp.exp(g_col - m_new) * k
        c_scr[h] = decay * c_old + jnp.dot(wk.T.astype(BF16), vh,
                                           preferred_element_type=F32)
        n_scr[h:h + 1, :] = decay * n_old + jnp.sum(wk, axis=0, keepdims=True)
        m_scr[h:h + 1, :] = jnp.broadcast_to(m_new, (1, m_scr.shape[1]))

        mu = jnp.mean(hh, axis=-1, keepdims=True)
        hc = hh - mu
        var = jnp.mean(hc * hc, axis=-1, keepdims=True)
        hn = hc * lax.rsqrt(var + LN_EPS)
        gate = jax.nn.sigmoid(og_ref[:, h * dh:(h + 1) * dh])
        out_ref[:, h * dh:(h + 1) * dh] = (
            hn * g_ref[:, h * dh:(h + 1) * dh] * gate).astype(out_ref.dtype)


def _mlstm(hf3, hb3, conv_w, conv_b, gbias, ml_g, L):
    B, S, _ = hf3.shape
    W = ml_g.shape[-1]
    H = ML_HEADS
    dh = W // H
    kern = functools.partial(_ml_kernel, L=L, H=H, dh=dh)
    return pl.pallas_call(
        kern,
        grid=(B, S // L),
        in_specs=[pl.BlockSpec((None, L, 2 * W), lambda b, c: (b, c, 0)),
                  pl.BlockSpec((None, L, W), lambda b, c: (b, c, 2)),
                  pl.BlockSpec((None, L, LANES), lambda b, c: (b, c, 3 * W // LANES)),
                  pl.BlockSpec((None, L, W), lambda b, c: (b, c, 3)),
                  pl.BlockSpec((CONV_K, 2 * W), lambda b, c: (0, 0)),
                  pl.BlockSpec((1, 2 * W), lambda b, c: (0, 0)),
                  pl.BlockSpec((1, LANES), lambda b, c: (0, 0)),
                  pl.BlockSpec((1, W), lambda b, c: (0, 0))],
        out_specs=pl.BlockSpec((None, L, W), lambda b, c: (b, c, 0)),
        out_shape=jax.ShapeDtypeStruct((B, S, W), BF16),
        scratch_shapes=[pltpu.VMEM((L + SUBLANES, 2 * W), F32),
                        pltpu.VMEM((H, dh, dh), F32),
                        pltpu.VMEM((SUBLANES, dh), F32),
                        pltpu.VMEM((SUBLANES, LANES), F32)],
        compiler_params=_cparams(("parallel", "arbitrary")),
        name="mlstm",
    )(hf3, hf3, hf3, hb3, conv_w, conv_b, gbias, ml_g)


def _mixmem_kernel(x_ref, da_ref, hm_ref, wout_ref, g1_ref, b1_ref, kv_ref, wq_ref, wo_ref,
                   g2_ref, b2_ref, o_ref, *, alpha, H):
    wd = da_ref.shape[-1]
    mix = (jnp.dot(da_ref[...], wout_ref[0:wd, :], preferred_element_type=F32)
           + jnp.dot(hm_ref[...], wout_ref[wd:, :], preferred_element_type=F32))
    x = _layer_norm(alpha * x_ref[...] + mix, g1_ref[...], b1_ref[...])
    D = x.shape[-1]
    dh = D // H
    q = jnp.dot(x.astype(BF16), wq_ref[...], preferred_element_type=F32) * (dh ** -0.5)
    qb = q.astype(BF16)
    outs = []
    for h in range(H):
        kh = kv_ref[:, h * dh:(h + 1) * dh]
        vh = kv_ref[:, D + h * dh:D + (h + 1) * dh]
        s = lax.dot_general(qb[:, h * dh:(h + 1) * dh], kh, _NT, preferred_element_type=F32)
        s = s - jnp.max(s, axis=-1, keepdims=True)
        e = jnp.exp(s)
        p = e / jnp.sum(e, axis=-1, keepdims=True)
        outs.append(jnp.dot(p.astype(BF16), vh, preferred_element_type=F32))
    o = jnp.concatenate(outs, axis=-1).astype(BF16)
    att = jnp.dot(o, wo_ref[...], preferred_element_type=F32)
    o_ref[...] = _layer_norm(alpha * x + att, g2_ref[...], b2_ref[...])


def _mix_mem_ln(x3, da3, hm3, w_out, g1, b1, kv3, wq, wo, g2, b2, alpha, tm):
    B, S, D = x3.shape
    M = kv3.shape[1]
    wd, wm = da3.shape[-1], hm3.shape[-1]
    kern = functools.partial(_mixmem_kernel, alpha=alpha, H=MEM_HEADS)
    row = pl.BlockSpec((1, D), lambda bb, i: (0, 0))
    return pl.pallas_call(
        kern,
        grid=(B, S // tm),
        in_specs=[pl.BlockSpec((None, tm, D), lambda bb, i: (bb, i, 0)),
                  pl.BlockSpec((None, tm, wd), lambda bb, i: (bb, i, 0)),
                  pl.BlockSpec((None, tm, wm), lambda bb, i: (bb, i, 0)),
                  pl.BlockSpec((wd + wm, D), lambda bb, i: (0, 0)),
                  row, row,
                  pl.BlockSpec((None, M, 2 * D), lambda bb, i: (bb, 0, 0)),
                  pl.BlockSpec((D, D), lambda bb, i: (0, 0)),
                  pl.BlockSpec((D, D), lambda bb, i: (0, 0)),
                  row, row],
        out_specs=pl.BlockSpec((None, tm, D), lambda bb, i: (bb, i, 0)),
        out_shape=jax.ShapeDtypeStruct((B, S, D), F32),
        compiler_params=_cparams(("parallel", "parallel")),
        name="mix_mem_ln",
    )(x3, da3, hm3, w_out, g1, b1, kv3, wq, wo, g2, b2)


def _sort_network(n):
    def merge(lo, hi, r):
        step = r * 2
        if step < hi - lo:
            yield from merge(lo, hi, step)
            yield from merge(lo + r, hi, step)
            yield from [(i, i + r) for i in range(lo + r, hi - r, step)]
        else:
            yield (lo, lo + r)

    def sort(lo, hi):
        if hi - lo >= 1:
            mid = lo + (hi - lo) // 2
            yield from sort(lo, mid)
            yield from sort(mid + 1, hi)
            yield from merge(lo, hi, 1)

    return list(sort(0, n - 1))


def _top_desc_sorted(s, count):
    rows = s.shape[0]
    n = rows // SUBLANES
    lists = [s[g * SUBLANES:(g + 1) * SUBLANES, :] for g in range(n)]
    for i, j in _sort_network(n):
        hi = jnp.maximum(lists[i], lists[j])
        lo = jnp.minimum(lists[i], lists[j])
        lists[i], lists[j] = hi, lo
    tops = []
    for r in range(count):
        mx = jnp.max(lists[0], axis=0, keepdims=True)
        tops.append(mx)
        if r + 1 < count:
            hit = lists[0] == mx
            depth = min(n, count - 1 - r)
            for l in range(depth):
                nxt = lists[l + 1] if l + 1 < n else NEG_INF
                lists[l] = jnp.where(hit, nxt, lists[l])
    return tops


def _extract_desc(v, count):
    tops = []
    for r in range(count):
        mx = jnp.max(v, axis=0, keepdims=True)
        tops.append(mx)
        if r + 1 < count:
            v = jnp.where(v == mx, NEG_INF, v)
    return tops


def _count_greater(x, t):
    assert len(t) == 16
    m8 = t[7] > x
    m4 = jnp.where(m8, t[11], t[3]) > x
    m2 = jnp.where(m8, jnp.where(m4, t[13], t[9]), jnp.where(m4, t[5], t[1])) > x
    th = jnp.where(m8,
                   jnp.where(m4, jnp.where(m2, t[14], t[12]), jnp.where(m2, t[10], t[8])),
                   jnp.where(m4, jnp.where(m2, t[6], t[4]), jnp.where(m2, t[2], t[0])))
    m1 = th > x
    cnt = (jnp.where(m8, 8.0, 0.0) + jnp.where(m4, 4.0, 0.0)
           + jnp.where(m2, 2.0, 0.0) + jnp.where(m1, 1.0, 0.0))
    return cnt + jnp.where(t[15] > x, 1.0, 0.0)


def _dup_bf16_words(x):
    u = lax.bitcast_convert_type(x.astype(BF16).astype(F32), jnp.uint32)
    return u | (u >> 16)


def _route_kernel(x_ref, wpq_ref, sk_ref, c1_ref, e1_ref, r2_ref, e2_ref, q_scr, *, H, nk, topk):
    q_scr[...] = jnp.dot(x_ref[...].astype(BF16), wpq_ref[...], preferred_element_type=F32)
    T = x_ref.shape[0]
    half = sk_ref.shape[-1]
    kk = topk + 1
    sub = lax.broadcasted_iota(jnp.int32, (SUBLANES, T), 0)

    def head(h, carry):
        base = pl.multiple_of(h * 2 * half, 2 * half)
        qa = q_scr[:, pl.ds(base, half)].astype(BF16)
        qb = q_scr[:, pl.ds(base + half, half)].astype(BF16)
        s1 = lax.dot_general(sk_ref[0], qa, _NT, preferred_element_type=F32)
        s2 = lax.dot_general(sk_ref[1], qb, _NT, preferred_element_type=F32)
        a = _top_desc_sorted(s1, kk)
        b = _top_desc_sorted(s2, kk)
        pad = [jnp.full((1, T), NEG_INF, F32)] * ((-kk) % SUBLANES)
        a_arr = jnp.concatenate(a + pad, axis=0)
        b_arr = jnp.concatenate(b + pad, axis=0)
        slabs = []
        for p in range(2):
            for s0 in range(0, kk // (p + 1), SUBLANES):
                slabs.append(a[p] + b_arr[s0:s0 + SUBLANES, :])
        for q in range(2):
            for s0 in range(0, kk // (q + 1), SUBLANES):
                blk = b[q] + a_arr[s0:s0 + SUBLANES, :]
                slabs.append(jnp.where(sub >= 2, blk, NEG_INF) if s0 == 0 else blk)
        rest = [a[p] + b[q] for p in range(2, kk) for q in range(2, kk) if (p + 1) * (q + 1) <= kk]
        rest = rest + [jnp.full((1, T), NEG_INF, F32)] * ((-len(rest)) % SUBLANES)
        slabs.append(jnp.concatenate(rest, axis=0))
        cs = _extract_desc(jnp.concatenate(slabs, axis=0), kk)
        tau = 0.5 * (cs[topk - 1] + cs[topk])
        z = jnp.zeros((1, T), F32)
        for r in range(topk):
            z = z + jnp.exp(cs[r] - cs[0])
        c1_ref[h] = _dup_bf16_words(_count_greater(tau - s1, b[:topk]))
        e1_ref[h] = _dup_bf16_words(jnp.exp(s1 - a[0]))
        r2_ref[h] = _count_greater(s2, b[:topk]).astype(BF16)
        e2_ref[h] = (jnp.exp(s2 - b[0]) / z).astype(BF16)
        return carry

    lax.fori_loop(0, H, head, 0)


def _peer_route(x2d, wpq, sk, tr):
    T, D = x2d.shape
    H = PEER_HEADS
    nk = sk.shape[1]
    kern = functools.partial(_route_kernel, H=H, nk=nk, topk=PEER_TOPK)
    shp_w = jax.ShapeDtypeStruct((H, nk, T), jnp.uint32)
    shp_b = jax.ShapeDtypeStruct((H, nk, T), BF16)
    ospec = pl.BlockSpec((H, nk, tr), lambda i: (0, 0, i))
    return pl.pallas_call(
        kern,
        grid=(T // tr,),
        in_specs=[pl.BlockSpec((tr, D), lambda i: (i, 0)),
                  pl.BlockSpec(wpq.shape, lambda i: (0, 0)),
                  pl.BlockSpec(sk.shape, lambda i: (0, 0, 0))],
        out_specs=[ospec, ospec, ospec, ospec],
        out_shape=[shp_w, shp_w, shp_b, shp_b],
        scratch_shapes=[pltpu.VMEM((tr, wpq.shape[1]), F32)],
        compiler_params=_cparams(("parallel",)),
        name="peer_route",
    )(x2d, wpq, sk)


def _peer_kernel(x_ref, c1_ref, e1_ref, r2_ref, e2_ref, u_ref, vt_ref, g_ref, b_ref, o_ref,
                 xb_scr, act_scr, w_scr, acc_scr, bc1_scr, be1_scr, *, alpha, H, nk, ib, sub):
    step = pl.program_id(1)
    tt = x_ref.shape[0]
    stages = ib // sub
    se = sub * nk
    rc = 2 * SUBLANES
    grp = 4

    @pl.when(step == 0)
    def _():
        xb_scr[...] = x_ref[...].astype(BF16)
        acc_scr[...] = jnp.zeros(acc_scr.shape, F32)

    def act_mm(k):
        act_scr[k] = lax.dot_general(u_ref[k * se:(k + 1) * se, :], xb_scr[...], _NT,
                                     preferred_element_type=F32)

    def out_mm(k):
        acc_scr[...] += jnp.dot(vt_ref[:, k * se:(k + 1) * se], w_scr[k],
                                preferred_element_type=F32)

    def gates(k):
        for half in range(sub):
            ii = k * sub + half
            par = ii % 2
            for h in range(H):
                bc1_scr[par, h] = pltpu.bitcast(
                    jnp.broadcast_to(c1_ref[h, ii:ii + 1, :], (SUBLANES, tt)), BF16)
                be1_scr[par, h] = pltpu.bitcast(
                    jnp.broadcast_to(e1_ref[h, ii:ii + 1, :], (SUBLANES, tt)), BF16)
            for g in range(nk // (grp * rc)):
                accs = [jnp.zeros((rc, tt), BF16) for _ in range(grp)]
                for h in range(H):
                    cb = bc1_scr[par, h]
                    eb = be1_scr[par, h]
                    for c in range(grp):
                        j0 = (g * grp + c) * rc
                        sel = jnp.where(r2_ref[h, j0:j0 + rc, :] < cb,
                                        e2_ref[h, j0:j0 + rc, :], jnp.zeros((), BF16))
                        accs[c] = accs[c] + sel * eb
                for c in range(grp):
                    e0 = half * nk + (g * grp + c) * rc
                    a = act_scr[k, e0:e0 + rc, :]
                    gel = a * (1.0 + lax.erf(a))
                    w_scr[k, e0:e0 + rc, :] = accs[c] * gel.astype(BF16)

    act_mm(0)
    for k in range(stages):
        if k + 1 < stages:
            act_mm(k + 1)
        gates(k)
        if k >= 1:
            out_mm(k - 1)
    out_mm(stages - 1)

    @pl.when(step == pl.num_programs(1) - 1)
    def _():
        y = alpha * x_ref[...] + RSQRT2 * acc_scr[...].T
        o_ref[...] = _layer_norm(y, g_ref[...], b_ref[...])


def _peer_experts_ln(x2d, c1, e1, r2, e2, u_bf, vt_bf, g, b, alpha, tt, ib):
    T, D = x2d.shape
    H, nk, _ = c1.shape
    ne = ib * nk
    steps = nk // ib
    sub = 4
    kern = functools.partial(_peer_kernel, alpha=alpha, H=H, nk=nk, ib=ib, sub=sub)
    return pl.pallas_call(
        kern,
        grid=(T // tt, steps),
        in_specs=[pl.BlockSpec((tt, D), lambda t, s: (t, 0)),
                  pl.BlockSpec((H, ib, tt), lambda t, s: (0, s, t)),
                  pl.BlockSpec((H, ib, tt), lambda t, s: (0, s, t)),
                  pl.BlockSpec((H, nk, tt), lambda t, s: (0, 0, t)),
                  pl.BlockSpec((H, nk, tt), lambda t, s: (0, 0, t)),
                  pl.BlockSpec((ne, D), lambda t, s: (s, 0)),
                  pl.BlockSpec((D, ne), lambda t, s: (0, s)),
                  pl.BlockSpec((1, D), lambda t, s: (0, 0)),
                  pl.BlockSpec((1, D), lambda t, s: (0, 0))],
        out_specs=pl.BlockSpec((tt, D), lambda t, s: (t, 0)),
        out_shape=jax.ShapeDtypeStruct((T, D), F32),
        scratch_shapes=[pltpu.VMEM((tt, D), BF16),
                        pltpu.VMEM((ib // sub, sub * nk, tt), F32),
                        pltpu.VMEM((ib // sub, sub * nk, tt), BF16),
                        pltpu.VMEM((D, tt), F32),
                        pltpu.VMEM((2, H, 2 * SUBLANES, tt), BF16),
                        pltpu.VMEM((2, H, 2 * SUBLANES, tt), BF16)],
        compiler_params=_cparams(("parallel", "arbitrary")),
        name="peer_experts_ln",
    )(x2d, c1, e1, r2, e2, u_bf, vt_bf, g, b)


def _tile(n, pref):
    t = min(n, pref)
    assert n % t == 0, (n, pref)
    return t


def kernel(x, mem, w_in, i_bias, f_bias, conv_w, conv_b, lam_qk, da_norm_g, ml_norm_g, w_out,
           ln1_g, ln1_b, wq_mem, wkv_mem, wo_mem, ln2_g, ln2_b, w_pq, sub_keys, u_tab, v_tab,
           ln3_g, ln3_b):
    B, S, D = x.shape
    depth = w_in.shape[0]
    T = B * S
    M = mem.shape[1]
    alpha = (2.0 * depth) ** 0.25
    wda = DA_HEADS * da_norm_g.shape[-1]
    wml = ml_norm_g.shape[-1]
    assert w_in.shape[-1] == 3 * wda + 4 * wml + 2 * ML_HEADS
    assert wda == wml and wml % LANES == 0 and 2 * ML_HEADS <= LANES

    tm = _tile(T, 512)
    tq = _tile(S, 1024)
    L = _tile(S, 256)
    tmem = _tile(S, 512)
    tr = _tile(T, 256)
    tt = _tile(T, 512)
    ib = 2 * SUBLANES

    mem2d = mem.reshape(B * M, D)
    o_q, o_k, o_v = 0, wda, 2 * wda
    o_mq = 3 * wda
    o_mk, o_mv, o_mo, o_gt = o_mq + wml, o_mq + 2 * wml, o_mq + 3 * wml, o_mq + 4 * wml

    for l in range(depth):
        lam_init = 0.8 - 0.6 * math.exp(-0.3 * l)
        wl = w_in[l]
        wb = jnp.concatenate([wl[:, o_q:o_mq], wl[:, o_mv:o_mo]], axis=1).astype(BF16)
        wf = jnp.concatenate(
            [wl[:, o_mq:o_mv], wl[:, o_mo:o_gt],
             jnp.pad(wl[:, o_gt:], ((0, 0), (0, LANES - 2 * ML_HEADS)))], axis=1).astype(BF16)
        gbias = jnp.pad(jnp.concatenate([i_bias[l], f_bias[l]]),
                        (0, LANES - 2 * ML_HEADS)).reshape(1, LANES)
        lami = jnp.full((1, da_norm_g.shape[-1]), lam_init, F32)

        x2d = x.reshape(T, D)
        hb, hf = _inproj(x2d, wb, wf, tm)
        hb3 = hb.reshape(B, S, -1)
        hf3 = hf.reshape(B, S, -1)
        da = _diff_attention(hb3, lam_qk[l], da_norm_g[l].reshape(1, -1), lami, tq)
        hm = _mlstm(hf3, hb3, conv_w[l], conv_b[l].reshape(1, -1), gbias,
                    ml_norm_g[l].reshape(1, -1), L)
        kv = _matmul(mem2d, wkv_mem[l].astype(BF16), _tile(B * M, 256), BF16)
        x2 = _mix_mem_ln(x, da, hm, w_out[l].astype(BF16),
                         ln1_g[l].reshape(1, D), ln1_b[l].reshape(1, D),
                         kv.reshape(B, M, 2 * D), wq_mem[l].astype(BF16), wo_mem[l].astype(BF16),
                         ln2_g[l].reshape(1, D), ln2_b[l].reshape(1, D), alpha, tmem)
        x2d = x2.reshape(T, D)

        c1, e1, r2, e2 = _peer_route(x2d, w_pq[l].astype(BF16), sub_keys[l].astype(BF16), tr)
        x3 = _peer_experts_ln(x2d, c1, e1, r2, e2, (u_tab[l] * RSQRT2).astype(BF16),
                              v_tab[l].T.astype(BF16), ln3_g[l].reshape(1, D),
                              ln3_b[l].reshape(1, D), alpha, tt, ib)
        x = x3.reshape(B, S, D)
    return x
```

```python
import functools
import math

import jax
import jax.numpy as jnp
from jax import lax
from jax.experimental import pallas as pl
from jax.experimental.pallas import tpu as pltpu

F32 = jnp.float32
BF16 = jnp.bfloat16
LN_EPS = 1e-5
NEG_INF = float("-inf")
RSQRT2 = 2.0 ** -0.5

DA_HEADS = 4
ML_HEADS = 4
MEM_HEADS = 4
PEER_HEADS = 8
PEER_TOPK = 16
CONV_K = 4

VMEM_LIMIT_BYTES = 56 * 1024 * 1024
LANES = 128
SUBLANES = 8

ROWS_PROJ = 512
ROWS_ATTN = 1024
ROWS_CHUNK = 256
ROWS_ROUTE = 512
TOKENS_EXPERTS = 512
IBLOCKS_EXPERTS = 16

_NT = (((1,), (1,)), ((), ()))


def _cparams(sem):
    return pltpu.CompilerParams(dimension_semantics=sem, vmem_limit_bytes=VMEM_LIMIT_BYTES)


def _layer_norm(y, g, b):
    mu = jnp.mean(y, axis=-1, keepdims=True)
    yc = y - mu
    var = jnp.mean(yc * yc, axis=-1, keepdims=True)
    return yc * lax.rsqrt(var + LN_EPS) * g + b


def _inproj_kernel(x_ref, wb_ref, wf_ref, hb_ref, hf_ref):
    xb = x_ref[...].astype(BF16)
    hb_ref[...] = jnp.dot(xb, wb_ref[...], preferred_element_type=F32).astype(BF16)
    hf_ref[...] = jnp.dot(xb, wf_ref[...], preferred_element_type=F32)


def _inproj(x2d, wb, wf, tm):
    T, D = x2d.shape
    nb, nf = wb.shape[1], wf.shape[1]
    return pl.pallas_call(
        _inproj_kernel,
        grid=(T // tm,),
        in_specs=[pl.BlockSpec((tm, D), lambda i: (i, 0)),
                  pl.BlockSpec((D, nb), lambda i: (0, 0)),
                  pl.BlockSpec((D, nf), lambda i: (0, 0))],
        out_specs=[pl.BlockSpec((tm, nb), lambda i: (i, 0)),
                   pl.BlockSpec((tm, nf), lambda i: (i, 0))],
        out_shape=[jax.ShapeDtypeStruct((T, nb), BF16),
                   jax.ShapeDtypeStruct((T, nf), F32)],
        compiler_params=_cparams(("parallel",)),
        name="inproj",
    )(x2d, wb, wf)


def _mm_kernel(x_ref, w_ref, o_ref):
    o_ref[...] = jnp.dot(x_ref[...].astype(BF16), w_ref[...],
                         preferred_element_type=F32).astype(o_ref.dtype)


def _matmul(x2d, w, tm, out_dtype):
    M, K = x2d.shape
    N = w.shape[1]
    return pl.pallas_call(
        _mm_kernel,
        grid=(M // tm,),
        in_specs=[pl.BlockSpec((tm, K), lambda i: (i, 0)),
                  pl.BlockSpec((K, N), lambda i: (0, 0))],
        out_specs=pl.BlockSpec((tm, N), lambda i: (i, 0)),
        out_shape=jax.ShapeDtypeStruct((M, N), out_dtype),
        compiler_params=_cparams(("parallel",)),
        name="matmul",
    )(x2d, w)


def _da_kernel(q_ref, k_ref, v_ref, lq_ref, g_ref, lami_ref, o_ref,
               m1, a1, m2, a2, *, tq, dk):
    qb = pl.program_id(2)
    dv = q_ref.shape[-1]
    rep = tq // LANES

    lane = lax.broadcasted_iota(jnp.int32, (1, dv), 1)
    qf = q_ref[...].astype(F32) * (dk ** -0.5 * math.log2(math.e))
    q1 = jnp.where(lane < dk, qf, 0.0).astype(BF16)
    q2 = jnp.where(lane >= dk, qf, 0.0).astype(BF16)
    ones = jnp.ones((tq, dv), BF16)

    m1[...] = jnp.full(m1.shape, NEG_INF, F32)
    m2[...] = jnp.full(m2.shape, NEG_INF, F32)
    a1[...] = jnp.zeros(a1.shape, F32)
    a2[...] = jnp.zeros(a2.shape, F32)

    def update(s, vones, m_ref, a_ref):
        m_prev = m_ref[...]
        m_cur = jnp.max(s, axis=1, keepdims=True)
        m_next = jnp.maximum(m_prev, m_cur)
        p = jnp.exp2(s - jnp.concatenate([m_next] * rep, axis=1))
        alpha = jnp.exp2(m_prev - m_next)
        a_ref[...] = (jnp.concatenate([alpha, alpha], axis=1) * a_ref[...]
                      + jnp.dot(p.astype(BF16), vones, preferred_element_type=F32))
        m_ref[...] = m_next

    def block(kb, masked):
        off = pl.multiple_of(kb * tq, tq)
        kblk = k_ref[pl.ds(off, tq), :]
        vones = jnp.concatenate([v_ref[pl.ds(off, tq), :], ones], axis=1)
        s1 = lax.dot_general(q1, kblk, _NT, preferred_element_type=F32)
        s2 = lax.dot_general(q2, kblk, _NT, preferred_element_type=F32)
        if masked:
            row = lax.broadcasted_iota(jnp.int32, (tq, tq), 0)
            col = lax.broadcasted_iota(jnp.int32, (tq, tq), 1)
            keep = col <= row
            s1 = jnp.where(keep, s1, NEG_INF)
            s2 = jnp.where(keep, s2, NEG_INF)
        update(s1, vones, m1, a1)
        update(s2, vones, m2, a2)

    def body(kb, carry):
        block(kb, False)
        return carry

    lax.fori_loop(0, qb, body, 0)
    block(qb, True)

    lq = lq_ref[...]
    lam_init = lami_ref[...][:, :1]
    lam = (jnp.exp(jnp.sum(lq[0:1] * lq[1:2], axis=1, keepdims=True))
           - jnp.exp(jnp.sum(lq[2:3] * lq[3:4], axis=1, keepdims=True)) + lam_init)
    o = a1[:, :dv] / a1[:, dv:] - lam * (a2[:, :dv] / a2[:, dv:])
    o = o * lax.rsqrt(jnp.mean(o * o, axis=-1, keepdims=True) + LN_EPS)
    o = o * g_ref[...] * (1.0 - lami_ref[...])
    o_ref[...] = o.astype(o_ref.dtype)


def _diff_attention(hb3, lam_qk, da_g, lami, tq):
    B, S, _ = hb3.shape
    dv = da_g.shape[-1]
    dk = dv // 2
    H = DA_HEADS
    assert dv == LANES
    kern = functools.partial(_da_kernel, tq=tq, dk=dk)
    return pl.pallas_call(
        kern,
        grid=(B, H, S // tq),
        in_specs=[pl.BlockSpec((None, tq, dv), lambda b, h, i: (b, i, h)),
                  pl.BlockSpec((None, S, dv), lambda b, h, i: (b, 0, H + h)),
                  pl.BlockSpec((None, S, dv), lambda b, h, i: (b, 0, 2 * H + h)),
                  pl.BlockSpec(lam_qk.shape, lambda b, h, i: (0, 0)),
                  pl.BlockSpec((1, dv), lambda b, h, i: (0, 0)),
                  pl.BlockSpec((1, dv), lambda b, h, i: (0, 0))],
        out_specs=pl.BlockSpec((None, tq, dv), lambda b, h, i: (b, i, h)),
        out_shape=jax.ShapeDtypeStruct((B, S, H * dv), BF16),
        scratch_shapes=[pltpu.VMEM((tq, LANES), F32), pltpu.VMEM((tq, 2 * dv), F32),
                        pltpu.VMEM((tq, LANES), F32), pltpu.VMEM((tq, 2 * dv), F32)],
        compiler_params=_cparams(("parallel", "parallel", "arbitrary")),
        name="diff_attention",
    )(hb3, hb3, hb3, lam_qk, da_g, lami)


def _log_sigmoid(x):
    return -(jnp.maximum(-x, 0.0) + jnp.log1p(jnp.exp(-jnp.abs(x))))


def _ml_kernel(qk_ref, og_ref, gt_ref, v_ref, cw_ref, cb_ref, gb_ref, g_ref, out_ref,
               buf, c_scr, n_scr, m_scr, *, L, H, dh):
    c = pl.program_id(1)
    W = H * dh

    @pl.when(c == 0)
    def _():
        buf[0:SUBLANES, :] = jnp.zeros((SUBLANES, buf.shape[1]), F32)
        c_scr[...] = jnp.zeros(c_scr.shape, F32)
        n_scr[...] = jnp.zeros(n_scr.shape, F32)
        m_scr[...] = jnp.zeros(m_scr.shape, F32)

    buf[SUBLANES:SUBLANES + L, :] = qk_ref[...]
    cw = cw_ref[...]
    y = cb_ref[...]
    for j in range(CONV_K):
        s0 = SUBLANES - (CONV_K - 1) + j
        y = y + cw[j:j + 1, :] * buf[s0:s0 + L, :]
    buf[0:SUBLANES, :] = buf[L:L + SUBLANES, :]
    qk = y * jax.nn.sigmoid(y)

    G = gt_ref[...] + gb_ref[...]
    ls = _log_sigmoid(G)
    row = lax.broadcasted_iota(jnp.int32, (L, L), 0)
    col = lax.broadcasted_iota(jnp.int32, (L, L), 1)
    causal = col <= row
    tri = causal.astype(F32)
    tri_t = (row <= col).astype(F32)
    bcol_all = jnp.dot(tri, ls, precision=lax.Precision.HIGHEST,
                       preferred_element_type=F32)
    GT = G.T
    brow_all = jnp.dot(ls.T, tri_t, precision=lax.Precision.HIGHEST,
                       preferred_element_type=F32)

    for h in range(H):
        b_col = bcol_all[:, H + h:H + h + 1]
        i_col = G[:, h:h + 1]
        b_row = brow_all[H + h:H + h + 1, :]
        i_row = GT[h:h + 1, :]
        b_last = bcol_all[L - 1:L, H + h:H + h + 1]
        m_old = m_scr[h:h + 1, 0:1]

        a = b_col + m_old
        d = jnp.where(causal, b_col - b_row + i_row, NEG_INF)
        m_t = jnp.maximum(a, jnp.max(d, axis=1, keepdims=True))
        w_inter = jnp.exp(a - m_t)

        q = qk[:, h * dh:(h + 1) * dh]
        k = qk[:, W + h * dh:W + (h + 1) * dh] * (dh ** -0.5)
        qb = q.astype(BF16)
        kb = k.astype(BF16)
        vh = v_ref[:, h * dh:(h + 1) * dh]
        s = lax.dot_general(qb, kb, _NT, preferred_element_type=F32)
        w_intra = jnp.exp(d - m_t) * s
        c_old = c_scr[h]
        n_old = n_scr[h:h + 1, :]
        num = (w_inter * jnp.dot(qb, c_old.astype(BF16), preferred_element_type=F32)
               + jnp.dot(w_intra.astype(BF16), vh, preferred_element_type=F32))
        qn = jnp.sum(q * n_old, axis=1, keepdims=True)
        nq = w_inter * qn + jnp.sum(w_intra, axis=1, keepdims=True)
        hh = num / jnp.maximum(jnp.abs(nq), jnp.exp(-m_t))

        g_col = b_last - b_col + i_col
        g_row = b_last - b_row + i_row
        m_new = jnp.maximum(b_last + m_old, jnp.max(g_row, axis=1, keepdims=True))
        decay = jnp.exp(b_last + m_old - m_new)
        wk = jnp.exp(g_col - m_new) * k
        c_scr[h] = decay * c_old + jnp.dot(wk.T.astype(BF16), vh,
                                           preferred_element_type=F32)
        n_scr[h:h + 1, :] = decay * n_old + jnp.sum(wk, axis=0, keepdims=True)
        m_scr[h:h + 1, :] = jnp.broadcast_to(m_new, (1, m_scr.shape[1]))

        mu = jnp.mean(hh, axis=-1, keepdims=True)
        hc = hh - mu
        var = jnp.mean(hc * hc, axis=-1, keepdims=True)
        hn = hc * lax.rsqrt(var + LN_EPS)
        gate = jax.nn.sigmoid(og_ref[:, h * dh:(h + 1) * dh])
        out_ref[:, h * dh:(h + 1) * dh] = (
            hn * g_ref[:, h * dh:(h + 1) * dh] * gate).astype(out_ref.dtype)


def _mlstm(hf3, hb3, conv_w, conv_b, gbias, ml_g, L):
    B, S, _ = hf3.shape
    W = ml_g.shape[-1]
    H = ML_HEADS
    dh = W // H
    kern = functools.partial(_ml_kernel, L=L, H=H, dh=dh)
    return pl.pallas_call(
        kern,
        grid=(B, S // L),
        in_specs=[pl.BlockSpec((None, L, 2 * W), lambda b, c: (b, c, 0)),
                  pl.BlockSpec((None, L, W), lambda b, c: (b, c, 2)),
                  pl.BlockSpec((None, L, LANES), lambda b, c: (b, c, 3 * W // LANES)),
                  pl.BlockSpec((None, L, W), lambda b, c: (b, c, 3)),
                  pl.BlockSpec((CONV_K, 2 * W), lambda b, c: (0, 0)),
                  pl.BlockSpec((1, 2 * W), lambda b, c: (0, 0)),
                  pl.BlockSpec((1, LANES), lambda b, c: (0, 0)),
                  pl.BlockSpec((1, W), lambda b, c: (0, 0))],
        out_specs=pl.BlockSpec((None, L, W), lambda b, c: (b, c, 0)),
        out_shape=jax.ShapeDtypeStruct((B, S, W), BF16),
        scratch_shapes=[pltpu.VMEM((L + SUBLANES, 2 * W), F32),
                        pltpu.VMEM((H, dh, dh), F32),
                        pltpu.VMEM((SUBLANES, dh), F32),
                        pltpu.VMEM((SUBLANES, LANES), F32)],
        compiler_params=_cparams(("parallel", "arbitrary")),
        name="mlstm",
    )(hf3, hf3, hf3, hb3, conv_w, conv_b, gbias, ml_g)


def _mixmem_kernel(x_ref, da_ref, hm_ref, wout_ref, g1_ref, b1_ref, kv_ref, wq_ref, wo_ref,
                   g2_ref, b2_ref, o_ref, *, alpha, H):
    wd = da_ref.shape[-1]
    mix = (jnp.dot(da_ref[...], wout_ref[0:wd, :], preferred_element_type=F32)
           + jnp.dot(hm_ref[...], wout_ref[wd:, :], preferred_element_type=F32))
    x = _layer_norm(alpha * x_ref[...] + mix, g1_ref[...], b1_ref[...])
    D = x.shape[-1]
    dh = D // H
    q = jnp.dot(x.astype(BF16), wq_ref[...], preferred_element_type=F32) * (dh ** -0.5)
    qb = q.astype(BF16)
    outs = []
    for h in range(H):
        kh = kv_ref[:, h * dh:(h + 1) * dh]
        vh = kv_ref[:, D + h * dh:D + (h + 1) * dh]
        s = lax.dot_general(qb[:, h * dh:(h + 1) * dh], kh, _NT, preferred_element_type=F32)
        s = s - jnp.max(s, axis=-1, keepdims=True)
        e = jnp.exp(s)
        p = e / jnp.sum(e, axis=-1, keepdims=True)
        outs.append(jnp.dot(p.astype(BF16), vh, preferred_element_type=F32))
    o = jnp.concatenate(outs, axis=-1).astype(BF16)
    att = jnp.dot(o, wo_ref[...], preferred_element_type=F32)
    o_ref[...] = _layer_norm(alpha * x + att, g2_ref[...], b2_ref[...])


def _mix_mem_ln(x3, da3, hm3, w_out, g1, b1, kv3, wq, wo, g2, b2, alpha, tm):
    B, S, D = x3.shape
    M = kv3.shape[1]
    wd, wm = da3.shape[-1], hm3.shape[-1]
    kern = functools.partial(_mixmem_kernel, alpha=alpha, H=MEM_HEADS)
    row = pl.BlockSpec((1, D), lambda bb, i: (0, 0))
    return pl.pallas_call(
        kern,
        grid=(B, S // tm),
        in_specs=[pl.BlockSpec((None, tm, D), lambda bb, i: (bb, i, 0)),
                  pl.BlockSpec((None, tm, wd), lambda bb, i: (bb, i, 0)),
                  pl.BlockSpec((None, tm, wm), lambda bb, i: (bb, i, 0)),
                  pl.BlockSpec((wd + wm, D), lambda bb, i: (0, 0)),
                  row, row,
                  pl.BlockSpec((None, M, 2 * D), lambda bb, i: (bb, 0, 0)),
                  pl.BlockSpec((D, D), lambda bb, i: (0, 0)),
                  pl.BlockSpec((D, D), lambda bb, i: (0, 0)),
                  row, row],
        out_specs=pl.BlockSpec((None, tm, D), lambda bb, i: (bb, i, 0)),
        out_shape=jax.ShapeDtypeStruct((B, S, D), F32),
        compiler_params=_cparams(("parallel", "parallel")),
        name="mix_mem_ln",
    )(x3, da3, hm3, w_out, g1, b1, kv3, wq, wo, g2, b2)


def _sort_network(n):
    def merge(lo, hi, r):
        step = r * 2
        if step < hi - lo:
            yield from merge(lo, hi, step)
            yield from merge(lo + r, hi, step)
            yield from [(i, i + r) for i in range(lo + r, hi - r, step)]
        else:
            yield (lo, lo + r)

    def sort(lo, hi):
        if hi - lo >= 1:
            mid = lo + (hi - lo) // 2
            yield from sort(lo, mid)
            yield from sort(mid + 1, hi)
            yield from merge(lo, hi, 1)

    return list(sort(0, n - 1))


def _top_desc_sorted(s, count):
    rows = s.shape[0]
    n = rows // SUBLANES
    lists = [s[g * SUBLANES:(g + 1) * SUBLANES, :] for g in range(n)]
    for i, j in _sort_network(n):
        hi = jnp.maximum(lists[i], lists[j])
        lo = jnp.minimum(lists[i], lists[j])
        lists[i], lists[j] = hi, lo
    tops = []
    for r in range(count):
        mx = jnp.max(lists[0], axis=0, keepdims=True)
        tops.append(mx)
        if r + 1 < count:
            hit = lists[0] == mx
            depth = min(n, count - 1 - r)
            for l in range(depth):
                nxt = lists[l + 1] if l + 1 < n else NEG_INF
                lists[l] = jnp.where(hit, nxt, lists[l])
    return tops


def _extract_desc(v, count):
    tops = []
    for r in range(count):
        mx = jnp.max(v, axis=0, keepdims=True)
        tops.append(mx)
        if r + 1 < count:
            v = jnp.where(v == mx, NEG_INF, v)
    return tops


def _count_greater(x, t):
    assert len(t) == 16
    m8 = t[7] > x
    m4 = jnp.where(m8, t[11], t[3]) > x
    m2 = jnp.where(m8, jnp.where(m4, t[13], t[9]), jnp.where(m4, t[5], t[1])) > x
    th = jnp.where(m8,
                   jnp.where(m4, jnp.where(m2, t[14], t[12]), jnp.where(m2, t[10], t[8])),
                   jnp.where(m4, jnp.where(m2, t[6], t[4]), jnp.where(m2, t[2], t[0])))
    m1 = th > x
    cnt = (jnp.where(m8, 8.0, 0.0) + jnp.where(m4, 4.0, 0.0)
           + jnp.where(m2, 2.0, 0.0) + jnp.where(m1, 1.0, 0.0))
    return cnt + jnp.where(t[15] > x, 1.0, 0.0)


def _dup_bf16_words(x):
    u = lax.bitcast_convert_type(x.astype(BF16).astype(F32), jnp.uint32)
    return u | (u >> 16)


def _route_kernel(x_ref, wpq_ref, sk_ref, c1_ref, e1_ref, r2_ref, e2_ref, q_scr, *, H, nk, topk):
    q_scr[...] = jnp.dot(x_ref[...].astype(BF16), wpq_ref[...], preferred_element_type=F32)
    T = x_ref.shape[0]
    half = sk_ref.shape[-1]
    kk = topk + 1
    sub = lax.broadcasted_iota(jnp.int32, (SUBLANES, T), 0)

    def head(h, carry):
        base = pl.multiple_of(h * 2 * half, 2 * half)
        qa = q_scr[:, pl.ds(base, half)].astype(BF16)
        qb = q_scr[:, pl.ds(base + half, half)].astype(BF16)
        s1 = lax.dot_general(sk_ref[0], qa, _NT, preferred_element_type=F32)
        s2 = lax.dot_general(sk_ref[1], qb, _NT, preferred_element_type=F32)
        a = _top_desc_sorted(s1, kk)
        b = _top_desc_sorted(s2, kk)
        pad = [jnp.full((1, T), NEG_INF, F32)] * ((-kk) % SUBLANES)
        a_arr = jnp.concatenate(a + pad, axis=0)
        b_arr = jnp.concatenate(b + pad, axis=0)
        slabs = []
        for p in range(2):
            for s0 in range(0, kk // (p + 1), SUBLANES):
                slabs.append(a[p] + b_arr[s0:s0 + SUBLANES, :])
        for q in range(2):
            for s0 in range(0, kk // (q + 1), SUBLANES):
                blk = b[q] + a_arr[s0:s0 + SUBLANES, :]
                slabs.append(jnp.where(sub >= 2, blk, NEG_INF) if s0 == 0 else blk)
        rest = [a[p] + b[q] for p in range(2, kk) for q in range(2, kk) if (p + 1) * (q + 1) <= kk]
        rest = rest + [jnp.full((1, T), NEG_INF, F32)] * ((-len(rest)) % SUBLANES)
        slabs.append(jnp.concatenate(rest, axis=0))
        cs = _extract_desc(jnp.concatenate(slabs, axis=0), kk)
        tau = 0.5 * (cs[topk - 1] + cs[topk])
        z = jnp.zeros((1, T), F32)
        for r in range(topk):
            z = z + jnp.exp(cs[r] - cs[0])
        c1_ref[h] = _dup_bf16_words(_count_greater(tau - s1, b[:topk]))
        e1_ref[h] = _dup_bf16_words(jnp.exp(s1 - a[0]))
        r2_ref[h] = _count_greater(s2, b[:topk]).astype(BF16)
        e2_ref[h] = (jnp.exp(s2 - b[0]) / z).astype(BF16)
        return carry

    lax.fori_loop(0, H, head, 0)


def _peer_route(x2d, wpq, sk, tr):
    T, D = x2d.shape
    H = PEER_HEADS
    nk = sk.shape[1]
    kern = functools.partial(_route_kernel, H=H, nk=nk, topk=PEER_TOPK)
    shp_w = jax.ShapeDtypeStruct((H, nk, T), jnp.uint32)
    shp_b = jax.ShapeDtypeStruct((H, nk, T), BF16)
    ospec = pl.BlockSpec((H, nk, tr), lambda i: (0, 0, i))
    return pl.pallas_call(
        kern,
        grid=(T // tr,),
        in_specs=[pl.BlockSpec((tr, D), lambda i: (i, 0)),
                  pl.BlockSpec(wpq.shape, lambda i: (0, 0)),
                  pl.BlockSpec(sk.shape, lambda i: (0, 0, 0))],
        out_specs=[ospec, ospec, ospec, ospec],
        out_shape=[shp_w, shp_w, shp_b, shp_b],
        scratch_shapes=[pltpu.VMEM((tr, wpq.shape[1]), F32)],
        compiler_params=_cparams(("parallel",)),
        name="peer_route",
    )(x2d, wpq, sk)


def _peer_kernel(x_ref, c1_ref, e1_ref, r2_ref, e2_ref, u_ref, vt_ref, g_ref, b_ref, o_ref,
                 xb_scr, act_scr, w_scr, acc_scr, bc1_scr, be1_scr, *, alpha, H, nk, ib, sub):
    step = pl.program_id(1)
    tt = x_ref.shape[0]
    stages = ib // sub
    se = sub * nk
    rc = 2 * SUBLANES
    grp = 4

    @pl.when(step == 0)
    def _():
        xb_scr[...] = x_ref[...].astype(BF16)
        acc_scr[...] = jnp.zeros(acc_scr.shape, F32)

    def act_mm(k):
        act_scr[k] = lax.dot_general(u_ref[k * se:(k + 1) * se, :], xb_scr[...], _NT,
                                     preferred_element_type=F32)

    def out_mm(k):
        acc_scr[...] += jnp.dot(vt_ref[:, k * se:(k + 1) * se], w_scr[k],
                                preferred_element_type=F32)

    def gates(k):
        for half in range(sub):
            ii = k * sub + half
            par = ii % 2
            for h in range(H):
                bc1_scr[par, h] = pltpu.bitcast(
                    jnp.broadcast_to(c1_ref[h, ii:ii + 1, :], (SUBLANES, tt)), BF16)
                be1_scr[par, h] = pltpu.bitcast(
                    jnp.broadcast_to(e1_ref[h, ii:ii + 1, :], (SUBLANES, tt)), BF16)
            for g in range(nk // (grp * rc)):
                accs = [jnp.zeros((rc, tt), BF16) for _ in range(grp)]
                for h in range(H):
                    cb = bc1_scr[par, h]
                    eb = be1_scr[par, h]
                    for c in range(grp):
                        j0 = (g * grp + c) * rc
                        sel = jnp.where(r2_ref[h, j0:j0 + rc, :] < cb,
                                        e2_ref[h, j0:j0 + rc, :], jnp.zeros((), BF16))
                        accs[c] = accs[c] + sel * eb
                for c in range(grp):
                    e0 = half * nk + (g * grp + c) * rc
                    a = act_scr[k, e0:e0 + rc, :]
                    gel = a * (1.0 + lax.erf(a))
                    w_scr[k, e0:e0 + rc, :] = accs[c] * gel.astype(BF16)

    act_mm(0)
    for k in range(stages):
        if k + 1 < stages:
            act_mm(k + 1)
        gates(k)
        if k >= 1:
            out_mm(k - 1)
    out_mm(stages - 1)

    @pl.when(step == pl.num_programs(1) - 1)
    def _():
        y = alpha * x_ref[...] + RSQRT2 * acc_scr[...].T
        o_ref[...] = _layer_norm(y, g_ref[...], b_ref[...])


def _peer_experts_ln(x2d, c1, e1, r2, e2, u_bf, vt_bf, g, b, alpha, tt, ib):
    T, D = x2d.shape
    H, nk, _ = c1.shape
    ne = ib * nk
    steps = nk // ib
    sub = 4
    kern = functools.partial(_peer_kernel, alpha=alpha, H=H, nk=nk, ib=ib, sub=sub)
    return pl.pallas_call(
        kern,
        grid=(T // tt, steps),
        in_specs=[pl.BlockSpec((tt, D), lambda t, s: (t, 0)),
                  pl.BlockSpec((H, ib, tt), lambda t, s: (0, s, t)),
                  pl.BlockSpec((H, ib, tt), lambda t, s: (0, s, t)),
                  pl.BlockSpec((H, nk, tt), lambda t, s: (0, 0, t)),
                  pl.BlockSpec((H, nk, tt), lambda t, s: (0, 0, t)),
                  pl.BlockSpec((ne, D), lambda t, s: (s, 0)),
                  pl.BlockSpec((D, ne), lambda t, s: (0, s)),
                  pl.BlockSpec((1, D), lambda t, s: (0, 0)),
                  pl.BlockSpec((1, D), lambda t, s: (0, 0))],
        out_specs=pl.BlockSpec((tt, D), lambda t, s: (t, 0)),
        out_shape=jax.ShapeDtypeStruct((T, D), F32),
        scratch_shapes=[pltpu.VMEM((tt, D), BF16),
                        pltpu.VMEM((ib // sub, sub * nk, tt), F32),
                        pltpu.VMEM((ib // sub, sub * nk, tt), BF16),
                        pltpu.VMEM((D, tt), F32),
                        pltpu.VMEM((2, H, 2 * SUBLANES, tt), BF16),
                        pltpu.VMEM((2, H, 2 * SUBLANES, tt), BF16)],
        compiler_params=_cparams(("parallel", "arbitrary")),
        name="peer_experts_ln",
    )(x2d, c1, e1, r2, e2, u_bf, vt_bf, g, b)


def _tile(n, pref):
    t = min(n, pref)
    assert n % t == 0, (n, pref)
    return t


def kernel(x, mem, w_in, i_bias, f_bias, conv_w, conv_b, lam_qk, da_norm_g, ml_norm_g, w_out,
           ln1_g, ln1_b, wq_mem, wkv_mem, wo_mem, ln2_g, ln2_b, w_pq, sub_keys, u_tab, v_tab,
           ln3_g, ln3_b):
    B, S, D = x.shape
    depth = w_in.shape[0]
    T = B * S
    M = mem.shape[1]
    alpha = (2.0 * depth) ** 0.25
    wda = DA_HEADS * da_norm_g.shape[-1]
    wml = ml_norm_g.shape[-1]
    assert w_in.shape[-1] == 3 * wda + 4 * wml + 2 * ML_HEADS
    assert wda == wml and wml % LANES == 0 and 2 * ML_HEADS <= LANES

    tm = _tile(T, ROWS_PROJ)
    tq = _tile(S, ROWS_ATTN)
    L = _tile(S, ROWS_CHUNK)
    tmem = _tile(S, ROWS_PROJ)
    tr = _tile(T, ROWS_ROUTE)
    tt = _tile(T, TOKENS_EXPERTS)
    ib = IBLOCKS_EXPERTS

    mem2d = mem.reshape(B * M, D)
    o_q, o_k, o_v = 0, wda, 2 * wda
    o_mq = 3 * wda
    o_mk, o_mv, o_mo, o_gt = o_mq + wml, o_mq + 2 * wml, o_mq + 3 * wml, o_mq + 4 * wml

    for l in range(depth):
        lam_init = 0.8 - 0.6 * math.exp(-0.3 * l)
        wl = w_in[l]
        wb = jnp.concatenate([wl[:, o_q:o_mq], wl[:, o_mv:o_mo]], axis=1).astype(BF16)
        wf = jnp.concatenate(
            [wl[:, o_mq:o_mv], wl[:, o_mo:o_gt],
             jnp.pad(wl[:, o_gt:], ((0, 0), (0, LANES - 2 * ML_HEADS)))], axis=1).astype(BF16)
        gbias = jnp.pad(jnp.concatenate([i_bias[l], f_bias[l]]),
                        (0, LANES - 2 * ML_HEADS)).reshape(1, LANES)
        lami = jnp.full((1, da_norm_g.shape[-1]), lam_init, F32)

        x2d = x.reshape(T, D)
        hb, hf = _inproj(x2d, wb, wf, tm)
        hb3 = hb.reshape(B, S, -1)
        hf3 = hf.reshape(B, S, -1)
        da = _diff_attention(hb3, lam_qk[l], da_norm_g[l].reshape(1, -1), lami, tq)
        hm = _mlstm(hf3, hb3, conv_w[l], conv_b[l].reshape(1, -1), gbias,
                    ml_norm_g[l].reshape(1, -1), L)
        kv = _matmul(mem2d, wkv_mem[l].astype(BF16), _tile(B * M, 256), BF16)
        x2 = _mix_mem_ln(x, da, hm, w_out[l].astype(BF16),
                         ln1_g[l].reshape(1, D), ln1_b[l].reshape(1, D),
                         kv.reshape(B, M, 2 * D), wq_mem[l].astype(BF16), wo_mem[l].astype(BF16),
                         ln2_g[l].reshape(1, D), ln2_b[l].reshape(1, D), alpha, tmem)
        x2d = x2.reshape(T, D)

        c1, e1, r2, e2 = _peer_route(x2d, w_pq[l].astype(BF16), sub_keys[l].astype(BF16), tr)
        x3 = _peer_experts_ln(x2d, c1, e1, r2, e2, (u_tab[l] * RSQRT2).astype(BF16),
                              v_tab[l].T.astype(BF16), ln3_g[l].reshape(1, D),
                              ln3_b[l].reshape(1, D), alpha, tt, ib)
        x = x3.reshape(B, S, D)
    return x
```

```python
import functools
import math

import jax
import jax.numpy as jnp
from jax import lax
from jax.experimental import pallas as pl
from jax.experimental.pallas import tpu as pltpu

F32 = jnp.float32
BF16 = jnp.bfloat16
LN_EPS = 1e-5
NEG_INF = float("-inf")
RSQRT2 = 2.0 ** -0.5

DA_HEADS = 4
ML_HEADS = 4
MEM_HEADS = 4
PEER_HEADS = 8
PEER_TOPK = 16
CONV_K = 4

VMEM_LIMIT_BYTES = 56 * 1024 * 1024
LANES = 128
SUBLANES = 8

ROWS_PROJ = 512
ROWS_ATTN = 1024
ROWS_CHUNK = 256
ROWS_ROUTE = 512
TOKENS_EXPERTS = 512
IBLOCKS_EXPERTS = 16

_NT = (((1,), (1,)), ((), ()))


def _cparams(sem):
    return pltpu.CompilerParams(dimension_semantics=sem, vmem_limit_bytes=VMEM_LIMIT_BYTES)


def _layer_norm(y, g, b):
    mu = jnp.mean(y, axis=-1, keepdims=True)
    yc = y - mu
    var = jnp.mean(yc * yc, axis=-1, keepdims=True)
    return yc * lax.rsqrt(var + LN_EPS) * g + b


def _inproj_kernel(x_ref, wb_ref, wf_ref, hb_ref, hf_ref):
    xb = x_ref[...].astype(BF16)
    hb_ref[...] = jnp.dot(xb, wb_ref[...], preferred_element_type=F32).astype(BF16)
    hf_ref[...] = jnp.dot(xb, wf_ref[...], preferred_element_type=F32)


def _inproj(x2d, wb, wf, tm):
    T, D = x2d.shape
    nb, nf = wb.shape[1], wf.shape[1]
    return pl.pallas_call(
        _inproj_kernel,
        grid=(T // tm,),
        in_specs=[pl.BlockSpec((tm, D), lambda i: (i, 0)),
                  pl.BlockSpec((D, nb), lambda i: (0, 0)),
                  pl.BlockSpec((D, nf), lambda i: (0, 0))],
        out_specs=[pl.BlockSpec((tm, nb), lambda i: (i, 0)),
                   pl.BlockSpec((tm, nf), lambda i: (i, 0))],
        out_shape=[jax.ShapeDtypeStruct((T, nb), BF16),
                   jax.ShapeDtypeStruct((T, nf), F32)],
        compiler_params=_cparams(("parallel",)),
        name="inproj",
    )(x2d, wb, wf)


def _mm_kernel(x_ref, w_ref, o_ref):
    o_ref[...] = jnp.dot(x_ref[...].astype(BF16), w_ref[...],
                         preferred_element_type=F32).astype(o_ref.dtype)


def _matmul(x2d, w, tm, out_dtype):
    M, K = x2d.shape
    N = w.shape[1]
    return pl.pallas_call(
        _mm_kernel,
        grid=(M // tm,),
        in_specs=[pl.BlockSpec((tm, K), lambda i: (i, 0)),
                  pl.BlockSpec((K, N), lambda i: (0, 0))],
        out_specs=pl.BlockSpec((tm, N), lambda i: (i, 0)),
        out_shape=jax.ShapeDtypeStruct((M, N), out_dtype),
        compiler_params=_cparams(("parallel",)),
        name="matmul",
    )(x2d, w)


def _da_kernel(q_ref, k_ref, v_ref, lq_ref, g_ref, lami_ref, o_ref,
               m1, a1, m2, a2, *, tq, dk):
    qb = pl.program_id(2)
    dv = q_ref.shape[-1]
    rep = tq // LANES

    lane = lax.broadcasted_iota(jnp.int32, (1, dv), 1)
    qf = q_ref[...].astype(F32) * (dk ** -0.5 * math.log2(math.e))
    q1 = jnp.where(lane < dk, qf, 0.0).astype(BF16)
    q2 = jnp.where(lane >= dk, qf, 0.0).astype(BF16)
    ones = jnp.ones((tq, dv), BF16)

    m1[...] = jnp.full(m1.shape, NEG_INF, F32)
    m2[...] = jnp.full(m2.shape, NEG_INF, F32)
    a1[...] = jnp.zeros(a1.shape, F32)
    a2[...] = jnp.zeros(a2.shape, F32)

    def update(s, vones, m_ref, a_ref):
        m_prev = m_ref[...]
        m_cur = jnp.max(s, axis=1, keepdims=True)
        m_next = jnp.maximum(m_prev, m_cur)
        p = jnp.exp2(s - jnp.concatenate([m_next] * rep, axis=1))
        alpha = jnp.exp2(m_prev - m_next)
        a_ref[...] = (jnp.concatenate([alpha, alpha], axis=1) * a_ref[...]
                      + jnp.dot(p.astype(BF16), vones, preferred_element_type=F32))
        m_ref[...] = m_next

    def block(kb, masked):
        off = pl.multiple_of(kb * tq, tq)
        kblk = k_ref[pl.ds(off, tq), :]
        vones = jnp.concatenate([v_ref[pl.ds(off, tq), :], ones], axis=1)
        s1 = lax.dot_general(q1, kblk, _NT, preferred_element_type=F32)
        s2 = lax.dot_general(q2, kblk, _NT, preferred_element_type=F32)
        if masked:
            row = lax.broadcasted_iota(jnp.int32, (tq, tq), 0)
            col = lax.broadcasted_iota(jnp.int32, (tq, tq), 1)
            keep = col <= row
            s1 = jnp.where(keep, s1, NEG_INF)
            s2 = jnp.where(keep, s2, NEG_INF)
        update(s1, vones, m1, a1)
        update(s2, vones, m2, a2)

    def body(kb, carry):
        block(kb, False)
        return carry

    lax.fori_loop(0, qb, body, 0)
    block(qb, True)

    lq = lq_ref[...]
    lam_init = lami_ref[...][:, :1]
    lam = (jnp.exp(jnp.sum(lq[0:1] * lq[1:2], axis=1, keepdims=True))
           - jnp.exp(jnp.sum(lq[2:3] * lq[3:4], axis=1, keepdims=True)) + lam_init)
    o = a1[:, :dv] / a1[:, dv:] - lam * (a2[:, :dv] / a2[:, dv:])
    o = o * lax.rsqrt(jnp.mean(o * o, axis=-1, keepdims=True) + LN_EPS)
    o = o * g_ref[...] * (1.0 - lami_ref[...])
    o_ref[...] = o.astype(o_ref.dtype)


def _diff_attention(hb3, lam_qk, da_g, lami, tq):
    B, S, _ = hb3.shape
    dv = da_g.shape[-1]
    dk = dv // 2
    H = DA_HEADS
    assert dv == LANES
    kern = functools.partial(_da_kernel, tq=tq, dk=dk)
    return pl.pallas_call(
        kern,
        grid=(B, H, S // tq),
        in_specs=[pl.BlockSpec((None, tq, dv), lambda b, h, i: (b, i, h)),
                  pl.BlockSpec((None, S, dv), lambda b, h, i: (b, 0, H + h)),
                  pl.BlockSpec((None, S, dv), lambda b, h, i: (b, 0, 2 * H + h)),
                  pl.BlockSpec(lam_qk.shape, lambda b, h, i: (0, 0)),
                  pl.BlockSpec((1, dv), lambda b, h, i: (0, 0)),
                  pl.BlockSpec((1, dv), lambda b, h, i: (0, 0))],
        out_specs=pl.BlockSpec((None, tq, dv), lambda b, h, i: (b, i, h)),
        out_shape=jax.ShapeDtypeStruct((B, S, H * dv), BF16),
        scratch_shapes=[pltpu.VMEM((tq, LANES), F32), pltpu.VMEM((tq, 2 * dv), F32),
                        pltpu.VMEM((tq, LANES), F32), pltpu.VMEM((tq, 2 * dv), F32)],
        compiler_params=_cparams(("parallel", "parallel", "arbitrary")),
        name="diff_attention",
    )(hb3, hb3, hb3, lam_qk, da_g, lami)


def _log_sigmoid(x):
    return -(jnp.maximum(-x, 0.0) + jnp.log1p(jnp.exp(-jnp.abs(x))))


def _ml_kernel(qk_ref, og_ref, gt_ref, v_ref, cw_ref, cb_ref, gb_ref, g_ref, out_ref,
               buf, c_scr, n_scr, m_scr, *, L, H, dh):
    c = pl.program_id(1)
    W = H * dh

    @pl.when(c == 0)
    def _():
        buf[0:SUBLANES, :] = jnp.zeros((SUBLANES, buf.shape[1]), F32)
        c_scr[...] = jnp.zeros(c_scr.shape, F32)
        n_scr[...] = jnp.zeros(n_scr.shape, F32)
        m_scr[...] = jnp.zeros(m_scr.shape, F32)

    buf[SUBLANES:SUBLANES + L, :] = qk_ref[...]
    cw = cw_ref[...]
    y = cb_ref[...]
    for j in range(CONV_K):
        s0 = SUBLANES - (CONV_K - 1) + j
        y = y + cw[j:j + 1, :] * buf[s0:s0 + L, :]
    buf[0:SUBLANES, :] = buf[L:L + SUBLANES, :]
    qk = y * jax.nn.sigmoid(y)

    G = gt_ref[...] + gb_ref[...]
    ls = _log_sigmoid(G)
    row = lax.broadcasted_iota(jnp.int32, (L, L), 0)
    col = lax.broadcasted_iota(jnp.int32, (L, L), 1)
    causal = col <= row
    tri = causal.astype(F32)
    tri_t = (row <= col).astype(F32)
    bcol_all = jnp.dot(tri, ls, precision=lax.Precision.HIGHEST,
                       preferred_element_type=F32)
    GT = G.T
    brow_all = jnp.dot(ls.T, tri_t, precision=lax.Precision.HIGHEST,
                       preferred_element_type=F32)

    for h in range(H):
        b_col = bcol_all[:, H + h:H + h + 1]
        i_col = G[:, h:h + 1]
        b_row = brow_all[H + h:H + h + 1, :]
        i_row = GT[h:h + 1, :]
        b_last = bcol_all[L - 1:L, H + h:H + h + 1]
        m_old = m_scr[h:h + 1, 0:1]

        a = b_col + m_old
        d = jnp.where(causal, b_col - b_row + i_row, NEG_INF)
        m_t = jnp.maximum(a, jnp.max(d, axis=1, keepdims=True))
        w_inter = jnp.exp(a - m_t)

        q = qk[:, h * dh:(h + 1) * dh]
        k = qk[:, W + h * dh:W + (h + 1) * dh] * (dh ** -0.5)
        qb = q.astype(BF16)
        kb = k.astype(BF16)
        vh = v_ref[:, h * dh:(h + 1) * dh]
        s = lax.dot_general(qb, kb, _NT, preferred_element_type=F32)
        w_intra = jnp.exp(d - m_t) * s
        c_old = c_scr[h]
        n_old = n_scr[h:h + 1, :]
        vones = jnp.concatenate([vh, jnp.ones((L, dh), BF16)], axis=1)
        intra = jnp.dot(w_intra.astype(BF16), vones, preferred_element_type=F32)
        num = (w_inter * jnp.dot(qb, c_old.astype(BF16), preferred_element_type=F32)
               + intra[:, :dh])
        qn = jnp.sum(q * n_old, axis=1, keepdims=True)
        nq = w_inter * qn + intra[:, dh:]
        hh = num / jnp.maximum(jnp.abs(nq), jnp.exp(-m_t))

        g_col = b_last - b_col + i_col
        g_row = b_last - b_row + i_row
        m_new = jnp.maximum(b_last + m_old, jnp.max(g_row, axis=1, keepdims=True))
        decay = jnp.exp(b_last + m_old - m_new)
        wk = jnp.exp(g_col - m_new) * k
        c_scr[h] = decay * c_old + jnp.dot(wk.T.astype(BF16), vh,
                                           preferred_element_type=F32)
        n_scr[h:h + 1, :] = decay * n_old + jnp.sum(wk, axis=0, keepdims=True)
        m_scr[h:h + 1, :] = jnp.broadcast_to(m_new, (1, m_scr.shape[1]))

        mu = jnp.mean(hh, axis=-1, keepdims=True)
        hc = hh - mu
        var = jnp.mean(hc * hc, axis=-1, keepdims=True)
        hn = hc * lax.rsqrt(var + LN_EPS)
        gate = jax.nn.sigmoid(og_ref[:, h * dh:(h + 1) * dh])
        out_ref[:, h * dh:(h + 1) * dh] = (
            hn * g_ref[:, h * dh:(h + 1) * dh] * gate).astype(out_ref.dtype)


def _mlstm(hf3, hb3, conv_w, conv_b, gbias, ml_g, L):
    B, S, _ = hf3.shape
    W = ml_g.shape[-1]
    H = ML_HEADS
    dh = W // H
    kern = functools.partial(_ml_kernel, L=L, H=H, dh=dh)
    return pl.pallas_call(
        kern,
        grid=(B, S // L),
        in_specs=[pl.BlockSpec((None, L, 2 * W), lambda b, c: (b, c, 0)),
                  pl.BlockSpec((None, L, W), lambda b, c: (b, c, 2)),
                  pl.BlockSpec((None, L, LANES), lambda b, c: (b, c, 3 * W // LANES)),
                  pl.BlockSpec((None, L, W), lambda b, c: (b, c, 3)),
                  pl.BlockSpec((CONV_K, 2 * W), lambda b, c: (0, 0)),
                  pl.BlockSpec((1, 2 * W), lambda b, c: (0, 0)),
                  pl.BlockSpec((1, LANES), lambda b, c: (0, 0)),
                  pl.BlockSpec((1, W), lambda b, c: (0, 0))],
        out_specs=pl.BlockSpec((None, L, W), lambda b, c: (b, c, 0)),
        out_shape=jax.ShapeDtypeStruct((B, S, W), BF16),
        scratch_shapes=[pltpu.VMEM((L + SUBLANES, 2 * W), F32),
                        pltpu.VMEM((H, dh, dh), F32),
                        pltpu.VMEM((SUBLANES, dh), F32),
                        pltpu.VMEM((SUBLANES, LANES), F32)],
        compiler_params=_cparams(("parallel", "arbitrary")),
        name="mlstm",
    )(hf3, hf3, hf3, hb3, conv_w, conv_b, gbias, ml_g)


def _mixmem_kernel(x_ref, da_ref, hm_ref, wout_ref, g1_ref, b1_ref, kv_ref, wq_ref, wo_ref,
                   g2_ref, b2_ref, o_ref, *, alpha, H):
    wd = da_ref.shape[-1]
    mix = (jnp.dot(da_ref[...], wout_ref[0:wd, :], preferred_element_type=F32)
           + jnp.dot(hm_ref[...], wout_ref[wd:, :], preferred_element_type=F32))
    x = _layer_norm(alpha * x_ref[...] + mix, g1_ref[...], b1_ref[...])
    D = x.shape[-1]
    dh = D // H
    q = jnp.dot(x.astype(BF16), wq_ref[...], preferred_element_type=F32) * (dh ** -0.5)
    qb = q.astype(BF16)
    outs = []
    for h in range(H):
        kh = kv_ref[:, h * dh:(h + 1) * dh]
        vh = kv_ref[:, D + h * dh:D + (h + 1) * dh]
        s = lax.dot_general(qb[:, h * dh:(h + 1) * dh], kh, _NT, preferred_element_type=F32)
        s = s - jnp.max(s, axis=-1, keepdims=True)
        e = jnp.exp(s)
        p = e / jnp.sum(e, axis=-1, keepdims=True)
        outs.append(jnp.dot(p.astype(BF16), vh, preferred_element_type=F32))
    o = jnp.concatenate(outs, axis=-1).astype(BF16)
    att = jnp.dot(o, wo_ref[...], preferred_element_type=F32)
    o_ref[...] = _layer_norm(alpha * x + att, g2_ref[...], b2_ref[...])


def _mix_mem_ln(x3, da3, hm3, w_out, g1, b1, kv3, wq, wo, g2, b2, alpha, tm):
    B, S, D = x3.shape
    M = kv3.shape[1]
    wd, wm = da3.shape[-1], hm3.shape[-1]
    kern = functools.partial(_mixmem_kernel, alpha=alpha, H=MEM_HEADS)
    row = pl.BlockSpec((1, D), lambda bb, i: (0, 0))
    return pl.pallas_call(
        kern,
        grid=(B, S // tm),
        in_specs=[pl.BlockSpec((None, tm, D), lambda bb, i: (bb, i, 0)),
                  pl.BlockSpec((None, tm, wd), lambda bb, i: (bb, i, 0)),
                  pl.BlockSpec((None, tm, wm), lambda bb, i: (bb, i, 0)),
                  pl.BlockSpec((wd + wm, D), lambda bb, i: (0, 0)),
                  row, row,
                  pl.BlockSpec((None, M, 2 * D), lambda bb, i: (bb, 0, 0)),
                  pl.BlockSpec((D, D), lambda bb, i: (0, 0)),
                  pl.BlockSpec((D, D), lambda bb, i: (0, 0)),
                  row, row],
        out_specs=pl.BlockSpec((None, tm, D), lambda bb, i: (bb, i, 0)),
        out_shape=jax.ShapeDtypeStruct((B, S, D), F32),
        compiler_params=_cparams(("parallel", "parallel")),
        name="mix_mem_ln",
    )(x3, da3, hm3, w_out, g1, b1, kv3, wq, wo, g2, b2)


def _sort_network(n):
    def merge(lo, hi, r):
        step = r * 2
        if step < hi - lo:
            yield from merge(lo, hi, step)
            yield from merge(lo + r, hi, step)
            yield from [(i, i + r) for i in range(lo + r, hi - r, step)]
        else:
            yield (lo, lo + r)

    def sort(lo, hi):
        if hi - lo >= 1:
            mid = lo + (hi - lo) // 2
            yield from sort(lo, mid)
            yield from sort(mid + 1, hi)
            yield from merge(lo, hi, 1)

    return list(sort(0, n - 1))


def _top_desc_sorted(s, count):
    rows = s.shape[0]
    n = rows // SUBLANES
    lists = [s[g * SUBLANES:(g + 1) * SUBLANES, :] for g in range(n)]
    for i, j in _sort_network(n):
        hi = jnp.maximum(lists[i], lists[j])
        lo = jnp.minimum(lists[i], lists[j])
        lists[i], lists[j] = hi, lo
    tops = []
    for r in range(count):
        mx = jnp.max(lists[0], axis=0, keepdims=True)
        tops.append(mx)
        if r + 1 < count:
            hit = lists[0] == mx
            depth = min(n, count - 1 - r)
            for l in range(depth):
                nxt = lists[l + 1] if l + 1 < n else NEG_INF
                lists[l] = jnp.where(hit, nxt, lists[l])
    return tops


def _extract_desc(v, count):
    tops = []
    for r in range(count):
        mx = jnp.max(v, axis=0, keepdims=True)
        tops.append(mx)
        if r + 1 < count:
            v = jnp.where(v == mx, NEG_INF, v)
    return tops


def _count_greater(x, t):
    assert len(t) == 16
    m8 = t[7] > x
    m4 = jnp.where(m8, t[11], t[3]) > x
    m2 = jnp.where(m8, jnp.where(m4, t[13], t[9]), jnp.where(m4, t[5], t[1])) > x
    th = jnp.where(m8,
                   jnp.where(m4, jnp.where(m2, t[14], t[12]), jnp.where(m2, t[10], t[8])),
                   jnp.where(m4, jnp.where(m2, t[6], t[4]), jnp.where(m2, t[2], t[0])))
    m1 = th > x
    cnt = (jnp.where(m8, 8.0, 0.0) + jnp.where(m4, 4.0, 0.0)
           + jnp.where(m2, 2.0, 0.0) + jnp.where(m1, 1.0, 0.0))
    return cnt + jnp.where(t[15] > x, 1.0, 0.0)


def _dup_bf16_words(x):
    u = lax.bitcast_convert_type(x.astype(BF16).astype(F32), jnp.uint32)
    return u | (u >> 16)


def _route_kernel(x_ref, wpq_ref, sk_ref, c1_ref, e1_ref, r2_ref, e2_ref, q_scr, *, H, nk, topk):
    q_scr[...] = jnp.dot(x_ref[...].astype(BF16), wpq_ref[...], preferred_element_type=F32)
    T = x_ref.shape[0]
    half = sk_ref.shape[-1]
    kk = topk + 1
    sub = lax.broadcasted_iota(jnp.int32, (SUBLANES, T), 0)

    def head(h, carry):
        base = pl.multiple_of(h * 2 * half, 2 * half)
        qa = q_scr[:, pl.ds(base, half)].astype(BF16)
        qb = q_scr[:, pl.ds(base + half, half)].astype(BF16)
        s1 = lax.dot_general(sk_ref[0], qa, _NT, preferred_element_type=F32)
        s2 = lax.dot_general(sk_ref[1], qb, _NT, preferred_element_type=F32)
        a = _top_desc_sorted(s1, kk)
        b = _top_desc_sorted(s2, kk)
        pad = [jnp.full((1, T), NEG_INF, F32)] * ((-kk) % SUBLANES)
        a_arr = jnp.concatenate(a + pad, axis=0)
        b_arr = jnp.concatenate(b + pad, axis=0)
        slabs = []
        for p in range(2):
            for s0 in range(0, kk // (p + 1), SUBLANES):
                slabs.append(a[p] + b_arr[s0:s0 + SUBLANES, :])
        for q in range(2):
            for s0 in range(0, kk // (q + 1), SUBLANES):
                blk = b[q] + a_arr[s0:s0 + SUBLANES, :]
                slabs.append(jnp.where(sub >= 2, blk, NEG_INF) if s0 == 0 else blk)
        rest = [a[p] + b[q] for p in range(2, kk) for q in range(2, kk) if (p + 1) * (q + 1) <= kk]
        rest = rest + [jnp.full((1, T), NEG_INF, F32)] * ((-len(rest)) % SUBLANES)
        slabs.append(jnp.concatenate(rest, axis=0))
        cs = _extract_desc(jnp.concatenate(slabs, axis=0), kk)
        tau = 0.5 * (cs[topk - 1] + cs[topk])
        z = jnp.zeros((1, T), F32)
        for r in range(topk):
            z = z + jnp.exp(cs[r] - cs[0])
        c1_ref[h] = _dup_bf16_words(_count_greater(tau - s1, b[:topk]))
        e1_ref[h] = _dup_bf16_words(jnp.exp(s1 - a[0]))
        r2_ref[h] = _count_greater(s2, b[:topk]).astype(BF16)
        e2_ref[h] = (jnp.exp(s2 - b[0]) / z).astype(BF16)
        return carry

    lax.fori_loop(0, H, head, 0)


def _peer_route(x2d, wpq, sk, tr):
    T, D = x2d.shape
    H = PEER_HEADS
    nk = sk.shape[1]
    kern = functools.partial(_route_kernel, H=H, nk=nk, topk=PEER_TOPK)
    shp_w = jax.ShapeDtypeStruct((H, nk, T), jnp.uint32)
    shp_b = jax.ShapeDtypeStruct((H, nk, T), BF16)
    ospec = pl.BlockSpec((H, nk, tr), lambda i: (0, 0, i))
    return pl.pallas_call(
        kern,
        grid=(T // tr,),
        in_specs=[pl.BlockSpec((tr, D), lambda i: (i, 0)),
                  pl.BlockSpec(wpq.shape, lambda i: (0, 0)),
                  pl.BlockSpec(sk.shape, lambda i: (0, 0, 0))],
        out_specs=[ospec, ospec, ospec, ospec],
        out_shape=[shp_w, shp_w, shp_b, shp_b],
        scratch_shapes=[pltpu.VMEM((tr, wpq.shape[1]), F32)],
        compiler_params=_cparams(("parallel",)),
        name="peer_route",
    )(x2d, wpq, sk)


def _peer_kernel(x_ref, c1_ref, e1_ref, r2_ref, e2_ref, u_ref, vt_ref, g_ref, b_ref, o_ref,
                 xb_scr, act_scr, w_scr, acc_scr, bc1_scr, be1_scr, *, alpha, H, nk, ib, sub):
    step = pl.program_id(1)
    tt = x_ref.shape[0]
    stages = ib // sub
    se = sub * nk
    rc = 2 * SUBLANES
    grp = 4

    @pl.when(step == 0)
    def _():
        xb_scr[...] = x_ref[...].astype(BF16)
        acc_scr[...] = jnp.zeros(acc_scr.shape, F32)

    def act_mm(k):
        act_scr[k] = lax.dot_general(u_ref[k * se:(k + 1) * se, :], xb_scr[...], _NT,
                                     preferred_element_type=F32)

    def out_mm(k):
        acc_scr[...] += jnp.dot(vt_ref[:, k * se:(k + 1) * se], w_scr[k],
                                preferred_element_type=F32)

    def gates(k):
        for half in range(sub):
            ii = k * sub + half
            par = ii % 2
            for h in range(H):
                bc1_scr[par, h] = pltpu.bitcast(
                    jnp.broadcast_to(c1_ref[h, ii:ii + 1, :], (SUBLANES, tt)), BF16)
                be1_scr[par, h] = pltpu.bitcast(
                    jnp.broadcast_to(e1_ref[h, ii:ii + 1, :], (SUBLANES, tt)), BF16)
            for g in range(nk // (grp * rc)):
                accs = [jnp.zeros((rc, tt), BF16) for _ in range(grp)]
                for h in range(H):
                    cb = bc1_scr[par, h]
                    eb = be1_scr[par, h]
                    for c in range(grp):
                        j0 = (g * grp + c) * rc
                        sel = jnp.where(r2_ref[h, j0:j0 + rc, :] < cb,
                                        e2_ref[h, j0:j0 + rc, :], jnp.zeros((), BF16))
                        accs[c] = accs[c] + sel * eb
                for c in range(grp):
                    e0 = half * nk + (g * grp + c) * rc
                    a = act_scr[k, e0:e0 + rc, :]
                    gel = a * (1.0 + lax.erf(a))
                    w_scr[k, e0:e0 + rc, :] = accs[c] * gel.astype(BF16)

    act_mm(0)
    for k in range(stages):
        if k + 1 < stages:
            act_mm(k + 1)
        gates(k)
        if k >= 1:
            out_mm(k - 1)
    out_mm(stages - 1)

    @pl.when(step == pl.num_programs(1) - 1)
    def _():
        y = alpha * x_ref[...] + RSQRT2 * acc_scr[...].T
        o_ref[...] = _layer_norm(y, g_ref[...], b_ref[...])


def _peer_experts_ln(x2d, c1, e1, r2, e2, u_bf, vt_bf, g, b, alpha, tt, ib):
    T, D = x2d.shape
    H, nk, _ = c1.shape
    ne = ib * nk
    steps = nk // ib
    sub = 4
    kern = functools.partial(_peer_kernel, alpha=alpha, H=H, nk=nk, ib=ib, sub=sub)
    return pl.pallas_call(
        kern,
        grid=(T // tt, steps),
        in_specs=[pl.BlockSpec((tt, D), lambda t, s: (t, 0)),
                  pl.BlockSpec((H, ib, tt), lambda t, s: (0, s, t)),
                  pl.BlockSpec((H, ib, tt), lambda t, s: (0, s, t)),
                  pl.BlockSpec((H, nk, tt), lambda t, s: (0, 0, t)),
                  pl.BlockSpec((H, nk, tt), lambda t, s: (0, 0, t)),
                  pl.BlockSpec((ne, D), lambda t, s: (s, 0)),
                  pl.BlockSpec((D, ne), lambda t, s: (0, s)),
                  pl.BlockSpec((1, D), lambda t, s: (0, 0)),
                  pl.BlockSpec((1, D), lambda t, s: (0, 0))],
        out_specs=pl.BlockSpec((tt, D), lambda t, s: (t, 0)),
        out_shape=jax.ShapeDtypeStruct((T, D), F32),
        scratch_shapes=[pltpu.VMEM((tt, D), BF16),
                        pltpu.VMEM((ib // sub, sub * nk, tt), F32),
                        pltpu.VMEM((ib // sub, sub * nk, tt), BF16),
                        pltpu.VMEM((D, tt), F32),
                        pltpu.VMEM((2, H, 2 * SUBLANES, tt), BF16),
                        pltpu.VMEM((2, H, 2 * SUBLANES, tt), BF16)],
        compiler_params=_cparams(("parallel", "arbitrary")),
        name="peer_experts_ln",
    )(x2d, c1, e1, r2, e2, u_bf, vt_bf, g, b)


def _tile(n, pref):
    t = min(n, pref)
    assert n % t == 0, (n, pref)
    return t


def kernel(x, mem, w_in, i_bias, f_bias, conv_w, conv_b, lam_qk, da_norm_g, ml_norm_g, w_out,
           ln1_g, ln1_b, wq_mem, wkv_mem, wo_mem, ln2_g, ln2_b, w_pq, sub_keys, u_tab, v_tab,
           ln3_g, ln3_b):
    B, S, D = x.shape
    depth = w_in.shape[0]
    T = B * S
    M = mem.shape[1]
    alpha = (2.0 * depth) ** 0.25
    wda = DA_HEADS * da_norm_g.shape[-1]
    wml = ml_norm_g.shape[-1]
    assert w_in.shape[-1] == 3 * wda + 4 * wml + 2 * ML_HEADS
    assert wda == wml and wml % LANES == 0 and 2 * ML_HEADS <= LANES

    tm = _tile(T, ROWS_PROJ)
    tq = _tile(S, ROWS_ATTN)
    L = _tile(S, ROWS_CHUNK)
    tmem = _tile(S, ROWS_PROJ)
    tr = _tile(T, ROWS_ROUTE)
    tt = _tile(T, TOKENS_EXPERTS)
    ib = IBLOCKS_EXPERTS

    mem2d = mem.reshape(B * M, D)
    o_q, o_k, o_v = 0, wda, 2 * wda
    o_mq = 3 * wda
    o_mk, o_mv, o_mo, o_gt = o_mq + wml, o_mq + 2 * wml, o_mq + 3 * wml, o_mq + 4 * wml

    for l in range(depth):
        lam_init = 0.8 - 0.6 * math.exp(-0.3 * l)
        wl = w_in[l]
        wb = jnp.concatenate([wl[:, o_q:o_mq], wl[:, o_mv:o_mo]], axis=1).astype(BF16)
        wf = jnp.concatenate(
            [wl[:, o_mq:o_mv], wl[:, o_mo:o_gt],
             jnp.pad(wl[:, o_gt:], ((0, 0), (0, LANES - 2 * ML_HEADS)))], axis=1).astype(BF16)
        gbias = jnp.pad(jnp.concatenate([i_bias[l], f_bias[l]]),
                        (0, LANES - 2 * ML_HEADS)).reshape(1, LANES)
        lami = jnp.full((1, da_norm_g.shape[-1]), lam_init, F32)

        x2d = x.reshape(T, D)
        hb, hf = _inproj(x2d, wb, wf, tm)
        hb3 = hb.reshape(B, S, -1)
        hf3 = hf.reshape(B, S, -1)
        da = _diff_attention(hb3, lam_qk[l], da_norm_g[l].reshape(1, -1), lami, tq)
        hm = _mlstm(hf3, hb3, conv_w[l], conv_b[l].reshape(1, -1), gbias,
                    ml_norm_g[l].reshape(1, -1), L)
        kv = _matmul(mem2d, wkv_mem[l].astype(BF16), _tile(B * M, 256), BF16)
        x2 = _mix_mem_ln(x, da, hm, w_out[l].astype(BF16),
                         ln1_g[l].reshape(1, D), ln1_b[l].reshape(1, D),
                         kv.reshape(B, M, 2 * D), wq_mem[l].astype(BF16), wo_mem[l].astype(BF16),
                         ln2_g[l].reshape(1, D), ln2_b[l].reshape(1, D), alpha, tmem)
        x2d = x2.reshape(T, D)

        c1, e1, r2, e2 = _peer_route(x2d, w_pq[l].astype(BF16), sub_keys[l].astype(BF16), tr)
        x3 = _peer_experts_ln(x2d, c1, e1, r2, e2, (u_tab[l] * RSQRT2).astype(BF16),
                              v_tab[l].T.astype(BF16), ln3_g[l].reshape(1, D),
                              ln3_b[l].reshape(1, D), alpha, tt, ib)
        x = x3.reshape(B, S, D)
    return x
```
